```python
import math
import jax
import jax.numpy as jnp
from jax import lax
import numpy as np

D_MODEL = 1024
BATCH = 8
SEQ = 4096
DEPTH = 4
DEC_BATCH = 32
DEC_SEQ = 2048
PAST_LEN = 128

N_MIXERS = 3
HG_HEADS = 8
HG_DK = D_MODEL // HG_HEADS
HG_DV = D_MODEL // HG_HEADS
HG_CHUNK = 64
DA_PATTERNS = ((128, 1), (512, 4), (2048, 16))
DA_HEADS = 16
DA_HEAD_DIM = D_MODEL // DA_HEADS
DA_BLOCK = 64
ROPE_THETA = 10000.0
S5_GROUP = 16
S5_GROUPS = D_MODEL // S5_GROUP
S5_STATE = 64
N_EXPERTS = 16
D_EXPERT = 2048
CAPACITY_FACTOR = 2
EPS = 1e-6
N_LAYERS_A = (DEPTH + 2) // 3
N_LAYERS_B = (DEPTH + 1) // 3
N_LAYERS_C = DEPTH // 3

kernel_name = 'hybrid_bidir_encoder_two_groups'


def rms_norm(x, g):
    xf = x.astype(jnp.float32)
    y = xf * lax.rsqrt(jnp.mean(xf * xf, axis=-1, keepdims=True) + EPS)
    return (y * g.astype(jnp.float32)).astype(x.dtype)


def rope(x, pos):
    half = x.shape[-1] // 2
    inv = ROPE_THETA ** (-jnp.arange(half, dtype=jnp.float32) / half)
    ang = pos.astype(jnp.float32)[:, None] * inv[None, :]
    cos, sin = jnp.cos(ang), jnp.sin(ang)
    x1, x2 = x[..., :half], x[..., half:]
    return jnp.concatenate([x1 * cos - x2 * sin, x2 * cos + x1 * sin], axis=-1)


def chunked_gated_recurrence(q, k, v, logf):
    Bsz, H, T, K = q.shape
    V = v.shape[-1]
    C = min(HG_CHUNK, T)
    N = T // C
    r = lambda z: z.reshape(Bsz, H, N, C, z.shape[-1])
    q, k, v, logf = r(q), r(k), r(v), r(logf)
    b = jnp.cumsum(logf, axis=3)
    mid = C // 2
    b_ref = b[:, :, :, mid - 1:mid, :]
    scores = jnp.einsum('bhnik,bhnjk->bhnij', q * jnp.exp(b - b_ref), k * jnp.exp(b_ref - b))
    lower = jnp.tril(jnp.ones((C, C), dtype=bool))
    scores = jnp.where(lower, scores, 0.0)
    o_intra = jnp.einsum('bhnij,bhnjv->bhniv', scores, v)
    b_last = b[:, :, :, -1:, :]
    kv_chunk = jnp.einsum('bhnjk,bhnjv->bhnkv', k * jnp.exp(b_last - b), v)
    decay = jnp.exp(b_last[:, :, :, 0, :])

    def step(S, inp):
        dec, kv = inp
        return dec[..., None] * S + kv, S

    S0 = jnp.zeros((Bsz, H, K, V), jnp.float32)
    _, S_prev = lax.scan(step, S0, (jnp.moveaxis(decay, 2, 0), jnp.moveaxis(kv_chunk, 2, 0)))
    S_prev = jnp.moveaxis(S_prev, 0, 2)
    o_inter = jnp.einsum('bhnik,bhnkv->bhniv', q * jnp.exp(b), S_prev)
    return (o_intra + o_inter).reshape(Bsz, H, T, V)


def hgrn2_mixer(h, w_in, w_out, norm_g, lb):
    Bsz, T, _ = h.shape
    q, i, ff, fb, g = jnp.split(h @ w_in, 5, axis=-1)

    def heads(z):
        return z.reshape(Bsz, T, HG_HEADS, -1).transpose(0, 2, 1, 3).astype(jnp.float32)

    lbf = lb.astype(jnp.float32).reshape(HG_HEADS, 1, HG_DK)

    def gates(fz):
        z = heads(fz)
        logf = jnp.logaddexp(jnp.log(lbf), jnp.log1p(-lbf) + jax.nn.log_sigmoid(z))
        return logf, (1.0 - lbf) * jax.nn.sigmoid(-z)

    qh, vh = heads(q), heads(i)
    logf_f, k_f = gates(ff)
    logf_b, k_b = gates(fb)
    o_fwd = chunked_gated_recurrence(qh, k_f, vh, logf_f)
    flip = lambda z: jnp.flip(z, axis=2)
    o_bwd = flip(chunked_gated_recurrence(flip(qh), flip(k_b), flip(vh), flip(logf_b)))
    o = o_fwd + o_bwd
    o = o * lax.rsqrt(jnp.mean(o * o, axis=-1, keepdims=True) + EPS) * norm_g.astype(jnp.float32)
    o = o.transpose(0, 2, 1, 3).reshape(Bsz, T, D_MODEL) * jax.nn.sigmoid(g.astype(jnp.float32))
    return o.astype(h.dtype) @ w_out


def dilated_window_attention(q, k, v, dilation, half):
    Bsz, H, T, dh = q.shape
    L = T // dilation
    qb = math.gcd(L, DA_BLOCK)
    nb = L // qb
    span = qb + 2 * half

    def by_residue(z):
        return z.reshape(Bsz, H, L, dilation, dh).transpose(0, 1, 3, 2, 4)

    qr, kr, vr = by_residue(q), by_residue(k), by_residue(v)
    pad = ((0, 0), (0, 0), (0, 0), (half, half), (0, 0))
    kp, vp = jnp.pad(kr, pad), jnp.pad(vr, pad)
    idx = jnp.arange(nb)[:, None] * qb + jnp.arange(span)[None, :]
    kb = kp[:, :, :, idx]
    vb = vp[:, :, :, idx]
    qblk = qr.reshape(Bsz, H, dilation, nb, qb, dh)
    s = jnp.einsum('bhrnid,bhrnjd->bhrnij', qblk, kb) * (dh ** -0.5)
    rel = jnp.arange(span)[None, :] - half - jnp.arange(qb)[:, None]
    key_pos = idx - half
    valid = (jnp.abs(rel)[None] <= half) & (key_pos[:, None, :] >= 0) & (key_pos[:, None, :] < L)
    s = jnp.where(valid, s, -jnp.inf)
    m = jnp.max(s, axis=-1, keepdims=True)
    p = jnp.exp(s - m)
    l = jnp.sum(p, axis=-1, keepdims=True)
    o = jnp.einsum('bhrnij,bhrnjd->bhrnid', p, vb) / l
    lse = (m + jnp.log(l))[..., 0]
    o = o.reshape(Bsz, H, dilation, L, dh).transpose(0, 1, 3, 2, 4).reshape(Bsz, H, T, dh)
    lse = lse.reshape(Bsz, H, dilation, L).transpose(0, 1, 3, 2).reshape(Bsz, H, T)
    return o, lse


def dilated_mixer(h, w_qkv, w_out):
    Bsz, T, _ = h.shape
    n_groups = len(DA_PATTERNS)
    proj = (h @ w_qkv).astype(jnp.float32).reshape(Bsz, T, n_groups, 3, DA_HEADS, DA_HEAD_DIM)
    proj = proj.transpose(2, 3, 0, 4, 1, 5)
    pos = jnp.arange(T)
    outs, lses = [], []
    for gi, (window, dil) in enumerate(DA_PATTERNS):
        o, lse = dilated_window_attention(rope(proj[gi, 0], pos), rope(proj[gi, 1], pos),
                                          proj[gi, 2], dil, window // (2 * dil))
        outs.append(o)
        lses.append(lse)
    alpha = jax.nn.softmax(jnp.stack(lses, axis=0), axis=0)
    o = jnp.sum(alpha[..., None] * jnp.stack(outs, axis=0), axis=0)
    o = o.transpose(0, 2, 1, 3).reshape(Bsz, T, D_MODEL).astype(h.dtype)
    return o @ w_out


def _linear_recurrence_combine(e1, e2):
    a1, b1 = e1
    a2, b2 = e2
    return a1 * a2, a2 * b1 + b2


def s5_direction(ug, a_re, a_im, log_dt, b_mat, reverse):
    T = ug.shape[1]
    a = lax.complex(a_re.astype(jnp.float32), a_im.astype(jnp.float32))
    dt = jnp.exp(log_dt.astype(jnp.float32))[:, None]
    a_bar = jnp.exp(a * dt)
    b_bar = ((a_bar - 1.0) / a)[..., None] * b_mat
    bu = jnp.einsum('btgi,gpi->tbgp', ug.astype(jnp.complex64), b_bar)
    a_seq = jnp.broadcast_to(a_bar, (T, 1) + a_bar.shape)
    _, xs = lax.associative_scan(_linear_recurrence_combine, (a_seq, bu), reverse=reverse, axis=0)
    return xs


def s5_mixer(h, a_re, a_im, log_dt, b_re, b_im, c_re, c_im, d_skip, w_glu):
    Bsz, T, _ = h.shape
    u = h.astype(jnp.float32)
    ug = u.reshape(Bsz, T, S5_GROUPS, S5_GROUP)
    b_mat = lax.complex(b_re.astype(jnp.float32), b_im.astype(jnp.float32))
    c_mat = lax.complex(c_re.astype(jnp.float32), c_im.astype(jnp.float32))
    x_fwd = s5_direction(ug, a_re[0], a_im[0], log_dt[0], b_mat, False)
    x_bwd = s5_direction(ug, a_re[1], a_im[1], log_dt[1], b_mat, True)
    y = jnp.real(jnp.einsum('tbgp,gip->btgi', x_fwd + x_bwd, c_mat)).reshape(Bsz, T, D_MODEL)
    y = y + d_skip.astype(jnp.float32) * u
    z = jax.nn.gelu(y).astype(h.dtype)
    val, gate = jnp.split(z @ w_glu, 2, axis=-1)
    return val * jax.nn.sigmoid(gate)


def expert_choice_ffn(h, w_router, w_gate, w_up, w_down):
    Bsz, T, D = h.shape
    tokens = h.reshape(Bsz * T, D)
    capacity = CAPACITY_FACTOR * (Bsz * T) // N_EXPERTS
    affinity = jax.nn.softmax((tokens @ w_router).astype(jnp.float32), axis=-1)
    gate, idx = lax.top_k(affinity.T, capacity)
    xe = tokens[idx]
    hid = jax.nn.silu(jnp.einsum('ecd,edf->ecf', xe, w_gate)) * jnp.einsum('ecd,edf->ecf', xe, w_up)
    ye = jnp.einsum('ecf,efd->ecd', hid, w_down) * gate[..., None].astype(h.dtype)
    out = jnp.zeros_like(tokens).at[idx.reshape(-1)].add(ye.reshape(-1, D))
    return out.reshape(Bsz, T, D)


def encoder_trunk(x, c, norm_mix_g, norm_ffn_g, ada_w, ada_b,
                  hg_w_in, hg_w_out, hg_norm_g, hg_lb_logits,
                  da_w_qkv, da_w_out,
                  s5_a_re, s5_a_im, s5_log_dt, s5_b_re, s5_b_im, s5_c_re, s5_c_im, s5_d, s5_w_glu,
                  moe_w_router, moe_w_gate, moe_w_up, moe_w_down, final_g):
    lb_table = jnp.cumsum(jax.nn.softmax(hg_lb_logits.astype(jnp.float32), axis=0), axis=0)
    lb_table = lb_table - lb_table[0:1]
    cond = jax.nn.silu(c)
    for layer in range(DEPTH):
        kind, slot = layer % N_MIXERS, layer // N_MIXERS
        mod = (cond @ ada_w[layer] + ada_b[layer])[:, None, :]
        sh1, sc1, g1, sh2, sc2, g2 = jnp.split(mod, 6, axis=-1)
        h = rms_norm(x, norm_mix_g[layer]) * (1 + sc1) + sh1
        if kind == 0:
            m = hgrn2_mixer(h, hg_w_in[slot], hg_w_out[slot], hg_norm_g[slot], lb_table[layer])
        elif kind == 1:
            m = dilated_mixer(h, da_w_qkv[slot], da_w_out[slot])
        else:
            m = s5_mixer(h, s5_a_re[slot], s5_a_im[slot], s5_log_dt[slot], s5_b_re[slot], s5_b_im[slot],
                         s5_c_re[slot], s5_c_im[slot], s5_d[slot], s5_w_glu[slot])
        x = x + g1 * m
        h = rms_norm(x, norm_ffn_g[layer]) * (1 + sc2) + sh2
        x = x + g2 * expert_choice_ffn(h, moe_w_router[layer], moe_w_gate[layer], moe_w_up[layer], moe_w_down[layer])
    return rms_norm(x, final_g)


def setup_inputs(seed: int = 0) -> dict:
    key = jax.random.key(seed)
    ks = iter(jax.random.split(key, 40))
    D = D_MODEL
    nrm = lambda shape, scale: scale * jax.random.normal(next(ks), shape, jnp.float32)
    n_idx = jnp.arange(S5_STATE, dtype=jnp.float32)
    da_cols = len(DA_PATTERNS) * 3 * DA_HEADS * DA_HEAD_DIM
    return {
        'x_prompt': nrm((BATCH, SEQ, D), 1.0),
        'x_sample': nrm((DEC_BATCH, DEC_SEQ, D), 1.0),
        'c_prompt': nrm((BATCH, D), 1.0),
        'c_sample': nrm((DEC_BATCH, D), 1.0),
        'norm_mix_g': 1.0 + nrm((DEPTH, D), 0.02),
        'norm_ffn_g': 1.0 + nrm((DEPTH, D), 0.02),
        'ada_w': nrm((DEPTH, D, 6 * D), 0.02),
        'ada_b': nrm((DEPTH, 6 * D), 0.02),
        'hg_w_in': nrm((N_LAYERS_A, D, 5 * D), D ** -0.5),
        'hg_w_out': nrm((N_LAYERS_A, D, D), D ** -0.5),
        'hg_norm_g': 1.0 + nrm((N_LAYERS_A, HG_DV), 0.02),
        'hg_lb_logits': nrm((DEPTH, HG_HEADS * HG_DK), 0.1),
        'da_w_qkv': nrm((N_LAYERS_B, D, da_cols), D ** -0.5),
        'da_w_out': nrm((N_LAYERS_B, D, D), D ** -0.5),
        's5_a_re': -0.5 + nrm((N_LAYERS_C, 2, S5_GROUPS, S5_STATE), 0.01),
        's5_a_im': np.pi * n_idx + nrm((N_LAYERS_C, 2, S5_GROUPS, S5_STATE), 0.01),
        's5_log_dt': jax.random.uniform(next(ks), (N_LAYERS_C, 2, S5_GROUPS), jnp.float32,
                                        minval=math.log(1e-3), maxval=math.log(1e-1)),
        's5_b_re': nrm((N_LAYERS_C, S5_GROUPS, S5_STATE, S5_GROUP), (2 * S5_GROUP) ** -0.5),
        's5_b_im': nrm((N_LAYERS_C, S5_GROUPS, S5_STATE, S5_GROUP), (2 * S5_GROUP) ** -0.5),
        's5_c_re': nrm((N_LAYERS_C, S5_GROUPS, S5_GROUP, S5_STATE), S5_STATE ** -0.5),
        's5_c_im': nrm((N_LAYERS_C, S5_GROUPS, S5_GROUP, S5_STATE), S5_STATE ** -0.5),
        's5_d': nrm((N_LAYERS_C, D), 1.0),
        's5_w_glu': nrm((N_LAYERS_C, D, 2 * D), D ** -0.5),
        'moe_w_router': nrm((DEPTH, D, N_EXPERTS), D ** -0.5),
        'moe_w_gate': nrm((DEPTH, N_EXPERTS, D, D_EXPERT), D ** -0.5),
        'moe_w_up': nrm((DEPTH, N_EXPERTS, D, D_EXPERT), D ** -0.5),
        'moe_w_down': nrm((DEPTH, N_EXPERTS, D_EXPERT, D), D_EXPERT ** -0.5),
        'final_g': 1.0 + nrm((D,), 0.02),
    }


def reference(x_prompt, x_sample, c_prompt, c_sample, norm_mix_g, norm_ffn_g, ada_w, ada_b,
              hg_w_in, hg_w_out, hg_norm_g, hg_lb_logits, da_w_qkv, da_w_out,
              s5_a_re, s5_a_im, s5_log_dt, s5_b_re, s5_b_im, s5_c_re, s5_c_im, s5_d, s5_w_glu,
              moe_w_router, moe_w_gate, moe_w_up, moe_w_down, final_g):
    weights = (norm_mix_g, norm_ffn_g, ada_w, ada_b, hg_w_in, hg_w_out, hg_norm_g, hg_lb_logits,
               da_w_qkv, da_w_out, s5_a_re, s5_a_im, s5_log_dt, s5_b_re, s5_b_im, s5_c_re, s5_c_im,
               s5_d, s5_w_glu, moe_w_router, moe_w_gate, moe_w_up, moe_w_down, final_g)
    y_prompt = encoder_trunk(x_prompt, c_prompt, *weights)
    y_sample = encoder_trunk(x_sample, c_sample, *weights)
    return (y_prompt, y_sample)
```

```python
import functools
import math

import jax
import jax.numpy as jnp
from jax import lax
from jax.experimental import pallas as pl
from jax.experimental.pallas import tpu as pltpu

F32 = jnp.float32
BF16 = jnp.bfloat16
HIGHEST = lax.Precision.HIGHEST

EPS = 1e-6
N_MIXERS = 3
HG_HEADS = 8
HG_CHUNK = 64
DA_PATTERNS = ((128, 1), (512, 4), (2048, 16))
DA_HEADS = 16
DA_QBLOCK = 128
ROPE_THETA = 10000.0
S5_GROUP = 16
S5_STATE = 64
S5_CHUNK = 16
N_EXPERTS = 16
CAPACITY_FACTOR = 2

LANES = 128
VMEM_LIMIT = 56 * 1024 * 1024
ROW_TILE = 1024


def _cparams(*sem):
    return pltpu.CompilerParams(dimension_semantics=sem, vmem_limit_bytes=VMEM_LIMIT)


def _row_tile(t, cap=ROW_TILE):
    return math.gcd(t, cap)


def _norm_mod(x, g, sc, sh):
    ms = jnp.mean(x * x, axis=-1, keepdims=True)
    return (x * lax.rsqrt(ms + EPS) * g) * (1.0 + sc) + sh


def _nm_matmul_body(x_ref, g_ref, sc_ref, sh_ref, w_ref, o_ref, h_scr):
    @pl.when(pl.program_id(1) == 0)
    def _():
        h_scr[...] = _norm_mod(x_ref[...], g_ref[...], sc_ref[0], sh_ref[0]).astype(BF16)

    o_ref[...] = jnp.dot(h_scr[...], w_ref[...], preferred_element_type=F32).astype(o_ref.dtype)


def nm_matmul(x2, seq, g, sc, sh, w, tn=512):
    n, d = x2.shape
    f = w.shape[1]
    tm = _row_tile(seq)
    tn = math.gcd(f, tn)
    return pl.pallas_call(
        _nm_matmul_body,
        grid=(n // tm, f // tn),
        in_specs=[
            pl.BlockSpec((tm, d), lambda i, j: (i, 0)),
            pl.BlockSpec((1, d), lambda i, j: (0, 0)),
            pl.BlockSpec((1, 1, d), lambda i, j: ((i * tm) // seq, 0, 0)),
            pl.BlockSpec((1, 1, d), lambda i, j: ((i * tm) // seq, 0, 0)),
            pl.BlockSpec((d, tn), lambda i, j: (0, j)),
        ],
        out_specs=pl.BlockSpec((tm, tn), lambda i, j: (i, j)),
        out_shape=jax.ShapeDtypeStruct((n, f), F32),
        scratch_shapes=[pltpu.VMEM((tm, d), BF16)],
        compiler_params=_cparams("parallel", "arbitrary"),
        name="nm_matmul",
    )(x2, g, sc, sh, w)


def _norm_mod_body(x_ref, g_ref, sc_ref, sh_ref, o_ref):
    o_ref[...] = _norm_mod(x_ref[...], g_ref[...], sc_ref[0], sh_ref[0])


def norm_mod(x2, seq, g, sc, sh):
    n, d = x2.shape
    tm = _row_tile(seq)
    return pl.pallas_call(
        _norm_mod_body,
        grid=(n // tm,),
        in_specs=[
            pl.BlockSpec((tm, d), lambda i: (i, 0)),
            pl.BlockSpec((1, d), lambda i: (0, 0)),
            pl.BlockSpec((1, 1, d), lambda i: ((i * tm) // seq, 0, 0)),
            pl.BlockSpec((1, 1, d), lambda i: ((i * tm) // seq, 0, 0)),
        ],
        out_specs=pl.BlockSpec((tm, d), lambda i: (i, 0)),
        out_shape=jax.ShapeDtypeStruct((n, d), F32),
        compiler_params=_cparams("parallel"),
        name="norm_mod",
    )(x2, g, sc, sh)


def _norm_router_body(x_ref, g_ref, sc_ref, sh_ref, wr_ref, h_ref, lg_ref):
    h = _norm_mod(x_ref[...], g_ref[...], sc_ref[0], sh_ref[0])
    h_ref[...] = h.astype(BF16)
    lg_ref[...] = jnp.dot(h, wr_ref[...], precision=HIGHEST, preferred_element_type=F32)


def norm_router(x2, seq, g, sc, sh, w_router_padded):
    n, d = x2.shape
    ep = w_router_padded.shape[1]
    tm = _row_tile(seq, 512)
    return pl.pallas_call(
        _norm_router_body,
        grid=(n // tm,),
        in_specs=[
            pl.BlockSpec((tm, d), lambda i: (i, 0)),
            pl.BlockSpec((1, d), lambda i: (0, 0)),
            pl.BlockSpec((1, 1, d), lambda i: ((i * tm) // seq, 0, 0)),
            pl.BlockSpec((1, 1, d), lambda i: ((i * tm) // seq, 0, 0)),
            pl.BlockSpec((d, ep), lambda i: (0, 0)),
        ],
        out_specs=[pl.BlockSpec((tm, d), lambda i: (i, 0)),
                   pl.BlockSpec((tm, ep), lambda i: (i, 0))],
        out_shape=[jax.ShapeDtypeStruct((n, d), BF16), jax.ShapeDtypeStruct((n, ep), F32)],
        compiler_params=_cparams("parallel"),
        name="norm_router",
    )(x2, g, sc, sh, w_router_padded)


def _proj_res_body(m_ref, w_ref, x_ref, gate_ref, o_ref):
    y = jnp.dot(m_ref[...], w_ref[...], preferred_element_type=F32)
    o_ref[...] = x_ref[...] + gate_ref[0] * y


def _glu_res_body(m_ref, w_ref, x_ref, gate_ref, o_ref):
    d = x_ref.shape[-1]
    y = jnp.dot(m_ref[...], w_ref[...], preferred_element_type=F32)
    o_ref[...] = x_ref[...] + gate_ref[0] * (y[:, :d] * jax.nn.sigmoid(y[:, d:]))


def proj_residual(m2, w, x2, gate, seq, glu=False):
    n, d = x2.shape
    f = w.shape[1]
    tm = _row_tile(seq, 512)
    return pl.pallas_call(
        _glu_res_body if glu else _proj_res_body,
        grid=(n // tm,),
        in_specs=[
            pl.BlockSpec((tm, d), lambda i: (i, 0)),
            pl.BlockSpec((d, f), lambda i: (0, 0)),
            pl.BlockSpec((tm, d), lambda i: (i, 0)),
            pl.BlockSpec((1, 1, d), lambda i: ((i * tm) // seq, 0, 0)),
        ],
        out_specs=pl.BlockSpec((tm, d), lambda i: (i, 0)),
        out_shape=jax.ShapeDtypeStruct((n, d), F32),
        compiler_params=_cparams("parallel"),
        name="glu_residual" if glu else "proj_residual",
    )(m2, w, x2, gate)


def _res_norm_body(x_ref, y_ref, gate_ref, g_ref, o_ref):
    x = x_ref[...] + gate_ref[0] * y_ref[...]
    ms = jnp.mean(x * x, axis=-1, keepdims=True)
    o_ref[...] = x * lax.rsqrt(ms + EPS) * g_ref[...]


def residual_final_norm(x2, y2, gate, g, seq):
    n, d = x2.shape
    tm = _row_tile(seq)
    return pl.pallas_call(
        _res_norm_body,
        grid=(n // tm,),
        in_specs=[
            pl.BlockSpec((tm, d), lambda i: (i, 0)),
            pl.BlockSpec((tm, d), lambda i: (i, 0)),
            pl.BlockSpec((1, 1, d), lambda i: ((i * tm) // seq, 0, 0)),
            pl.BlockSpec((1, d), lambda i: (0, 0)),
        ],
        out_specs=pl.BlockSpec((tm, d), lambda i: (i, 0)),
        out_shape=jax.ShapeDtypeStruct((n, d), F32),
        compiler_params=_cparams("parallel"),
        name="residual_final_norm",
    )(x2, y2, gate, g)


def _dot_nt(a, b):
    return lax.dot_general(a, b, (((1,), (1,)), ((), ())), preferred_element_type=F32)


def _hgrn_gates(z, la, l1, om):
    e = jnp.exp(-jnp.abs(z))
    log_sig = jnp.minimum(z, 0.0) - jnp.log1p(e)
    k = om * (jnp.where(z >= 0.0, e, 1.0) / (1.0 + e))
    c = l1 + log_sig
    logf = jnp.maximum(la, c) + jnp.log1p(jnp.exp(-jnp.abs(la - c)))
    return logf, k


def _hgrn_chunk(q, v, z, la, l1, om, st, cum_mat, causal, ref_row, last_row):
    logf, k = _hgrn_gates(z, la, l1, om)
    b = jnp.dot(cum_mat, logf, precision=HIGHEST, preferred_element_type=F32)
    b_ref = b[ref_row:ref_row + 1, :]
    b_last = b[last_row:last_row + 1, :]
    qd = (q * jnp.exp(b - b_ref)).astype(BF16)
    kd = (k * jnp.exp(b_ref - b)).astype(BF16)
    s = jnp.where(causal, _dot_nt(qd, kd), 0.0)
    vb = v.astype(BF16)
    o = jnp.dot(s.astype(BF16), vb, preferred_element_type=F32)
    o = o + _dot_nt((q * jnp.exp(b)).astype(BF16), st.astype(BF16))
    kl = (k * jnp.exp(b_last - b)).astype(BF16)
    st_new = st * jnp.exp(b_last) + jnp.dot(v.T.astype(BF16), kl, preferred_element_type=F32)
    return o, st_new


def _hgrn_body(q_ref, v_ref, zf_ref, zb_ref, g_ref, la_ref, l1_ref, om_ref, ng_ref, o_ref,
               of_scr, ob_scr, st_scr):
    seq = q_ref.shape[1]
    c = min(HG_CHUNK, seq)
    n_chunks = seq // c
    la, l1, om = la_ref[0], l1_ref[0], om_ref[0]
    row = lax.broadcasted_iota(jnp.int32, (c, c), 0)
    col = lax.broadcasted_iota(jnp.int32, (c, c), 1)
    lower, upper = row >= col, row <= col
    tril, triu = lower.astype(F32), upper.astype(F32)
    mid = c // 2
    st_scr[...] = jnp.zeros_like(st_scr)

    def step(n, carry):
        rf = pl.ds(pl.multiple_of(n * c, c), c)
        of, sf = _hgrn_chunk(q_ref[0, rf, :], v_ref[0, rf, :], zf_ref[0, rf, :], la, l1, om,
                             st_scr[0], tril, lower, mid - 1, c - 1)
        of_scr[rf, :] = of
        st_scr[0] = sf
        rb = pl.ds(pl.multiple_of((n_chunks - 1 - n) * c, c), c)
        ob, sb = _hgrn_chunk(q_ref[0, rb, :], v_ref[0, rb, :], zb_ref[0, rb, :], la, l1, om,
                             st_scr[1], triu, upper, c - mid, 0)
        ob_scr[rb, :] = ob
        st_scr[1] = sb
        return carry

    lax.fori_loop(0, n_chunks, step, 0)

    tile = math.gcd(seq, 256)

    def finish(i, carry):
        r = pl.ds(pl.multiple_of(i * tile, tile), tile)
        o = of_scr[r, :] + ob_scr[r, :]
        o = o * lax.rsqrt(jnp.mean(o * o, axis=-1, keepdims=True) + EPS) * ng_ref[...]
        o_ref[0, r, :] = (o * jax.nn.sigmoid(g_ref[0, r, :])).astype(o_ref.dtype)
        return carry

    lax.fori_loop(0, seq // tile, finish, 0)


def hgrn_recurrence(proj, la, l1, om, norm_g):
    bsz, seq, d5 = proj.shape
    d = d5 // 5
    h = HG_HEADS
    dk = d // h
    assert dk == LANES

    def col(section):
        return pl.BlockSpec((1, seq, dk), lambda b, hh: (b, 0, section * h + hh))

    par = pl.BlockSpec((1, 1, dk), lambda b, hh: (hh, 0, 0))
    return pl.pallas_call(
        _hgrn_body,
        grid=(bsz, h),
        in_specs=[col(0), col(1), col(2), col(3), col(4), par, par, par,
                  pl.BlockSpec((1, dk), lambda b, hh: (0, 0))],
        out_specs=pl.BlockSpec((1, seq, dk), lambda b, hh: (b, 0, hh)),
        out_shape=jax.ShapeDtypeStruct((bsz, seq, d), BF16),
        scratch_shapes=[pltpu.VMEM((seq, dk), F32), pltpu.VMEM((seq, dk), F32),
                        pltpu.VMEM((2, dk, dk), F32)],
        compiler_params=_cparams("parallel", "parallel"),
        name="hgrn_recurrence",
    )(proj, proj, proj, proj, proj, la, l1, om, norm_g)


def _rope_pair(x, cos, sin_signed, first_half):
    partner = jnp.where(first_half, pltpu.roll(x, LANES - 32, 1), pltpu.roll(x, 32, 1))
    return x * cos + partner * sin_signed


def _attn_group(q_ref, k_ref, v_ref, cos_ref, sin_ref, qs, ks, vs, ot, mt, lt, dil, half):
    seq = q_ref.shape[1]
    length = seq // dil
    dh = LANES // 2
    lane = lax.broadcasted_iota(jnp.int32, (1, LANES), 1)
    first_half = (lane % dh) < (dh // 2)
    head0 = lane < dh
    tile = math.gcd(length, 256)
    for r in range(dil):
        for t0 in range(0, length, tile):
            if dil == 1:
                src = pl.ds(t0, tile)
            else:
                src = pl.ds(r + t0 * dil, tile, stride=dil)
            dst = pl.ds(r * length + t0, tile)
            cos, sin = cos_ref[src, :], sin_ref[src, :]
            qs[dst, :] = (_rope_pair(q_ref[0, src, :], cos, sin, first_half) * (dh ** -0.5)).astype(BF16)
            ks[dst, :] = _rope_pair(k_ref[0, src, :], cos, sin, first_half).astype(BF16)
            vs[dst, :] = v_ref[0, src, :].astype(BF16)

    qb = min(DA_QBLOCK, length)
    span = min(qb + 2 * half, length)
    ones = jnp.ones((span, LANES), BF16)

    def block(bi, carry):
        row0 = pl.multiple_of(bi * qb, qb)
        r = row0 // length
        m0 = row0 - r * length
        k0 = jnp.clip(m0 - half, 0, length - span)
        krows = pl.ds(pl.multiple_of(r * length + k0, 16), span)
        q = qs[pl.ds(row0, qb), :]
        kk = ks[krows, :]
        vv = vs[krows, :]
        qi = m0 + lax.broadcasted_iota(jnp.int32, (qb, span), 0)
        kj = k0 + lax.broadcasted_iota(jnp.int32, (qb, span), 1)
        valid = jnp.abs(kj - qi) <= half
        zero = jnp.zeros_like(q)
        outs = []
        for qh in (jnp.where(head0, q, zero), jnp.where(head0, zero, q)):
            s = jnp.where(valid, _dot_nt(qh, kk), -jnp.inf)
            m = jnp.max(s, axis=-1, keepdims=True)
            p = jnp.exp(s - m).astype(BF16)
            outs.append((jnp.dot(p, vv, preferred_element_type=F32),
                         jnp.dot(p, ones, preferred_element_type=F32),
                         jnp.broadcast_to(m, (qb, LANES))))
        rows = pl.ds(row0, qb)
        ot[rows, :] = jnp.where(head0, outs[0][0], outs[1][0])
        lt[rows, :] = jnp.where(head0, outs[0][1], outs[1][1])
        mt[rows, :] = jnp.where(head0, outs[0][2], outs[1][2])
        return carry

    lax.fori_loop(0, seq // qb, block, 0)


def _attn_body(q_ref, k_ref, v_ref, cos_ref, sin_ref, o_ref, qs, ks, vs, ot, mt, lt, acc, mrun, lrun):
    g = pl.program_id(2)
    seq = q_ref.shape[1]
    for gi, (window, dil) in enumerate(DA_PATTERNS):
        half = window // (2 * dil)
        assert half % 16 == 0
        length = seq // dil
        tile = math.gcd(length, 256)

        @pl.when(g == gi)
        def _(dil=dil, half=half, gi=gi, length=length, tile=tile):
            _attn_group(q_ref, k_ref, v_ref, cos_ref, sin_ref, qs, ks, vs, ot, mt, lt, dil, half)
            for r in range(dil):
                for t0 in range(0, length, tile):
                    src = pl.ds(r * length + t0, tile)
                    dst = pl.ds(t0, tile) if dil == 1 else pl.ds(r + t0 * dil, tile, stride=dil)
                    o_new, m_new, l_new = ot[src, :], mt[src, :], lt[src, :]
                    if gi == 0:
                        acc[dst, :], mrun[dst, :], lrun[dst, :] = o_new, m_new, l_new
                    else:
                        m_old = mrun[dst, :]
                        m_all = jnp.maximum(m_old, m_new)
                        w_old, w_new = jnp.exp(m_old - m_all), jnp.exp(m_new - m_all)
                        acc[dst, :] = acc[dst, :] * w_old + o_new * w_new
                        lrun[dst, :] = lrun[dst, :] * w_old + l_new * w_new
                        mrun[dst, :] = m_all

    @pl.when(g == len(DA_PATTERNS) - 1)
    def _():
        tile = math.gcd(seq, 256)

        def finish(i, carry):
            r = pl.ds(pl.multiple_of(i * tile, tile), tile)
            o_ref[0, r, :] = (acc[r, :] / lrun[r, :]).astype(o_ref.dtype)
            return carry

        lax.fori_loop(0, seq // tile, finish, 0)


def _rope_tables(seq):
    dh = LANES // 2
    halfd = dh // 2
    inv = ROPE_THETA ** (-jnp.arange(halfd, dtype=F32) / halfd)
    ang = jnp.arange(seq, dtype=F32)[:, None] * inv[None, :]
    cos, sin = jnp.cos(ang), jnp.sin(ang)
    return jnp.tile(cos, (1, 4)), jnp.concatenate([-sin, sin, -sin, sin], axis=1)


def dilated_attention(proj):
    bsz, seq, d9 = proj.shape
    n_groups = len(DA_PATTERNS)
    d = d9 // (3 * n_groups)
    assert d // DA_HEADS == LANES // 2
    pairs = d // LANES
    cos, sin = _rope_tables(seq)

    def col(part):
        return pl.BlockSpec((1, seq, LANES), lambda b, hp, g: (b, 0, (g * 3 + part) * pairs + hp))

    table = pl.BlockSpec((seq, LANES), lambda b, hp, g: (0, 0))
    return pl.pallas_call(
        _attn_body,
        grid=(bsz, pairs, n_groups),
        in_specs=[col(0), col(1), col(2), table, table],
        out_specs=pl.BlockSpec((1, seq, LANES), lambda b, hp, g: (b, 0, hp)),
        out_shape=jax.ShapeDtypeStruct((bsz, seq, d), BF16),
        scratch_shapes=[pltpu.VMEM((seq, LANES), BF16)] * 3 + [pltpu.VMEM((seq, LANES), F32)] * 6,
        compiler_params=_cparams("parallel", "parallel", "arbitrary"),
        name="dilated_attention",
    )(proj, proj, proj, cos, sin)


def _s5_operators(a_re, a_im, log_dt, b_re, b_im, c_re, c_im, d_skip):
    lc, i_dim, p_dim = S5_CHUNK, S5_GROUP, S5_STATE
    n_groups = a_re.shape[1]
    hp = dict(precision=HIGHEST)
    a = lax.complex(a_re.astype(F32), a_im.astype(F32))
    lam = a * jnp.exp(log_dt.astype(F32))[..., None]
    a_bar = jnp.exp(lam)
    bmat = lax.complex(b_re.astype(F32), b_im.astype(F32))
    cmat = lax.complex(c_re.astype(F32), c_im.astype(F32))
    b_bar = ((a_bar - 1.0) / a)[..., None] * bmat[None]
    tau = jnp.arange(lc + 1, dtype=F32)
    apow = jnp.exp(lam[:, :, None, :] * tau[None, None, :, None])
    kern = jnp.real(jnp.einsum('gip,dgtp,dgpj->dgtij', cmat, apow[:, :, :lc], b_bar, **hp))
    s_idx = jnp.arange(lc)[:, None]
    t_idx = jnp.arange(lc)[None, :]
    lag = t_idx - s_idx
    m_f = jnp.where((lag >= 0)[None, :, :, None, None], kern[0][:, jnp.clip(lag, 0)], 0.0)
    m_b = jnp.where((lag <= 0)[None, :, :, None, None], kern[1][:, jnp.clip(-lag, 0)], 0.0)
    m_op = (m_f + m_b).transpose(0, 1, 4, 2, 3).reshape(n_groups, lc * i_dim, lc * i_dim)
    rev = jnp.arange(lc - 1, -1, -1)
    p_f = apow[0][:, rev][:, :, :, None] * b_bar[0][:, None]
    p_b = apow[1][:, :lc][:, :, :, None] * b_bar[1][:, None]
    p_op = jnp.stack([jnp.real(p_f), jnp.imag(p_f), jnp.real(p_b), jnp.imag(p_b)], axis=0)
    p_op = p_op.transpose(1, 2, 4, 0, 3).reshape(n_groups, lc * i_dim, 4, p_dim)
    q_f = cmat[:, None] * apow[0][:, 1:lc + 1][:, :, None, :]
    q_b = cmat[:, None] * apow[1][:, lc - jnp.arange(lc)][:, :, None, :]
    q_op = jnp.stack([jnp.real(q_f), -jnp.imag(q_f), jnp.real(q_b), -jnp.imag(q_b)], axis=0)
    q_op = q_op.transpose(1, 0, 4, 2, 3).reshape(n_groups, 4, p_dim, lc * i_dim)
    a_chunk = jnp.stack([jnp.real(apow[0][:, lc]), jnp.imag(apow[0][:, lc]),
                         jnp.real(apow[1][:, lc]), jnp.imag(apow[1][:, lc])], axis=1)
    n_pairs = n_groups // 2
    eye = jnp.eye(2, dtype=F32)
    w = lc * i_dim
    m_pair = m_op.reshape(n_pairs, 2, w, w)
    p_pair = jnp.einsum('narqp,ab->narqbp', p_op.reshape(n_pairs, 2, w, 4, p_dim), eye)
    p_pair = p_pair.reshape(n_pairs, 2 * w, 8 * p_dim)
    q_pair = jnp.einsum('nbqpc,ab->nqbpac', q_op.reshape(n_pairs, 2, 4, p_dim, w), eye)
    q_pair = q_pair.reshape(n_pairs, 8 * p_dim, 2 * w)
    a_pair = a_chunk.reshape(n_pairs, 2, 4, p_dim).transpose(0, 2, 1, 3).reshape(n_pairs, 4, 2 * p_dim)
    d_pair = jnp.broadcast_to(d_skip.astype(F32).reshape(n_pairs, 2, 1, i_dim), (n_pairs, 2, lc, i_dim))
    d_pair = d_pair.reshape(n_pairs, 1, 2 * w)
    return m_pair, p_pair, q_pair, a_pair, d_pair


def _s5_body(u_ref, m_ref, p_ref, q_ref, a_ref, d_ref, o_ref, v_scr, s_scr, *, bsz):
    rows = u_ref.shape[1]
    width = u_ref.shape[2]
    half_w = width // 2
    sw = a_ref.shape[2]
    n_chunks = rows // bsz
    tile = math.gcd(rows, 512)
    n_tiles = rows // tile

    def project(i, carry):
        r = pl.ds(pl.multiple_of(i * tile, tile), tile)
        v_scr[r, :] = jnp.dot(u_ref[0, r, :], p_ref[0], precision=HIGHEST, preferred_element_type=F32)
        return carry

    lax.fori_loop(0, n_tiles, project, 0)

    a_f_re, a_f_im = a_ref[0, 0:1, :], a_ref[0, 1:2, :]
    a_b_re, a_b_im = a_ref[0, 2:3, :], a_ref[0, 3:4, :]
    zero = jnp.zeros((bsz, sw), F32)

    def scan(c, carry):
        f_re, f_im, b_re, b_im = carry
        rf = pl.ds(pl.multiple_of(c * bsz, bsz), bsz)
        s_scr[rf, 0:sw] = f_re
        s_scr[rf, sw:2 * sw] = f_im
        n_re = a_f_re * f_re - a_f_im * f_im + v_scr[rf, 0:sw]
        n_im = a_f_re * f_im + a_f_im * f_re + v_scr[rf, sw:2 * sw]
        rb = pl.ds(pl.multiple_of((n_chunks - 1 - c) * bsz, bsz), bsz)
        s_scr[rb, 2 * sw:3 * sw] = b_re
        s_scr[rb, 3 * sw:4 * sw] = b_im
        m_re = a_b_re * b_re - a_b_im * b_im + v_scr[rb, 2 * sw:3 * sw]
        m_im = a_b_re * b_im + a_b_im * b_re + v_scr[rb, 3 * sw:4 * sw]
        return n_re, n_im, m_re, m_im

    lax.fori_loop(0, n_chunks, scan, (zero, zero, zero, zero))

    def emit(i, carry):
        r = pl.ds(pl.multiple_of(i * tile, tile), tile)
        u = u_ref[0, r, :]
        y = jnp.dot(s_scr[r, :], q_ref[0], precision=HIGHEST, preferred_element_type=F32)
        y_lo = jnp.dot(u[:, :half_w], m_ref[0, 0], precision=HIGHEST, preferred_element_type=F32)
        y_hi = jnp.dot(u[:, half_w:], m_ref[0, 1], precision=HIGHEST, preferred_element_type=F32)
        y = y + jnp.concatenate([y_lo, y_hi], axis=1) + d_ref[0] * u
        o_ref[0, r, :] = jax.nn.gelu(y).astype(o_ref.dtype)
        return carry

    lax.fori_loop(0, n_tiles, emit, 0)


def s5_mix(h3, ops):
    m_pair, p_pair, q_pair, a_pair, d_pair = ops
    bsz, seq, d = h3.shape
    lc, i_dim = S5_CHUNK, S5_GROUP
    n_pairs = d // (2 * i_dim)
    n_chunks = seq // lc
    width = 2 * lc * i_dim
    rows = n_chunks * bsz
    assert bsz % 8 == 0 and seq % lc == 0
    u = h3.reshape(bsz, n_chunks, lc, n_pairs, 2, i_dim).transpose(3, 1, 0, 4, 2, 5).reshape(n_pairs, rows, width)
    sw = a_pair.shape[2]
    z = pl.pallas_call(
        functools.partial(_s5_body, bsz=bsz),
        grid=(n_pairs,),
        in_specs=[
            pl.BlockSpec((1, rows, width), lambda g: (g, 0, 0)),
            pl.BlockSpec((1, 2, width // 2, width // 2), lambda g: (g, 0, 0, 0)),
            pl.BlockSpec((1, width, 4 * sw), lambda g: (g, 0, 0)),
            pl.BlockSpec((1, 4 * sw, width), lambda g: (g, 0, 0)),
            pl.BlockSpec((1, 4, sw), lambda g: (g, 0, 0)),
            pl.BlockSpec((1, 1, width), lambda g: (g, 0, 0)),
        ],
        out_specs=pl.BlockSpec((1, rows, width), lambda g: (g, 0, 0)),
        out_shape=jax.ShapeDtypeStruct((n_pairs, rows, width), BF16),
        scratch_shapes=[pltpu.VMEM((rows, 4 * sw), F32), pltpu.VMEM((rows, 4 * sw), F32)],
        compiler_params=_cparams("parallel"),
        name="s5_mix",
    )(u, m_pair, p_pair, q_pair, a_pair, d_pair)
    z = z.reshape(n_pairs, n_chunks, bsz, 2, lc, i_dim).transpose(2, 1, 4, 0, 3, 5)
    return z.reshape(bsz, seq, d)


def _expert_body(x_ref, wg_ref, wu_ref, wd_ref, gate_ref, o_ref):
    f = pl.program_id(2)
    x = x_ref[0]
    a = jnp.dot(x, wg_ref[0], preferred_element_type=F32)
    u = jnp.dot(x, wu_ref[0], preferred_element_type=F32)
    hid = (a * jax.nn.sigmoid(a) * u).astype(BF16)
    part = jnp.dot(hid, wd_ref[0], preferred_element_type=F32)

    @pl.when(f == 0)
    def _():
        o_ref[0] = part

    @pl.when(f > 0)
    def _():
        o_ref[0] += part

    @pl.when(f == pl.num_programs(2) - 1)
    def _():
        o_ref[0] = o_ref[0] * gate_ref[0]


def expert_ffn(xe, w_gate, w_up, w_down, gate, tf=512):
    e, cap, d = xe.shape
    f = w_gate.shape[2]
    tm = math.gcd(cap, ROW_TILE)
    tf = math.gcd(f, tf)
    return pl.pallas_call(
        _expert_body,
        grid=(e, cap // tm, f // tf),
        in_specs=[
            pl.BlockSpec((1, tm, d), lambda ei, i, j: (ei, i, 0)),
            pl.BlockSpec((1, d, tf), lambda ei, i, j: (ei, 0, j)),
            pl.BlockSpec((1, d, tf), lambda ei, i, j: (ei, 0, j)),
            pl.BlockSpec((1, tf, d), lambda ei, i, j: (ei, j, 0)),
            pl.BlockSpec((1, tm, 1), lambda ei, i, j: (ei, i, 0)),
        ],
        out_specs=pl.BlockSpec((1, tm, d), lambda ei, i, j: (ei, i, 0)),
        out_shape=jax.ShapeDtypeStruct((e, cap, d), F32),
        compiler_params=_cparams("parallel", "parallel", "arbitrary"),
        name="expert_ffn",
    )(xe, w_gate, w_up, w_down, gate)


def expert_choice_moe(x2, seq, norm_g, sc, sh, w_router, w_gate, w_up, w_down):
    n, d = x2.shape
    e = w_router.shape[1]
    capacity = CAPACITY_FACTOR * n // e
    wr = jnp.zeros((d, LANES), F32).at[:, :e].set(w_router.astype(F32))
    h, logits = norm_router(x2, seq, norm_g, sc, sh, wr)
    affinity = jax.nn.softmax(logits[:, :e], axis=-1)
    gate, idx = lax.top_k(affinity.T, capacity)
    xe = h[idx]
    ye = expert_ffn(xe, w_gate, w_up, w_down, gate[..., None])
    return jnp.zeros((n, d), F32).at[idx.reshape(-1)].add(ye.reshape(-1, d))


def _trunk(x, c, params):
    (norm_mix_g, norm_ffn_g, ada_w, ada_b, hg_w_in, hg_w_out, hg_norm_g, lb_table,
     da_w_qkv, da_w_out, s5_ops, s5_w_glu, moe_w_router, moe_w_gate, moe_w_up, moe_w_down, final_g) = params
    bsz, seq, d = x.shape
    depth = norm_mix_g.shape[0]
    x2 = x.reshape(bsz * seq, d)
    cond = jax.nn.silu(c)
    out = None
    for layer in range(depth):
        kind, slot = layer % N_MIXERS, layer // N_MIXERS
        mod = (jnp.dot(cond, ada_w[layer], precision=HIGHEST) + ada_b[layer])[:, None, :]
        sh1, sc1, g1, sh2, sc2, g2 = jnp.split(mod, 6, axis=-1)
        gmix = norm_mix_g[layer][None, :]
        if kind == 0:
            proj = nm_matmul(x2, seq, gmix, sc1, sh1, hg_w_in[slot])
            lb = lb_table[layer].reshape(HG_HEADS, 1, -1)
            m = hgrn_recurrence(proj.reshape(bsz, seq, -1), jnp.log(lb), jnp.log1p(-lb), 1.0 - lb,
                                hg_norm_g[slot][None, :].astype(F32))
            x2 = proj_residual(m.reshape(bsz * seq, d), hg_w_out[slot], x2, g1, seq)
        elif kind == 1:
            proj = nm_matmul(x2, seq, gmix, sc1, sh1, da_w_qkv[slot])
            m = dilated_attention(proj.reshape(bsz, seq, -1))
            x2 = proj_residual(m.reshape(bsz * seq, d), da_w_out[slot], x2, g1, seq)
        else:
            h = norm_mod(x2, seq, gmix, sc1, sh1)
            z = s5_mix(h.reshape(bsz, seq, d), s5_ops[slot])
            x2 = proj_residual(z.reshape(bsz * seq, d), s5_w_glu[slot], x2, g1, seq, glu=True)
        y = expert_choice_moe(x2, seq, norm_ffn_g[layer][None, :], sc2, sh2, moe_w_router[layer],
                              moe_w_gate[layer], moe_w_up[layer], moe_w_down[layer])
        if layer == depth - 1:
            out = residual_final_norm(x2, y, g2, final_g[None, :].astype(F32), seq)
        else:
            x2 = x2 + jnp.broadcast_to(g2, (bsz, seq, d)).reshape(bsz * seq, d) * y
    return out.reshape(bsz, seq, d)


def kernel(x_prompt, x_sample, c_prompt, c_sample, norm_mix_g, norm_ffn_g, ada_w, ada_b, hg_w_in, hg_w_out, hg_norm_g, hg_lb_logits, da_w_qkv, da_w_out, s5_a_re, s5_a_im, s5_log_dt, s5_b_re, s5_b_im, s5_c_re, s5_c_im, s5_d, s5_w_glu, moe_w_router, moe_w_gate, moe_w_up, moe_w_down, final_g):
    lb_table = jnp.cumsum(jax.nn.softmax(hg_lb_logits.astype(F32), axis=0), axis=0)
    lb_table = lb_table - lb_table[0:1]
    s5_ops = [_s5_operators(s5_a_re[s], s5_a_im[s], s5_log_dt[s], s5_b_re[s], s5_b_im[s],
                            s5_c_re[s], s5_c_im[s], s5_d[s]) for s in range(s5_a_re.shape[0])]
    bf = lambda w: w.astype(BF16)
    params = (norm_mix_g.astype(F32), norm_ffn_g.astype(F32), ada_w, ada_b, bf(hg_w_in), bf(hg_w_out),
              hg_norm_g, lb_table, bf(da_w_qkv), bf(da_w_out), s5_ops, bf(s5_w_glu), moe_w_router,
              bf(moe_w_gate), bf(moe_w_up), bf(moe_w_down), final_g)
    return (_trunk(x_prompt, c_prompt, params), _trunk(x_sample, c_sample, params))
```

```python
import functools
import math

import jax
import jax.numpy as jnp
from jax import lax
from jax.experimental import pallas as pl
from jax.experimental.pallas import tpu as pltpu

F32 = jnp.float32
BF16 = jnp.bfloat16
HIGHEST = lax.Precision.HIGHEST

EPS = 1e-6
N_MIXERS = 3
HG_HEADS = 8
HG_CHUNK = 64
HG_SUPER = 4
HG_FINISH = 8
DA_PATTERNS = ((128, 1), (512, 4), (2048, 16))
DA_HEADS = 16
DA_QBLOCK = 128
DA_SEGMENT = 8
DA_UNROLL = 8
ROPE_THETA = 10000.0
S5_GROUP = 16
S5_STATE = 64
S5_CHUNK = 16
S5_SEGMENTS = 8
N_EXPERTS = 16
CAPACITY_FACTOR = 2

LANES = 128
VMEM_LIMIT = 56 * 1024 * 1024
ROW_TILE = 1024


def _cparams(*sem):
    return pltpu.CompilerParams(dimension_semantics=sem, vmem_limit_bytes=VMEM_LIMIT)


def _row_tile(t, cap=ROW_TILE):
    return math.gcd(t, cap)


def _norm_mod(x, g, sc, sh):
    ms = jnp.mean(x * x, axis=-1, keepdims=True)
    return (x * lax.rsqrt(ms + EPS) * g) * (1.0 + sc) + sh


def _nm_matmul_body(x_ref, g_ref, sc_ref, sh_ref, w_ref, o_ref, h_scr):
    @pl.when(pl.program_id(1) == 0)
    def _():
        h_scr[...] = _norm_mod(x_ref[...], g_ref[...], sc_ref[0], sh_ref[0]).astype(BF16)

    o_ref[...] = jnp.dot(h_scr[...], w_ref[...], preferred_element_type=F32).astype(o_ref.dtype)


def nm_matmul(x2, seq, g, sc, sh, w, tn=512):
    n, d = x2.shape
    f = w.shape[1]
    tm = _row_tile(seq)
    tn = math.gcd(f, tn)
    return pl.pallas_call(
        _nm_matmul_body,
        grid=(n // tm, f // tn),
        in_specs=[
            pl.BlockSpec((tm, d), lambda i, j: (i, 0)),
            pl.BlockSpec((1, d), lambda i, j: (0, 0)),
            pl.BlockSpec((1, 1, d), lambda i, j: ((i * tm) // seq, 0, 0)),
            pl.BlockSpec((1, 1, d), lambda i, j: ((i * tm) // seq, 0, 0)),
            pl.BlockSpec((d, tn), lambda i, j: (0, j)),
        ],
        out_specs=pl.BlockSpec((tm, tn), lambda i, j: (i, j)),
        out_shape=jax.ShapeDtypeStruct((n, f), F32),
        scratch_shapes=[pltpu.VMEM((tm, d), BF16)],
        compiler_params=_cparams("parallel", "arbitrary"),
        name="nm_matmul",
    )(x2, g, sc, sh, w)


def _norm_mod_body(x_ref, g_ref, sc_ref, sh_ref, o_ref):
    o_ref[...] = _norm_mod(x_ref[...], g_ref[...], sc_ref[0], sh_ref[0])


def norm_mod(x2, seq, g, sc, sh):
    n, d = x2.shape
    tm = _row_tile(seq)
    return pl.pallas_call(
        _norm_mod_body,
        grid=(n // tm,),
        in_specs=[
            pl.BlockSpec((tm, d), lambda i: (i, 0)),
            pl.BlockSpec((1, d), lambda i: (0, 0)),
            pl.BlockSpec((1, 1, d), lambda i: ((i * tm) // seq, 0, 0)),
            pl.BlockSpec((1, 1, d), lambda i: ((i * tm) // seq, 0, 0)),
        ],
        out_specs=pl.BlockSpec((tm, d), lambda i: (i, 0)),
        out_shape=jax.ShapeDtypeStruct((n, d), F32),
        compiler_params=_cparams("parallel"),
        name="norm_mod",
    )(x2, g, sc, sh)


def _norm_router_body(x_ref, g_ref, sc_ref, sh_ref, wr_ref, h_ref, lg_ref):
    h = _norm_mod(x_ref[...], g_ref[...], sc_ref[0], sh_ref[0])
    h_ref[...] = h.astype(BF16)
    lg_ref[...] = jnp.dot(h, wr_ref[...], precision=HIGHEST, preferred_element_type=F32)


def norm_router(x2, seq, g, sc, sh, w_router_padded):
    n, d = x2.shape
    ep = w_router_padded.shape[1]
    tm = _row_tile(seq, 512)
    return pl.pallas_call(
        _norm_router_body,
        grid=(n // tm,),
        in_specs=[
            pl.BlockSpec((tm, d), lambda i: (i, 0)),
            pl.BlockSpec((1, d), lambda i: (0, 0)),
            pl.BlockSpec((1, 1, d), lambda i: ((i * tm) // seq, 0, 0)),
            pl.BlockSpec((1, 1, d), lambda i: ((i * tm) // seq, 0, 0)),
            pl.BlockSpec((d, ep), lambda i: (0, 0)),
        ],
        out_specs=[pl.BlockSpec((tm, d), lambda i: (i, 0)),
                   pl.BlockSpec((tm, ep), lambda i: (i, 0))],
        out_shape=[jax.ShapeDtypeStruct((n, d), BF16), jax.ShapeDtypeStruct((n, ep), F32)],
        compiler_params=_cparams("parallel"),
        name="norm_router",
    )(x2, g, sc, sh, w_router_padded)


def _proj_res_body(m_ref, w_ref, x_ref, gate_ref, o_ref):
    y = jnp.dot(m_ref[...], w_ref[...], preferred_element_type=F32)
    o_ref[...] = x_ref[...] + gate_ref[0] * y


def _glu_res_body(m_ref, w_ref, x_ref, gate_ref, o_ref):
    d = x_ref.shape[-1]
    y = jnp.dot(m_ref[...].astype(BF16), w_ref[...], preferred_element_type=F32)
    o_ref[...] = x_ref[...] + gate_ref[0] * (y[:, :d] * jax.nn.sigmoid(y[:, d:]))


def proj_residual(m2, w, x2, gate, seq, glu=False):
    n, d = x2.shape
    f = w.shape[1]
    tm = _row_tile(seq, 512)
    return pl.pallas_call(
        _glu_res_body if glu else _proj_res_body,
        grid=(n // tm,),
        in_specs=[
            pl.BlockSpec((tm, d), lambda i: (i, 0)),
            pl.BlockSpec((d, f), lambda i: (0, 0)),
            pl.BlockSpec((tm, d), lambda i: (i, 0)),
            pl.BlockSpec((1, 1, d), lambda i: ((i * tm) // seq, 0, 0)),
        ],
        out_specs=pl.BlockSpec((tm, d), lambda i: (i, 0)),
        out_shape=jax.ShapeDtypeStruct((n, d), F32),
        compiler_params=_cparams("parallel"),
        name="glu_residual" if glu else "proj_residual",
    )(m2, w, x2, gate)


def _res_norm_body(x_ref, y_ref, gate_ref, g_ref, o_ref):
    x = x_ref[...] + gate_ref[0] * y_ref[...]
    ms = jnp.mean(x * x, axis=-1, keepdims=True)
    o_ref[...] = x * lax.rsqrt(ms + EPS) * g_ref[...]


def residual_final_norm(x2, y2, gate, g, seq):
    n, d = x2.shape
    tm = _row_tile(seq)
    return pl.pallas_call(
        _res_norm_body,
        grid=(n // tm,),
        in_specs=[
            pl.BlockSpec((tm, d), lambda i: (i, 0)),
            pl.BlockSpec((tm, d), lambda i: (i, 0)),
            pl.BlockSpec((1, 1, d), lambda i: ((i * tm) // seq, 0, 0)),
            pl.BlockSpec((1, d), lambda i: (0, 0)),
        ],
        out_specs=pl.BlockSpec((tm, d), lambda i: (i, 0)),
        out_shape=jax.ShapeDtypeStruct((n, d), F32),
        compiler_params=_cparams("parallel"),
        name="residual_final_norm",
    )(x2, y2, gate, g)


def _dot_nt(a, b):
    return lax.dot_general(a, b, (((1,), (1,)), ((), ())), preferred_element_type=F32)


def _hgrn_gates(z, lb, l1, om):
    e = jnp.exp(-jnp.abs(z))
    r = 1.0 / (1.0 + e)
    pos = z >= 0.0
    k = om * (jnp.where(pos, e, 1.0) * r)
    f = lb + om * (jnp.where(pos, 1.0, e) * r)
    log_sig = jnp.minimum(z, 0.0) + jnp.log(r)
    return jnp.maximum(jnp.log(f), l1 + log_sig), k


def _split2(x):
    hi = x.astype(BF16)
    return hi, (x - hi.astype(F32)).astype(BF16)


def _bcast_rows(x, rows, c, n_sub):
    return jnp.concatenate(
        [jnp.broadcast_to(x[rows[j]:rows[j] + 1, :], (c, x.shape[1])) for j in range(n_sub)], axis=0)


def _hgrn_intra(q, v, z, lb, l1, om, cum_mat, causal, ref_row, last_row, c, n_sub):
    logf, k = _hgrn_gates(z, lb, l1, om)
    l_hi, l_lo = _split2(logf)
    b = (jnp.dot(cum_mat, l_hi, preferred_element_type=F32)
         + jnp.dot(cum_mat, l_lo, preferred_element_type=F32))
    b_ref = _bcast_rows(b, [j * c + ref_row for j in range(n_sub)], c, n_sub)
    b_last = _bcast_rows(b, [j * c + last_row for j in range(n_sub)], c, n_sub)
    qd = (q * jnp.exp(b - b_ref)).astype(BF16)
    kd = (k * jnp.exp(b_ref - b)).astype(BF16)
    s = jnp.where(causal, _dot_nt(qd, kd), 0.0).astype(BF16)
    o = jnp.dot(s, v.astype(BF16), preferred_element_type=F32)
    qe = (q * jnp.exp(b)).astype(BF16)
    kl = (k * jnp.exp(b_last - b)).astype(BF16)
    kvs, decs = [], []
    for j in range(n_sub):
        r = slice(j * c, (j + 1) * c)
        kvs.append(jnp.dot(v[r].T.astype(BF16), kl[r], preferred_element_type=F32))
        decs.append(jnp.exp(b[j * c + last_row:j * c + last_row + 1, :]))
    return o, qe, kvs, decs


def _hgrn_body(q_ref, v_ref, zf_ref, zb_ref, g_ref, lb_ref, l1_ref, om_ref, ng_ref, o_ref,
               oi_scr, qe_scr, kv_scr, dec_scr, st_scr):
    seq = q_ref.shape[1]
    c = min(HG_CHUNK, seq)
    n_chunks = seq // c
    n_sub = math.gcd(n_chunks, HG_SUPER)
    sc = n_sub * c
    lb, l1, om = lb_ref[0], l1_ref[0], om_ref[0]
    row = lax.broadcasted_iota(jnp.int32, (sc, sc), 0)
    col = lax.broadcasted_iota(jnp.int32, (sc, sc), 1)
    same = (row // c) == (col // c)
    lower, upper = same & (row >= col), same & (row <= col)
    tril, triu = lower.astype(BF16), upper.astype(BF16)
    mid = c // 2

    def intra(i, carry):
        r = pl.ds(pl.multiple_of(i * sc, sc), sc)
        q, v = q_ref[0, r, :], v_ref[0, r, :]
        of, qf, kvf, decf = _hgrn_intra(q, v, zf_ref[0, r, :], lb, l1, om, tril, lower, mid - 1, c - 1, c, n_sub)
        ob, qb, kvb, decb = _hgrn_intra(q, v, zb_ref[0, r, :], lb, l1, om, triu, upper, c - mid, 0, c, n_sub)
        oi_scr[r, :] = of + ob
        qe_scr[r, 0:LANES] = qf
        qe_scr[r, LANES:2 * LANES] = qb
        for j in range(n_sub):
            n = i * n_sub + j
            kv_scr[0, n], kv_scr[1, n] = kvf[j], kvb[j]
            dec_scr[0, n], dec_scr[1, n] = jnp.broadcast_to(decf[j], (8, LANES)), jnp.broadcast_to(decb[j], (8, LANES))
        return carry

    lax.fori_loop(0, n_chunks // n_sub, intra, 0)

    def carry_state(n, carry):
        sf, sb = carry
        nb = n_chunks - 1 - n
        st_scr[n, :, 0:LANES] = sf.astype(BF16)
        st_scr[nb, :, LANES:2 * LANES] = sb.astype(BF16)
        sf = sf * dec_scr[0, n, 0:1, :] + kv_scr[0, n]
        sb = sb * dec_scr[1, nb, 0:1, :] + kv_scr[1, nb]
        return sf, sb

    zero = jnp.zeros((LANES, LANES), F32)
    lax.fori_loop(0, n_chunks, carry_state, (zero, zero))

    n_fin = math.gcd(n_chunks, HG_FINISH)
    fc = n_fin * c

    def finish(i, carry):
        parts = []
        for j in range(n_fin):
            n = i * n_fin + j
            parts.append(_dot_nt(qe_scr[pl.ds(pl.multiple_of(n * c, c), c), :], st_scr[n]))
        r = pl.ds(pl.multiple_of(i * fc, fc), fc)
        o = oi_scr[r, :] + jnp.concatenate(parts, axis=0)
        o = o * lax.rsqrt(jnp.mean(o * o, axis=-1, keepdims=True) + EPS) * ng_ref[...]
        o_ref[0, r, :] = (o * jax.nn.sigmoid(g_ref[0, r, :])).astype(o_ref.dtype)
        return carry

    lax.fori_loop(0, n_chunks // n_fin, finish, 0)


def hgrn_recurrence(proj, lb, l1, om, norm_g):
    bsz, seq, d5 = proj.shape
    d = d5 // 5
    h = HG_HEADS
    dk = d // h
    assert dk == LANES
    n_chunks = seq // min(HG_CHUNK, seq)

    def col(section):
        return pl.BlockSpec((1, seq, dk), lambda b, hh: (b, 0, section * h + hh))

    par = pl.BlockSpec((1, 1, dk), lambda b, hh: (hh, 0, 0))
    return pl.pallas_call(
        _hgrn_body,
        grid=(bsz, h),
        in_specs=[col(0), col(1), col(2), col(3), col(4), par, par, par,
                  pl.BlockSpec((1, dk), lambda b, hh: (0, 0))],
        out_specs=pl.BlockSpec((1, seq, dk), lambda b, hh: (b, 0, hh)),
        out_shape=jax.ShapeDtypeStruct((bsz, seq, d), BF16),
        scratch_shapes=[pltpu.VMEM((seq, dk), F32), pltpu.VMEM((seq, 2 * dk), BF16),
                        pltpu.VMEM((2, n_chunks, dk, dk), F32), pltpu.VMEM((2, n_chunks, 8, dk), F32),
                        pltpu.VMEM((n_chunks, dk, 2 * dk), BF16)],
        compiler_params=_cparams("parallel", "parallel"),
        name="hgrn_recurrence",
    )(proj, proj, proj, proj, proj, lb, l1, om, norm_g)


def _rope_pair(x, cos, sin_signed, swap):
    partner = jnp.dot(x.astype(BF16), swap, preferred_element_type=F32)
    return x * cos + partner * sin_signed


def _attn_group(q_ref, k_ref, v_ref, cos_ref, sin_ref, qs, ks, vs, ot, mt, lt, s_scr, p_scr, dil, half):
    seq = q_ref.shape[1]
    length = seq // dil
    dh = LANES // 2
    lane = lax.broadcasted_iota(jnp.int32, (1, LANES), 1)
    src_lane = lax.broadcasted_iota(jnp.int32, (LANES, LANES), 0)
    dst_lane = lax.broadcasted_iota(jnp.int32, (LANES, LANES), 1)
    quarter = dh // 2
    swap = (src_lane == jnp.where((dst_lane % dh) < quarter, dst_lane + quarter, dst_lane - quarter)).astype(BF16)
    head0 = lane < dh
    tile = math.gcd(length, 256)
    for r in range(dil):
        for t0 in range(0, length, tile):
            if dil == 1:
                src = pl.ds(t0, tile)
            else:
                src = pl.ds(r + t0 * dil, tile, stride=dil)
            dst = pl.ds(r * length + t0, tile)
            cos, sin = cos_ref[0, dst, :], sin_ref[0, dst, :]
            qs[dst, :] = (_rope_pair(q_ref[0, src, :], cos, sin, swap) * (dh ** -0.5)).astype(BF16)
            ks[dst, :] = _rope_pair(k_ref[0, src, :], cos, sin, swap).astype(BF16)
            vs[dst, :] = v_ref[0, src, :].astype(BF16)

    qb = min(DA_QBLOCK, length)
    span = min(qb + 2 * half, length)
    n_blocks = seq // qb
    seg = math.gcd(n_blocks, DA_SEGMENT)
    delta = lax.broadcasted_iota(jnp.int32, (qb, span), 1) - lax.broadcasted_iota(jnp.int32, (qb, span), 0)

    def place(bi):
        row0 = pl.multiple_of(bi * qb, qb)
        r = row0 // length
        m0 = row0 - r * length
        k0 = jnp.clip(m0 - half, 0, length - span)
        return pl.ds(row0, qb), pl.ds(pl.multiple_of(r * length + k0, 16), span), k0 - m0

    def segment(si, carry):
        def scores(j, c):
            rows, krows, off = place(si * seg + j)
            q, kk = qs[rows, :], ks[krows, :]
            bias = jnp.where((delta >= -half - off) & (delta <= half - off), 0.0, -jnp.inf)
            zero = jnp.zeros_like(q)
            s_scr[j, 0, 0:qb, 0:span] = _dot_nt(jnp.where(head0, q, zero), kk) + bias
            s_scr[j, 1, 0:qb, 0:span] = _dot_nt(jnp.where(head0, zero, q), kk) + bias
            return c

        def softmax(j, c):
            rows, _, _ = place(si * seg + j)
            ms = []
            for h in range(2):
                s = s_scr[j, h, 0:qb, 0:span]
                m = jnp.max(s, axis=-1, keepdims=True)
                p_scr[j, h, 0:qb, 0:span] = jnp.exp(s - m).astype(BF16)
                ms.append(jnp.broadcast_to(m, (qb, LANES)))
            mt[rows, :] = jnp.where(head0, ms[0], ms[1])
            return c

        def values(j, c):
            rows, krows, _ = place(si * seg + j)
            vv = vs[krows, :]
            one = jnp.ones_like(vv)
            r0 = jnp.dot(p_scr[j, 0, 0:qb, 0:span], jnp.where(head0, vv, one), preferred_element_type=F32)
            r1 = jnp.dot(p_scr[j, 1, 0:qb, 0:span], jnp.where(head0, one, vv), preferred_element_type=F32)
            ot[rows, :] = jnp.where(head0, r0, r1)
            lt[rows, :] = pltpu.roll(jnp.where(head0, r1, r0), dh, 1)
            return c

        lax.fori_loop(0, seg, scores, 0, unroll=DA_UNROLL)
        lax.fori_loop(0, seg, softmax, 0, unroll=DA_UNROLL)
        lax.fori_loop(0, seg, values, 0, unroll=DA_UNROLL)
        return carry

    lax.fori_loop(0, n_blocks // seg, segment, 0)


def _attn_body(q_ref, k_ref, v_ref, cos_ref, sin_ref, o_ref, qs, ks, vs, ot, mt, lt, acc, mrun, lrun, s_scr, p_scr):
    g = pl.program_id(2)
    seq = q_ref.shape[1]
    for gi, (window, dil) in enumerate(DA_PATTERNS):
        half = window // (2 * dil)
        assert half % 16 == 0
        length = seq // dil
        tile = math.gcd(length, 256)

        @pl.when(g == gi)
        def _(dil=dil, half=half, gi=gi, length=length, tile=tile):
            _attn_group(q_ref, k_ref, v_ref, cos_ref, sin_ref, qs, ks, vs, ot, mt, lt, s_scr, p_scr, dil, half)
            for r in range(dil):
                for t0 in range(0, length, tile):
                    src = pl.ds(r * length + t0, tile)
                    dst = pl.ds(t0, tile) if dil == 1 else pl.ds(r + t0 * dil, tile, stride=dil)
                    o_new, m_new, l_new = ot[src, :], mt[src, :], lt[src, :]
                    if gi == 0:
                        acc[dst, :], mrun[dst, :], lrun[dst, :] = o_new, m_new, l_new
                    else:
                        m_old = mrun[dst, :]
                        m_all = jnp.maximum(m_old, m_new)
                        w_old, w_new = jnp.exp(m_old - m_all), jnp.exp(m_new - m_all)
                        acc[dst, :] = acc[dst, :] * w_old + o_new * w_new
                        lrun[dst, :] = lrun[dst, :] * w_old + l_new * w_new
                        mrun[dst, :] = m_all

    @pl.when(g == len(DA_PATTERNS) - 1)
    def _():
        tile = math.gcd(seq, 256)

        def finish(i, carry):
            r = pl.ds(pl.multiple_of(i * tile, tile), tile)
            o_ref[0, r, :] = (acc[r, :] / lrun[r, :]).astype(o_ref.dtype)
            return carry

        lax.fori_loop(0, seq // tile, finish, 0)


def _rope_tables(seq):
    dh = LANES // 2
    halfd = dh // 2
    inv = ROPE_THETA ** (-jnp.arange(halfd, dtype=F32) / halfd)
    cos_t, sin_t = [], []
    for _, dil in DA_PATTERNS:
        pos = jnp.arange(seq, dtype=F32).reshape(seq // dil, dil).T.reshape(seq)
        ang = pos[:, None] * inv[None, :]
        cos, sin = jnp.cos(ang), jnp.sin(ang)
        cos_t.append(jnp.tile(cos, (1, 4)))
        sin_t.append(jnp.concatenate([-sin, sin, -sin, sin], axis=1))
    return jnp.stack(cos_t), jnp.stack(sin_t)


def dilated_attention(proj):
    bsz, seq, d9 = proj.shape
    n_groups = len(DA_PATTERNS)
    d = d9 // (3 * n_groups)
    assert d // DA_HEADS == LANES // 2
    pairs = d // LANES
    cos, sin = _rope_tables(seq)
    qb_max = max(min(DA_QBLOCK, seq // dl) for _, dl in DA_PATTERNS)
    span_max = max(min(min(DA_QBLOCK, seq // dl) + 2 * (w // (2 * dl)), seq // dl) for w, dl in DA_PATTERNS)

    def col(part):
        return pl.BlockSpec((1, seq, LANES), lambda b, hp, g: (b, 0, (g * 3 + part) * pairs + hp))

    table = pl.BlockSpec((1, seq, LANES), lambda b, hp, g: (g, 0, 0))
    return pl.pallas_call(
        _attn_body,
        grid=(bsz, pairs, n_groups),
        in_specs=[col(0), col(1), col(2), table, table],
        out_specs=pl.BlockSpec((1, seq, LANES), lambda b, hp, g: (b, 0, hp)),
        out_shape=jax.ShapeDtypeStruct((bsz, seq, d), BF16),
        scratch_shapes=[pltpu.VMEM((seq, LANES), BF16)] * 3 + [pltpu.VMEM((seq, LANES), F32)] * 6
        + [pltpu.VMEM((DA_SEGMENT, 2, qb_max, span_max), F32),
           pltpu.VMEM((DA_SEGMENT, 2, qb_max, span_max), BF16)],
        compiler_params=_cparams("parallel", "parallel", "arbitrary"),
        name="dilated_attention",
    )(proj, proj, proj, cos, sin)


def _s5_operators(a_re, a_im, log_dt, b_re, b_im, c_re, c_im):
    lc, i_dim, p_dim = S5_CHUNK, S5_GROUP, S5_STATE
    n_groups = a_re.shape[1]
    hp = dict(precision=HIGHEST)
    a = lax.complex(a_re.astype(F32), a_im.astype(F32))
    lam = a * jnp.exp(log_dt.astype(F32))[..., None]
    a_bar = jnp.exp(lam)
    bmat = lax.complex(b_re.astype(F32), b_im.astype(F32))
    cmat = lax.complex(c_re.astype(F32), c_im.astype(F32))
    b_bar = ((a_bar - 1.0) / a)[..., None] * bmat[None]
    tau = jnp.arange(lc + 1, dtype=F32)
    apow = jnp.exp(lam[:, :, None, :] * tau[None, None, :, None])
    kern = jnp.real(jnp.einsum('gip,dgtp,dgpj->dgtij', cmat, apow[:, :, :lc], b_bar, **hp))
    s_idx = jnp.arange(lc)[:, None]
    t_idx = jnp.arange(lc)[None, :]
    lag = t_idx - s_idx
    m_f = jnp.where((lag >= 0)[None, :, :, None, None], kern[0][:, jnp.clip(lag, 0)], 0.0)
    m_b = jnp.where((lag <= 0)[None, :, :, None, None], kern[1][:, jnp.clip(-lag, 0)], 0.0)
    m_op = (m_f + m_b).transpose(0, 1, 4, 2, 3).reshape(n_groups, lc * i_dim, lc * i_dim)
    rev = jnp.arange(lc - 1, -1, -1)
    p_f = apow[0][:, rev][:, :, :, None] * b_bar[0][:, None]
    p_b = apow[1][:, :lc][:, :, :, None] * b_bar[1][:, None]
    p_op = jnp.stack([jnp.real(p_f), jnp.imag(p_f), jnp.real(p_b), jnp.imag(p_b)], axis=0)
    p_op = p_op.transpose(1, 2, 4, 0, 3).reshape(n_groups, lc * i_dim, 4, p_dim)
    q_f = cmat[:, None] * apow[0][:, 1:lc + 1][:, :, None, :]
    q_b = cmat[:, None] * apow[1][:, lc - jnp.arange(lc)][:, :, None, :]
    q_op = jnp.stack([jnp.real(q_f), -jnp.imag(q_f), jnp.real(q_b), -jnp.imag(q_b)], axis=0)
    q_op = q_op.transpose(1, 0, 4, 2, 3).reshape(n_groups, 4, p_dim, lc * i_dim)
    n_pairs = n_groups // 2
    eye = jnp.eye(2, dtype=F32)
    w = lc * i_dim
    m_pair = m_op.reshape(n_pairs, 2, w, w)
    p_pair = jnp.einsum('narqp,ab->narqbp', p_op.reshape(n_pairs, 2, w, 4, p_dim), eye)
    p_pair = p_pair.reshape(n_pairs, 2 * w, 8 * p_dim)
    q_pair = jnp.einsum('nbqpc,ab->nqbpac', q_op.reshape(n_pairs, 2, 4, p_dim, w), eye)
    q_pair = q_pair.reshape(n_pairs, 8 * p_dim, 2 * w)
    lam_chunk = (lam * lc).reshape(2, n_groups // 8, 8 * p_dim)
    tiles = n_groups // 8
    return (m_pair.astype(BF16).reshape(tiles, 4, 2, w, w), p_pair.astype(BF16).reshape(tiles, 4, 2 * w, 8 * p_dim),
            q_pair.astype(BF16).reshape(tiles, 4, 8 * p_dim, 2 * w), lam_chunk)


def _cmul(ar, ai, xr, xi):
    return ar * xr - ai * xi, ar * xi + ai * xr


def _s5_body(h_ref, perm_ref, m_ref, p_ref, q_ref, pw_ref, d_ref, o_ref, u_scr, uc_scr, v_scr, s_scr, yc_scr):
    seq = h_ref.shape[1]
    lc = S5_CHUNK
    n_chunks = seq // lc
    n_seg = S5_SEGMENTS
    ns = n_chunks // n_seg
    n_pairs = m_ref.shape[1]
    pw = p_ref.shape[2]
    sw = pw // 4
    qw = n_pairs * sw

    for s in range(n_seg):
        for t in range(lc):
            u_scr[t, pl.ds(s, ns, stride=n_seg), :] = h_ref[0, pl.ds(lc * s * ns + t, ns, stride=lc), :]
    u_all = jnp.concatenate([u_scr[t].astype(BF16) for t in range(lc)], axis=1)
    uc_scr[...] = jnp.dot(u_all, perm_ref[...], preferred_element_type=F32).astype(BF16)
    for p in range(n_pairs):
        vp = jnp.dot(uc_scr[:, p * pw:(p + 1) * pw], p_ref[0, p], preferred_element_type=F32)
        for c in range(4):
            v_scr[:, c * qw + p * sw:c * qw + (p + 1) * sw] = vp[:, c * sw:(c + 1) * sw]

    a_f = (pw_ref[0, 0, 1:2, :], pw_ref[0, 1, 1:2, :])
    a_b = (pw_ref[0, 2, 1:2, :], pw_ref[0, 3, 1:2, :])
    zero = jnp.zeros((n_seg, qw), F32)

    def scan(k, carry):
        f_re, f_im, b_re, b_im = carry
        rf = pl.ds(pl.multiple_of(k * n_seg, n_seg), n_seg)
        s_scr[rf, 0:qw] = f_re
        s_scr[rf, qw:2 * qw] = f_im
        n_re, n_im = _cmul(*a_f, f_re, f_im)
        rb = pl.ds(pl.multiple_of((ns - 1 - k) * n_seg, n_seg), n_seg)
        s_scr[rb, 2 * qw:3 * qw] = b_re
        s_scr[rb, 3 * qw:4 * qw] = b_im
        m_re, m_im = _cmul(*a_b, b_re, b_im)
        return (n_re + v_scr[rf, 0:qw], n_im + v_scr[rf, qw:2 * qw],
                m_re + v_scr[rb, 2 * qw:3 * qw], m_im + v_scr[rb, 3 * qw:4 * qw])

    f_re, f_im, b_re, b_im = lax.fori_loop(0, ns, scan, (zero, zero, zero, zero))

    a_seg_f = (pw_ref[0, 0, ns:ns + 1, :], pw_ref[0, 1, ns:ns + 1, :])
    a_seg_b = (pw_ref[0, 2, ns:ns + 1, :], pw_ref[0, 3, ns:ns + 1, :])
    row0 = jnp.zeros((1, qw), F32)
    cf = [(row0, row0)]
    for s in range(1, n_seg):
        xr, xi = _cmul(*a_seg_f, *cf[-1])
        cf.append((xr + f_re[s - 1:s, :], xi + f_im[s - 1:s, :]))
    cb = [(row0, row0)]
    for s in range(n_seg - 2, -1, -1):
        xr, xi = _cmul(*a_seg_b, *cb[0])
        cb.insert(0, (xr + b_re[s + 1:s + 2, :], xi + b_im[s + 1:s + 2, :]))
    cf_re, cf_im = (jnp.concatenate([c[j] for c in cf], axis=0) for j in range(2))
    cb_re, cb_im = (jnp.concatenate([c[j] for c in cb], axis=0) for j in range(2))

    def correct(k, carry):
        rf = pl.ds(pl.multiple_of(k * n_seg, n_seg), n_seg)
        xr, xi = _cmul(pw_ref[0, 0, pl.ds(k, 1), :], pw_ref[0, 1, pl.ds(k, 1), :], cf_re, cf_im)
        s_scr[rf, 0:qw] += xr
        s_scr[rf, qw:2 * qw] += xi
        kb = ns - 1 - k
        yr, yi = _cmul(pw_ref[0, 2, pl.ds(kb, 1), :], pw_ref[0, 3, pl.ds(kb, 1), :], cb_re, cb_im)
        s_scr[rf, 2 * qw:3 * qw] += yr
        s_scr[rf, 3 * qw:4 * qw] += yi
        return carry

    lax.fori_loop(0, ns, correct, 0)

    half = pw // 2
    for p in range(n_pairs):
        uc = uc_scr[:, p * pw:(p + 1) * pw]
        st = jnp.concatenate([s_scr[:, c * qw + p * sw:c * qw + (p + 1) * sw] for c in range(4)], axis=1)
        y = jnp.dot(st.astype(BF16), q_ref[0, p], preferred_element_type=F32)
        y = y + jnp.concatenate([jnp.dot(uc[:, :half], m_ref[0, p, 0], preferred_element_type=F32),
                                 jnp.dot(uc[:, half:], m_ref[0, p, 1], preferred_element_type=F32)], axis=1)
        yc_scr[:, p * pw:(p + 1) * pw] = y.astype(BF16)
    y_all = _dot_nt(yc_scr[...], perm_ref[...])
    for t in range(lc):
        u_scr[t] = y_all[:, t * LANES:(t + 1) * LANES]
    for s in range(n_seg):
        for t in range(lc):
            rows = pl.ds(lc * s * ns + t, ns, stride=lc)
            y = u_scr[t, pl.ds(s, ns, stride=n_seg), :] + d_ref[...] * h_ref[0, rows, :]
            o_ref[0, rows, :] = jax.nn.gelu(y)


def s5_mix(h3, ops, d_skip):
    m_op, p_op, q_op, lam_chunk = ops
    bsz, seq, d = h3.shape
    lc = S5_CHUNK
    tiles = d // LANES
    n_chunks = seq // lc
    width = lc * LANES
    assert seq % (lc * S5_SEGMENTS) == 0 and m_op.shape[0] == tiles
    ns = n_chunks // S5_SEGMENTS
    powers = jnp.exp(lam_chunk[:, :, None, :] * jnp.arange(ns + 1, dtype=F32)[None, None, :, None])
    pw_tab = jnp.stack([jnp.real(powers[0]), jnp.imag(powers[0]), jnp.real(powers[1]), jnp.imag(powers[1])], axis=1)
    src = jnp.arange(width)
    t_idx, g_idx, i_idx = src // LANES, (src % LANES) // S5_GROUP, src % S5_GROUP
    perm = (jnp.arange(width)[None, :] == (g_idx * (lc * S5_GROUP) + t_idx * S5_GROUP + i_idx)[:, None]).astype(BF16)
    return pl.pallas_call(
        _s5_body,
        grid=(tiles, bsz),
        in_specs=[
            pl.BlockSpec((1, seq, LANES), lambda l, b: (b, 0, l)),
            pl.BlockSpec((width, width), lambda l, b: (0, 0)),
            pl.BlockSpec((1,) + m_op.shape[1:], lambda l, b: (l, 0, 0, 0, 0)),
            pl.BlockSpec((1,) + p_op.shape[1:], lambda l, b: (l, 0, 0, 0)),
            pl.BlockSpec((1,) + q_op.shape[1:], lambda l, b: (l, 0, 0, 0)),
            pl.BlockSpec((1,) + pw_tab.shape[1:], lambda l, b: (l, 0, 0, 0)),
            pl.BlockSpec((1, LANES), lambda l, b: (0, l)),
        ],
        out_specs=pl.BlockSpec((1, seq, LANES), lambda l, b: (b, 0, l)),
        out_shape=jax.ShapeDtypeStruct((bsz, seq, d), F32),
        scratch_shapes=[pltpu.VMEM((lc, n_chunks, LANES), F32), pltpu.VMEM((n_chunks, width), BF16),
                        pltpu.VMEM((n_chunks, width), F32), pltpu.VMEM((n_chunks, width), F32),
                        pltpu.VMEM((n_chunks, width), BF16)],
        compiler_params=_cparams("parallel", "parallel"),
        name="s5_mix",
    )(h3, perm, m_op, p_op, q_op, pw_tab, d_skip.astype(F32)[None, :])


def _expert_body(x_ref, wg_ref, wu_ref, wd_ref, gate_ref, o_ref):
    f = pl.program_id(2)
    x = x_ref[0]
    a = jnp.dot(x, wg_ref[0], preferred_element_type=F32)
    u = jnp.dot(x, wu_ref[0], preferred_element_type=F32)
    hid = (a * jax.nn.sigmoid(a) * u).astype(BF16)
    part = jnp.dot(hid, wd_ref[0], preferred_element_type=F32)

    @pl.when(f == 0)
    def _():
        o_ref[0] = part

    @pl.when(f > 0)
    def _():
        o_ref[0] += part

    @pl.when(f == pl.num_programs(2) - 1)
    def _():
        o_ref[0] = o_ref[0] * gate_ref[0]


def expert_ffn(xe, w_gate, w_up, w_down, gate, tf=512):
    e, cap, d = xe.shape
    f = w_gate.shape[2]
    tm = math.gcd(cap, ROW_TILE)
    tf = math.gcd(f, tf)
    return pl.pallas_call(
        _expert_body,
        grid=(e, cap // tm, f // tf),
        in_specs=[
            pl.BlockSpec((1, tm, d), lambda ei, i, j: (ei, i, 0)),
            pl.BlockSpec((1, d, tf), lambda ei, i, j: (ei, 0, j)),
            pl.BlockSpec((1, d, tf), lambda ei, i, j: (ei, 0, j)),
            pl.BlockSpec((1, tf, d), lambda ei, i, j: (ei, j, 0)),
            pl.BlockSpec((1, tm, 1), lambda ei, i, j: (ei, i, 0)),
        ],
        out_specs=pl.BlockSpec((1, tm, d), lambda ei, i, j: (ei, i, 0)),
        out_shape=jax.ShapeDtypeStruct((e, cap, d), F32),
        compiler_params=_cparams("parallel", "parallel", "arbitrary"),
        name="expert_ffn",
    )(xe, w_gate, w_up, w_down, gate)


def expert_choice_moe(x2, seq, norm_g, sc, sh, w_router, w_gate, w_up, w_down):
    n, d = x2.shape
    e = w_router.shape[1]
    capacity = CAPACITY_FACTOR * n // e
    wr = jnp.zeros((d, LANES), F32).at[:, :e].set(w_router.astype(F32))
    h, logits = norm_router(x2, seq, norm_g, sc, sh, wr)
    affinity = jax.nn.softmax(logits[:, :e], axis=-1)
    gate, idx = lax.top_k(affinity.T, capacity)
    xe = h[idx]
    ye = expert_ffn(xe, w_gate, w_up, w_down, gate[..., None])
    return jnp.zeros((n, d), F32).at[idx.reshape(-1)].add(ye.reshape(-1, d))


def _trunk(x, c, params):
    (norm_mix_g, norm_ffn_g, ada_w, ada_b, hg_w_in, hg_w_out, hg_norm_g, lb_table,
     da_w_qkv, da_w_out, s5_ops, s5_d, s5_w_glu, moe_w_router, moe_w_gate, moe_w_up, moe_w_down, final_g) = params
    bsz, seq, d = x.shape
    depth = norm_mix_g.shape[0]
    x2 = x.reshape(bsz * seq, d)
    cond = jax.nn.silu(c)
    out = None
    for layer in range(depth):
        kind, slot = layer % N_MIXERS, layer // N_MIXERS
        mod = (jnp.dot(cond, ada_w[layer], precision=HIGHEST) + ada_b[layer])[:, None, :]
        sh1, sc1, g1, sh2, sc2, g2 = jnp.split(mod, 6, axis=-1)
        gmix = norm_mix_g[layer][None, :]
        if kind == 0:
            proj = nm_matmul(x2, seq, gmix, sc1, sh1, hg_w_in[slot])
            lb = lb_table[layer].reshape(HG_HEADS, 1, -1)
            m = hgrn_recurrence(proj.reshape(bsz, seq, -1), lb, jnp.log1p(-lb), 1.0 - lb,
                                hg_norm_g[slot][None, :].astype(F32))
            x2 = proj_residual(m.reshape(bsz * seq, d), hg_w_out[slot], x2, g1, seq)
        elif kind == 1:
            proj = nm_matmul(x2, seq, gmix, sc1, sh1, da_w_qkv[slot])
            m = dilated_attention(proj.reshape(bsz, seq, -1))
            x2 = proj_residual(m.reshape(bsz * seq, d), da_w_out[slot], x2, g1, seq)
        else:
            h = norm_mod(x2, seq, gmix, sc1, sh1)
            z = s5_mix(h.reshape(bsz, seq, d), s5_ops[slot], s5_d[slot])
            x2 = proj_residual(z.reshape(bsz * seq, d), s5_w_glu[slot], x2, g1, seq, glu=True)
        y = expert_choice_moe(x2, seq, norm_ffn_g[layer][None, :], sc2, sh2, moe_w_router[layer],
                              moe_w_gate[layer], moe_w_up[layer], moe_w_down[layer])
        if layer == depth - 1:
            out = residual_final_norm(x2, y, g2, final_g[None, :].astype(F32), seq)
        else:
            x2 = x2 + jnp.broadcast_to(g2, (bsz, seq, d)).reshape(bsz * seq, d) * y
    return out.reshape(bsz, seq, d)


def kernel(x_prompt, x_sample, c_prompt, c_sample, norm_mix_g, norm_ffn_g, ada_w, ada_b, hg_w_in, hg_w_out, hg_norm_g, hg_lb_logits, da_w_qkv, da_w_out, s5_a_re, s5_a_im, s5_log_dt, s5_b_re, s5_b_im, s5_c_re, s5_c_im, s5_d, s5_w_glu, moe_w_router, moe_w_gate, moe_w_up, moe_w_down, final_g):
    lb_table = jnp.cumsum(jax.nn.softmax(hg_lb_logits.astype(F32), axis=0), axis=0)
    lb_table = lb_table - lb_table[0:1]
    s5_ops = [_s5_operators(s5_a_re[s], s5_a_im[s], s5_log_dt[s], s5_b_re[s], s5_b_im[s],
                            s5_c_re[s], s5_c_im[s]) for s in range(s5_a_re.shape[0])]
    bf = lambda w: w.astype(BF16)
    params = (norm_mix_g.astype(F32), norm_ffn_g.astype(F32), ada_w, ada_b, bf(hg_w_in), bf(hg_w_out),
              hg_norm_g, lb_table, bf(da_w_qkv), bf(da_w_out), s5_ops, s5_d, bf(s5_w_glu), moe_w_router,
              bf(moe_w_gate), bf(moe_w_up), bf(moe_w_down), final_g)
    return (_trunk(x_prompt, c_prompt, params), _trunk(x_sample, c_sample, params))
```

```python
import functools
import math

import jax
import jax.numpy as jnp
from jax import lax
from jax.experimental import pallas as pl
from jax.experimental.pallas import tpu as pltpu

F32 = jnp.float32
BF16 = jnp.bfloat16
HIGHEST = lax.Precision.HIGHEST

EPS = 1e-6
N_MIXERS = 3
HG_HEADS = 8
HG_CHUNK = 64
HG_SUPER = 4
HG_FINISH = 8
DA_PATTERNS = ((128, 1), (512, 4), (2048, 16))
DA_HEADS = 16
DA_QBLOCK = 128
DA_SEGMENT = 8
DA_UNROLL = 8
ROPE_THETA = 10000.0
S5_GROUP = 16
S5_STATE = 64
S5_CHUNK = 16
S5_SEGMENTS = 8
CAPACITY_FACTOR = 2
EXPERT_ROWS = 16
ROUTE_TILE = 512
DISPATCH_WINDOW = 128
ROW_ALIGN = 16

LANES = 128
VMEM_LIMIT = 56 * 1024 * 1024
ROW_TILE = 1024


def _cparams(*sem):
    return pltpu.CompilerParams(dimension_semantics=sem, vmem_limit_bytes=VMEM_LIMIT)


def _row_tile(t, cap=ROW_TILE):
    return math.gcd(t, cap)


def _norm_mod(x, g, sc, sh):
    ms = jnp.mean(x * x, axis=-1, keepdims=True)
    return (x * lax.rsqrt(ms + EPS) * g) * (1.0 + sc) + sh


def _nm_matmul_body(x_ref, g_ref, sc_ref, sh_ref, w_ref, o_ref, h_scr):
    @pl.when(pl.program_id(1) == 0)
    def _():
        h_scr[...] = _norm_mod(x_ref[...], g_ref[...], sc_ref[0], sh_ref[0]).astype(BF16)

    o_ref[...] = jnp.dot(h_scr[...], w_ref[...], preferred_element_type=F32).astype(o_ref.dtype)


def nm_matmul(x2, seq, g, sc, sh, w, tn=512):
    n, d = x2.shape
    f = w.shape[1]
    tm = _row_tile(seq)
    tn = math.gcd(f, tn)
    return pl.pallas_call(
        _nm_matmul_body,
        grid=(n // tm, f // tn),
        in_specs=[
            pl.BlockSpec((tm, d), lambda i, j: (i, 0)),
            pl.BlockSpec((1, d), lambda i, j: (0, 0)),
            pl.BlockSpec((1, 1, d), lambda i, j: ((i * tm) // seq, 0, 0)),
            pl.BlockSpec((1, 1, d), lambda i, j: ((i * tm) // seq, 0, 0)),
            pl.BlockSpec((d, tn), lambda i, j: (0, j)),
        ],
        out_specs=pl.BlockSpec((tm, tn), lambda i, j: (i, j)),
        out_shape=jax.ShapeDtypeStruct((n, f), F32),
        scratch_shapes=[pltpu.VMEM((tm, d), BF16)],
        compiler_params=_cparams("parallel", "arbitrary"),
        name="nm_matmul",
    )(x2, g, sc, sh, w)


def _norm_mod_body(x_ref, g_ref, sc_ref, sh_ref, o_ref):
    o_ref[...] = _norm_mod(x_ref[...], g_ref[...], sc_ref[0], sh_ref[0])


def norm_mod(x2, seq, g, sc, sh):
    n, d = x2.shape
    tm = _row_tile(seq)
    return pl.pallas_call(
        _norm_mod_body,
        grid=(n // tm,),
        in_specs=[
            pl.BlockSpec((tm, d), lambda i: (i, 0)),
            pl.BlockSpec((1, d), lambda i: (0, 0)),
            pl.BlockSpec((1, 1, d), lambda i: ((i * tm) // seq, 0, 0)),
            pl.BlockSpec((1, 1, d), lambda i: ((i * tm) // seq, 0, 0)),
        ],
        out_specs=pl.BlockSpec((tm, d), lambda i: (i, 0)),
        out_shape=jax.ShapeDtypeStruct((n, d), F32),
        compiler_params=_cparams("parallel"),
        name="norm_mod",
    )(x2, g, sc, sh)


def _norm_router_body(x_ref, g_ref, sc_ref, sh_ref, wr_ref, h_ref, aff_ref, afft_ref, *, n_exp):
    h = _norm_mod(x_ref[...], g_ref[...], sc_ref[0], sh_ref[0])
    h_ref[...] = h.astype(BF16)
    logits = jnp.dot(h, wr_ref[...], precision=HIGHEST, preferred_element_type=F32)
    lane = lax.broadcasted_iota(jnp.int32, (1, LANES), 1)
    logits = jnp.where(lane < n_exp, logits, -jnp.inf)
    ex = jnp.exp(logits - jnp.max(logits, axis=-1, keepdims=True))
    aff = ex / jnp.sum(ex, axis=-1, keepdims=True)
    aff_ref[...] = aff
    afft_ref[...] = aff.T[:EXPERT_ROWS, :]


def norm_router(x2, seq, g, sc, sh, w_router):
    n, d = x2.shape
    n_exp = w_router.shape[1]
    assert n_exp <= EXPERT_ROWS
    wr = jnp.zeros((d, LANES), F32).at[:, :n_exp].set(w_router.astype(F32))
    tm = _row_tile(seq, ROUTE_TILE)
    return pl.pallas_call(
        functools.partial(_norm_router_body, n_exp=n_exp),
        grid=(n // tm,),
        in_specs=[
            pl.BlockSpec((tm, d), lambda i: (i, 0)),
            pl.BlockSpec((1, d), lambda i: (0, 0)),
            pl.BlockSpec((1, 1, d), lambda i: ((i * tm) // seq, 0, 0)),
            pl.BlockSpec((1, 1, d), lambda i: ((i * tm) // seq, 0, 0)),
            pl.BlockSpec((d, LANES), lambda i: (0, 0)),
        ],
        out_specs=[pl.BlockSpec((tm, d), lambda i: (i, 0)),
                   pl.BlockSpec((tm, LANES), lambda i: (i, 0)),
                   pl.BlockSpec((EXPERT_ROWS, tm), lambda i: (0, i))],
        out_shape=[jax.ShapeDtypeStruct((n, d), BF16), jax.ShapeDtypeStruct((n, LANES), F32),
                   jax.ShapeDtypeStruct((EXPERT_ROWS, n), F32)],
        compiler_params=_cparams("parallel"),
        name="norm_router",
    )(x2, g, sc, sh, wr)


def _proj_res_body(m_ref, w_ref, x_ref, gate_ref, o_ref):
    y = jnp.dot(m_ref[...], w_ref[...], preferred_element_type=F32)
    o_ref[...] = x_ref[...] + gate_ref[0] * y


def _glu_res_body(m_ref, w_ref, x_ref, gate_ref, o_ref):
    d = x_ref.shape[-1]
    y = jnp.dot(m_ref[...].astype(BF16), w_ref[...], preferred_element_type=F32)
    o_ref[...] = x_ref[...] + gate_ref[0] * (y[:, :d] * jax.nn.sigmoid(y[:, d:]))


def proj_residual(m2, w, x2, gate, seq, glu=False):
    n, d = x2.shape
    f = w.shape[1]
    tm = _row_tile(seq, 512)
    return pl.pallas_call(
        _glu_res_body if glu else _proj_res_body,
        grid=(n // tm,),
        in_specs=[
            pl.BlockSpec((tm, d), lambda i: (i, 0)),
            pl.BlockSpec((d, f), lambda i: (0, 0)),
            pl.BlockSpec((tm, d), lambda i: (i, 0)),
            pl.BlockSpec((1, 1, d), lambda i: ((i * tm) // seq, 0, 0)),
        ],
        out_specs=pl.BlockSpec((tm, d), lambda i: (i, 0)),
        out_shape=jax.ShapeDtypeStruct((n, d), F32),
        compiler_params=_cparams("parallel"),
        name="glu_residual" if glu else "proj_residual",
    )(m2, w, x2, gate)


def _dot_nt(a, b):
    return lax.dot_general(a, b, (((1,), (1,)), ((), ())), preferred_element_type=F32)


def _hgrn_gates(z, lb, l1, om):
    e = jnp.exp(-jnp.abs(z))
    r = 1.0 / (1.0 + e)
    pos = z >= 0.0
    k = om * (jnp.where(pos, e, 1.0) * r)
    f = lb + om * (jnp.where(pos, 1.0, e) * r)
    log_sig = jnp.minimum(z, 0.0) + jnp.log(r)
    return jnp.maximum(jnp.log(f), l1 + log_sig), k


def _split2(x):
    hi = x.astype(BF16)
    return hi, (x - hi.astype(F32)).astype(BF16)


def _bcast_rows(x, rows, c, n_sub):
    return jnp.concatenate(
        [jnp.broadcast_to(x[rows[j]:rows[j] + 1, :], (c, x.shape[1])) for j in range(n_sub)], axis=0)


def _hgrn_intra(q, v, z, lb, l1, om, cum_mat, causal, ref_row, last_row, c, n_sub):
    logf, k = _hgrn_gates(z, lb, l1, om)
    l_hi, l_lo = _split2(logf)
    b = (jnp.dot(cum_mat, l_hi, preferred_element_type=F32)
         + jnp.dot(cum_mat, l_lo, preferred_element_type=F32))
    b_ref = _bcast_rows(b, [j * c + ref_row for j in range(n_sub)], c, n_sub)
    b_last = _bcast_rows(b, [j * c + last_row for j in range(n_sub)], c, n_sub)
    qd = (q * jnp.exp(b - b_ref)).astype(BF16)
    kd = (k * jnp.exp(b_ref - b)).astype(BF16)
    s = jnp.where(causal, _dot_nt(qd, kd), 0.0).astype(BF16)
    o = jnp.dot(s, v.astype(BF16), preferred_element_type=F32)
    qe = (q * jnp.exp(b)).astype(BF16)
    kl = (k * jnp.exp(b_last - b)).astype(BF16)
    kvs, decs = [], []
    for j in range(n_sub):
        r = slice(j * c, (j + 1) * c)
        kvs.append(jnp.dot(v[r].T.astype(BF16), kl[r], preferred_element_type=F32))
        decs.append(jnp.exp(b[j * c + last_row:j * c + last_row + 1, :]))
    return o, qe, kvs, decs


def _hgrn_body(q_ref, v_ref, zf_ref, zb_ref, g_ref, lb_ref, l1_ref, om_ref, ng_ref, o_ref,
               oi_scr, qe_scr, kv_scr, dec_scr, st_scr):
    seq = q_ref.shape[1]
    c = min(HG_CHUNK, seq)
    n_chunks = seq // c
    n_sub = math.gcd(n_chunks, HG_SUPER)
    sc = n_sub * c
    lb, l1, om = lb_ref[0], l1_ref[0], om_ref[0]
    row = lax.broadcasted_iota(jnp.int32, (sc, sc), 0)
    col = lax.broadcasted_iota(jnp.int32, (sc, sc), 1)
    same = (row // c) == (col // c)
    lower, upper = same & (row >= col), same & (row <= col)
    tril, triu = lower.astype(BF16), upper.astype(BF16)
    mid = c // 2

    def intra(i, carry):
        r = pl.ds(pl.multiple_of(i * sc, sc), sc)
        q, v = q_ref[0, r, :], v_ref[0, r, :]
        of, qf, kvf, decf = _hgrn_intra(q, v, zf_ref[0, r, :], lb, l1, om, tril, lower, mid - 1, c - 1, c, n_sub)
        ob, qb, kvb, decb = _hgrn_intra(q, v, zb_ref[0, r, :], lb, l1, om, triu, upper, c - mid, 0, c, n_sub)
        oi_scr[r, :] = of + ob
        qe_scr[r, 0:LANES] = qf
        qe_scr[r, LANES:2 * LANES] = qb
        for j in range(n_sub):
            n = i * n_sub + j
            kv_scr[0, n], kv_scr[1, n] = kvf[j], kvb[j]
            dec_scr[0, n], dec_scr[1, n] = jnp.broadcast_to(decf[j], (8, LANES)), jnp.broadcast_to(decb[j], (8, LANES))
        return carry

    lax.fori_loop(0, n_chunks // n_sub, intra, 0)

    def carry_state(n, carry):
        sf, sb = carry
        nb = n_chunks - 1 - n
        st_scr[n, :, 0:LANES] = sf.astype(BF16)
        st_scr[nb, :, LANES:2 * LANES] = sb.astype(BF16)
        sf = sf * dec_scr[0, n, 0:1, :] + kv_scr[0, n]
        sb = sb * dec_scr[1, nb, 0:1, :] + kv_scr[1, nb]
        return sf, sb

    zero = jnp.zeros((LANES, LANES), F32)
    lax.fori_loop(0, n_chunks, carry_state, (zero, zero))

    n_fin = math.gcd(n_chunks, HG_FINISH)
    fc = n_fin * c

    def finish(i, carry):
        parts = []
        for j in range(n_fin):
            n = i * n_fin + j
            parts.append(_dot_nt(qe_scr[pl.ds(pl.multiple_of(n * c, c), c), :], st_scr[n]))
        r = pl.ds(pl.multiple_of(i * fc, fc), fc)
        o = oi_scr[r, :] + jnp.concatenate(parts, axis=0)
        o = o * lax.rsqrt(jnp.mean(o * o, axis=-1, keepdims=True) + EPS) * ng_ref[...]
        o_ref[0, r, :] = (o * jax.nn.sigmoid(g_ref[0, r, :])).astype(o_ref.dtype)
        return carry

    lax.fori_loop(0, n_chunks // n_fin, finish, 0)


def hgrn_recurrence(proj, lb, l1, om, norm_g):
    bsz, seq, d5 = proj.shape
    d = d5 // 5
    h = HG_HEADS
    dk = d // h
    assert dk == LANES
    n_chunks = seq // min(HG_CHUNK, seq)

    def col(section):
        return pl.BlockSpec((1, seq, dk), lambda b, hh: (b, 0, section * h + hh))

    par = pl.BlockSpec((1, 1, dk), lambda b, hh: (hh, 0, 0))
    return pl.pallas_call(
        _hgrn_body,
        grid=(bsz, h),
        in_specs=[col(0), col(1), col(2), col(3), col(4), par, par, par,
                  pl.BlockSpec((1, dk), lambda b, hh: (0, 0))],
        out_specs=pl.BlockSpec((1, seq, dk), lambda b, hh: (b, 0, hh)),
        out_shape=jax.ShapeDtypeStruct((bsz, seq, d), BF16),
        scratch_shapes=[pltpu.VMEM((seq, dk), F32), pltpu.VMEM((seq, 2 * dk), BF16),
                        pltpu.VMEM((2, n_chunks, dk, dk), F32), pltpu.VMEM((2, n_chunks, 8, dk), F32),
                        pltpu.VMEM((n_chunks, dk, 2 * dk), BF16)],
        compiler_params=_cparams("parallel", "parallel"),
        name="hgrn_recurrence",
    )(proj, proj, proj, proj, proj, lb, l1, om, norm_g)


def _rope_pair(x, cos, sin_signed, swap):
    partner = jnp.dot(x.astype(BF16), swap, preferred_element_type=F32)
    return x * cos + partner * sin_signed


def _attn_group(q_ref, k_ref, v_ref, cos_ref, sin_ref, qs, ks, vs, ot, mt, lt, s_scr, p_scr, dil, half):
    seq = q_ref.shape[1]
    length = seq // dil
    dh = LANES // 2
    lane = lax.broadcasted_iota(jnp.int32, (1, LANES), 1)
    src_lane = lax.broadcasted_iota(jnp.int32, (LANES, LANES), 0)
    dst_lane = lax.broadcasted_iota(jnp.int32, (LANES, LANES), 1)
    quarter = dh // 2
    swap = (src_lane == jnp.where((dst_lane % dh) < quarter, dst_lane + quarter, dst_lane - quarter)).astype(BF16)
    head0 = lane < dh
    tile = math.gcd(length, 256)
    for r in range(dil):
        for t0 in range(0, length, tile):
            if dil == 1:
                src = pl.ds(t0, tile)
            else:
                src = pl.ds(r + t0 * dil, tile, stride=dil)
            dst = pl.ds(r * length + t0, tile)
            cos, sin = cos_ref[0, dst, :], sin_ref[0, dst, :]
            qs[dst, :] = (_rope_pair(q_ref[0, src, :], cos, sin, swap) * (dh ** -0.5)).astype(BF16)
            ks[dst, :] = _rope_pair(k_ref[0, src, :], cos, sin, swap).astype(BF16)
            vs[dst, :] = v_ref[0, src, :].astype(BF16)

    qb = min(DA_QBLOCK, length)
    span = min(qb + 2 * half, length)
    n_blocks = seq // qb
    seg = math.gcd(n_blocks, DA_SEGMENT)
    delta = lax.broadcasted_iota(jnp.int32, (qb, span), 1) - lax.broadcasted_iota(jnp.int32, (qb, span), 0)

    def place(bi):
        row0 = pl.multiple_of(bi * qb, qb)
        r = row0 // length
        m0 = row0 - r * length
        k0 = jnp.clip(m0 - half, 0, length - span)
        return pl.ds(row0, qb), pl.ds(pl.multiple_of(r * length + k0, 16), span), k0 - m0

    def segment(si, carry):
        def scores(j, c):
            rows, krows, off = place(si * seg + j)
            q, kk = qs[rows, :], ks[krows, :]
            bias = jnp.where((delta >= -half - off) & (delta <= half - off), 0.0, -jnp.inf)
            zero = jnp.zeros_like(q)
            s_scr[j, 0, 0:qb, 0:span] = _dot_nt(jnp.where(head0, q, zero), kk) + bias
            s_scr[j, 1, 0:qb, 0:span] = _dot_nt(jnp.where(head0, zero, q), kk) + bias
            return c

        def softmax(j, c):
            rows, _, _ = place(si * seg + j)
            ms = []
            for h in range(2):
                s = s_scr[j, h, 0:qb, 0:span]
                m = jnp.max(s, axis=-1, keepdims=True)
                p_scr[j, h, 0:qb, 0:span] = jnp.exp(s - m).astype(BF16)
                ms.append(jnp.broadcast_to(m, (qb, LANES)))
            mt[rows, :] = jnp.where(head0, ms[0], ms[1])
            return c

        def values(j, c):
            rows, krows, _ = place(si * seg + j)
            vv = vs[krows, :]
            one = jnp.ones_like(vv)
            r0 = jnp.dot(p_scr[j, 0, 0:qb, 0:span], jnp.where(head0, vv, one), preferred_element_type=F32)
            r1 = jnp.dot(p_scr[j, 1, 0:qb, 0:span], jnp.where(head0, one, vv), preferred_element_type=F32)
            ot[rows, :] = jnp.where(head0, r0, r1)
            lt[rows, :] = pltpu.roll(jnp.where(head0, r1, r0), dh, 1)
            return c

        lax.fori_loop(0, seg, scores, 0, unroll=DA_UNROLL)
        lax.fori_loop(0, seg, softmax, 0, unroll=DA_UNROLL)
        lax.fori_loop(0, seg, values, 0, unroll=DA_UNROLL)
        return carry

    lax.fori_loop(0, n_blocks // seg, segment, 0)


def _attn_body(q_ref, k_ref, v_ref, cos_ref, sin_ref, o_ref, qs, ks, vs, ot, mt, lt, acc, mrun, lrun, s_scr, p_scr):
    g = pl.program_id(2)
    seq = q_ref.shape[1]
    for gi, (window, dil) in enumerate(DA_PATTERNS):
        half = window // (2 * dil)
        assert half % 16 == 0
        length = seq // dil
        tile = math.gcd(length, 256)

        @pl.when(g == gi)
        def _(dil=dil, half=half, gi=gi, length=length, tile=tile):
            _attn_group(q_ref, k_ref, v_ref, cos_ref, sin_ref, qs, ks, vs, ot, mt, lt, s_scr, p_scr, dil, half)
            for r in range(dil):
                for t0 in range(0, length, tile):
                    src = pl.ds(r * length + t0, tile)
                    dst = pl.ds(t0, tile) if dil == 1 else pl.ds(r + t0 * dil, tile, stride=dil)
                    o_new, m_new, l_new = ot[src, :], mt[src, :], lt[src, :]
                    if gi == 0:
                        acc[dst, :], mrun[dst, :], lrun[dst, :] = o_new, m_new, l_new
                    else:
                        m_old = mrun[dst, :]
                        m_all = jnp.maximum(m_old, m_new)
                        w_old, w_new = jnp.exp(m_old - m_all), jnp.exp(m_new - m_all)
                        acc[dst, :] = acc[dst, :] * w_old + o_new * w_new
                        lrun[dst, :] = lrun[dst, :] * w_old + l_new * w_new
                        mrun[dst, :] = m_all

    @pl.when(g == len(DA_PATTERNS) - 1)
    def _():
        tile = math.gcd(seq, 256)

        def finish(i, carry):
            r = pl.ds(pl.multiple_of(i * tile, tile), tile)
            o_ref[0, r, :] = (acc[r, :] / lrun[r, :]).astype(o_ref.dtype)
            return carry

        lax.fori_loop(0, seq // tile, finish, 0)


def _rope_tables(seq):
    dh = LANES // 2
    halfd = dh // 2
    inv = ROPE_THETA ** (-jnp.arange(halfd, dtype=F32) / halfd)
    cos_t, sin_t = [], []
    for _, dil in DA_PATTERNS:
        pos = jnp.arange(seq, dtype=F32).reshape(seq // dil, dil).T.reshape(seq)
        ang = pos[:, None] * inv[None, :]
        cos, sin = jnp.cos(ang), jnp.sin(ang)
        cos_t.append(jnp.tile(cos, (1, 4)))
        sin_t.append(jnp.concatenate([-sin, sin, -sin, sin], axis=1))
    return jnp.stack(cos_t), jnp.stack(sin_t)


def dilated_attention(proj):
    bsz, seq, d9 = proj.shape
    n_groups = len(DA_PATTERNS)
    d = d9 // (3 * n_groups)
    assert d // DA_HEADS == LANES // 2
    pairs = d // LANES
    cos, sin = _rope_tables(seq)
    qb_max = max(min(DA_QBLOCK, seq // dl) for _, dl in DA_PATTERNS)
    span_max = max(min(min(DA_QBLOCK, seq // dl) + 2 * (w // (2 * dl)), seq // dl) for w, dl in DA_PATTERNS)

    def col(part):
        return pl.BlockSpec((1, seq, LANES), lambda b, hp, g: (b, 0, (g * 3 + part) * pairs + hp))

    table = pl.BlockSpec((1, seq, LANES), lambda b, hp, g: (g, 0, 0))
    return pl.pallas_call(
        _attn_body,
        grid=(bsz, pairs, n_groups),
        in_specs=[col(0), col(1), col(2), table, table],
        out_specs=pl.BlockSpec((1, seq, LANES), lambda b, hp, g: (b, 0, hp)),
        out_shape=jax.ShapeDtypeStruct((bsz, seq, d), BF16),
        scratch_shapes=[pltpu.VMEM((seq, LANES), BF16)] * 3 + [pltpu.VMEM((seq, LANES), F32)] * 6
        + [pltpu.VMEM((DA_SEGMENT, 2, qb_max, span_max), F32),
           pltpu.VMEM((DA_SEGMENT, 2, qb_max, span_max), BF16)],
        compiler_params=_cparams("parallel", "parallel", "arbitrary"),
        name="dilated_attention",
    )(proj, proj, proj, cos, sin)


def _s5_operators(a_re, a_im, log_dt, b_re, b_im, c_re, c_im):
    lc, i_dim, p_dim = S5_CHUNK, S5_GROUP, S5_STATE
    n_groups = a_re.shape[1]
    hp = dict(precision=HIGHEST)
    a = lax.complex(a_re.astype(F32), a_im.astype(F32))
    lam = a * jnp.exp(log_dt.astype(F32))[..., None]
    a_bar = jnp.exp(lam)
    bmat = lax.complex(b_re.astype(F32), b_im.astype(F32))
    cmat = lax.complex(c_re.astype(F32), c_im.astype(F32))
    b_bar = ((a_bar - 1.0) / a)[..., None] * bmat[None]
    tau = jnp.arange(lc + 1, dtype=F32)
    apow = jnp.exp(lam[:, :, None, :] * tau[None, None, :, None])
    kern = jnp.real(jnp.einsum('gip,dgtp,dgpj->dgtij', cmat, apow[:, :, :lc], b_bar, **hp))
    s_idx = jnp.arange(lc)[:, None]
    t_idx = jnp.arange(lc)[None, :]
    lag = t_idx - s_idx
    m_f = jnp.where((lag >= 0)[None, :, :, None, None], kern[0][:, jnp.clip(lag, 0)], 0.0)
    m_b = jnp.where((lag <= 0)[None, :, :, None, None], kern[1][:, jnp.clip(-lag, 0)], 0.0)
    m_op = (m_f + m_b).transpose(0, 1, 4, 2, 3).reshape(n_groups, lc * i_dim, lc * i_dim)
    rev = jnp.arange(lc - 1, -1, -1)
    p_f = apow[0][:, rev][:, :, :, None] * b_bar[0][:, None]
    p_b = apow[1][:, :lc][:, :, :, None] * b_bar[1][:, None]
    p_op = jnp.stack([jnp.real(p_f), jnp.imag(p_f), jnp.real(p_b), jnp.imag(p_b)], axis=0)
    p_op = p_op.transpose(1, 2, 4, 0, 3).reshape(n_groups, lc * i_dim, 4, p_dim)
    q_f = cmat[:, None] * apow[0][:, 1:lc + 1][:, :, None, :]
    q_b = cmat[:, None] * apow[1][:, lc - jnp.arange(lc)][:, :, None, :]
    q_op = jnp.stack([jnp.real(q_f), -jnp.imag(q_f), jnp.real(q_b), -jnp.imag(q_b)], axis=0)
    q_op = q_op.transpose(1, 0, 4, 2, 3).reshape(n_groups, 4, p_dim, lc * i_dim)
    n_pairs = n_groups // 2
    eye = jnp.eye(2, dtype=F32)
    w = lc * i_dim
    m_pair = m_op.reshape(n_pairs, 2, w, w)
    p_pair = jnp.einsum('narqp,ab->narqbp', p_op.reshape(n_pairs, 2, w, 4, p_dim), eye)
    p_pair = p_pair.reshape(n_pairs, 2 * w, 8 * p_dim)
    q_pair = jnp.einsum('nbqpc,ab->nqbpac', q_op.reshape(n_pairs, 2, 4, p_dim, w), eye)
    q_pair = q_pair.reshape(n_pairs, 8 * p_dim, 2 * w)
    lam_chunk = (lam * lc).reshape(2, n_groups // 8, 8 * p_dim)
    tiles = n_groups // 8
    return (m_pair.astype(BF16).reshape(tiles, 4, 2, w, w), p_pair.astype(BF16).reshape(tiles, 4, 2 * w, 8 * p_dim),
            q_pair.astype(BF16).reshape(tiles, 4, 8 * p_dim, 2 * w), lam_chunk)


def _cmul(ar, ai, xr, xi):
    return ar * xr - ai * xi, ar * xi + ai * xr


def _s5_body(h_ref, perm_ref, m_ref, p_ref, q_ref, pw_ref, d_ref, o_ref, u_scr, uc_scr, v_scr, s_scr, yc_scr):
    seq = h_ref.shape[1]
    lc = S5_CHUNK
    n_chunks = seq // lc
    n_seg = S5_SEGMENTS
    ns = n_chunks // n_seg
    n_pairs = m_ref.shape[1]
    pw = p_ref.shape[2]
    sw = pw // 4
    qw = n_pairs * sw

    for s in range(n_seg):
        for t in range(lc):
            u_scr[t, pl.ds(s, ns, stride=n_seg), :] = h_ref[0, pl.ds(lc * s * ns + t, ns, stride=lc), :]
    u_all = jnp.concatenate([u_scr[t].astype(BF16) for t in range(lc)], axis=1)
    uc_scr[...] = jnp.dot(u_all, perm_ref[...], preferred_element_type=F32).astype(BF16)
    for p in range(n_pairs):
        vp = jnp.dot(uc_scr[:, p * pw:(p + 1) * pw], p_ref[0, p], preferred_element_type=F32)
        for c in range(4):
            v_scr[:, c * qw + p * sw:c * qw + (p + 1) * sw] = vp[:, c * sw:(c + 1) * sw]

    a_f = (pw_ref[0, 0, 1:2, :], pw_ref[0, 1, 1:2, :])
    a_b = (pw_ref[0, 2, 1:2, :], pw_ref[0, 3, 1:2, :])
    zero = jnp.zeros((n_seg, qw), F32)

    def scan(k, carry):
        f_re, f_im, b_re, b_im = carry
        rf = pl.ds(pl.multiple_of(k * n_seg, n_seg), n_seg)
        s_scr[rf, 0:qw] = f_re
        s_scr[rf, qw:2 * qw] = f_im
        n_re, n_im = _cmul(*a_f, f_re, f_im)
        rb = pl.ds(pl.multiple_of((ns - 1 - k) * n_seg, n_seg), n_seg)
        s_scr[rb, 2 * qw:3 * qw] = b_re
        s_scr[rb, 3 * qw:4 * qw] = b_im
        m_re, m_im = _cmul(*a_b, b_re, b_im)
        return (n_re + v_scr[rf, 0:qw], n_im + v_scr[rf, qw:2 * qw],
                m_re + v_scr[rb, 2 * qw:3 * qw], m_im + v_scr[rb, 3 * qw:4 * qw])

    f_re, f_im, b_re, b_im = lax.fori_loop(0, ns, scan, (zero, zero, zero, zero))

    a_seg_f = (pw_ref[0, 0, ns:ns + 1, :], pw_ref[0, 1, ns:ns + 1, :])
    a_seg_b = (pw_ref[0, 2, ns:ns + 1, :], pw_ref[0, 3, ns:ns + 1, :])
    row0 = jnp.zeros((1, qw), F32)
    cf = [(row0, row0)]
    for s in range(1, n_seg):
        xr, xi = _cmul(*a_seg_f, *cf[-1])
        cf.append((xr + f_re[s - 1:s, :], xi + f_im[s - 1:s, :]))
    cb = [(row0, row0)]
    for s in range(n_seg - 2, -1, -1):
        xr, xi = _cmul(*a_seg_b, *cb[0])
        cb.insert(0, (xr + b_re[s + 1:s + 2, :], xi + b_im[s + 1:s + 2, :]))
    cf_re, cf_im = (jnp.concatenate([c[j] for c in cf], axis=0) for j in range(2))
    cb_re, cb_im = (jnp.concatenate([c[j] for c in cb], axis=0) for j in range(2))

    def correct(k, carry):
        rf = pl.ds(pl.multiple_of(k * n_seg, n_seg), n_seg)
        xr, xi = _cmul(pw_ref[0, 0, pl.ds(k, 1), :], pw_ref[0, 1, pl.ds(k, 1), :], cf_re, cf_im)
        s_scr[rf, 0:qw] += xr
        s_scr[rf, qw:2 * qw] += xi
        kb = ns - 1 - k
        yr, yi = _cmul(pw_ref[0, 2, pl.ds(kb, 1), :], pw_ref[0, 3, pl.ds(kb, 1), :], cb_re, cb_im)
        s_scr[rf, 2 * qw:3 * qw] += yr
        s_scr[rf, 3 * qw:4 * qw] += yi
        return carry

    lax.fori_loop(0, ns, correct, 0)

    half = pw // 2
    for p in range(n_pairs):
        uc = uc_scr[:, p * pw:(p + 1) * pw]
        st = jnp.concatenate([s_scr[:, c * qw + p * sw:c * qw + (p + 1) * sw] for c in range(4)], axis=1)
        y = jnp.dot(st.astype(BF16), q_ref[0, p], preferred_element_type=F32)
        y = y + jnp.concatenate([jnp.dot(uc[:, :half], m_ref[0, p, 0], preferred_element_type=F32),
                                 jnp.dot(uc[:, half:], m_ref[0, p, 1], preferred_element_type=F32)], axis=1)
        yc_scr[:, p * pw:(p + 1) * pw] = y.astype(BF16)
    y_all = _dot_nt(yc_scr[...], perm_ref[...])
    for t in range(lc):
        u_scr[t] = y_all[:, t * LANES:(t + 1) * LANES]
    for s in range(n_seg):
        for t in range(lc):
            rows = pl.ds(lc * s * ns + t, ns, stride=lc)
            y = u_scr[t, pl.ds(s, ns, stride=n_seg), :] + d_ref[...] * h_ref[0, rows, :]
            o_ref[0, rows, :] = jax.nn.gelu(y)


def s5_mix(h3, ops, d_skip):
    m_op, p_op, q_op, lam_chunk = ops
    bsz, seq, d = h3.shape
    lc = S5_CHUNK
    tiles = d // LANES
    n_chunks = seq // lc
    width = lc * LANES
    assert seq % (lc * S5_SEGMENTS) == 0 and m_op.shape[0] == tiles
    ns = n_chunks // S5_SEGMENTS
    powers = jnp.exp(lam_chunk[:, :, None, :] * jnp.arange(ns + 1, dtype=F32)[None, None, :, None])
    pw_tab = jnp.stack([jnp.real(powers[0]), jnp.imag(powers[0]), jnp.real(powers[1]), jnp.imag(powers[1])], axis=1)
    src = jnp.arange(width)
    t_idx, g_idx, i_idx = src // LANES, (src % LANES) // S5_GROUP, src % S5_GROUP
    perm = (jnp.arange(width)[None, :] == (g_idx * (lc * S5_GROUP) + t_idx * S5_GROUP + i_idx)[:, None]).astype(BF16)
    return pl.pallas_call(
        _s5_body,
        grid=(tiles, bsz),
        in_specs=[
            pl.BlockSpec((1, seq, LANES), lambda l, b: (b, 0, l)),
            pl.BlockSpec((width, width), lambda l, b: (0, 0)),
            pl.BlockSpec((1,) + m_op.shape[1:], lambda l, b: (l, 0, 0, 0, 0)),
            pl.BlockSpec((1,) + p_op.shape[1:], lambda l, b: (l, 0, 0, 0)),
            pl.BlockSpec((1,) + q_op.shape[1:], lambda l, b: (l, 0, 0, 0)),
            pl.BlockSpec((1,) + pw_tab.shape[1:], lambda l, b: (l, 0, 0, 0)),
            pl.BlockSpec((1, LANES), lambda l, b: (0, l)),
        ],
        out_specs=pl.BlockSpec((1, seq, LANES), lambda l, b: (b, 0, l)),
        out_shape=jax.ShapeDtypeStruct((bsz, seq, d), F32),
        scratch_shapes=[pltpu.VMEM((lc, n_chunks, LANES), F32), pltpu.VMEM((n_chunks, width), BF16),
                        pltpu.VMEM((n_chunks, width), F32), pltpu.VMEM((n_chunks, width), F32),
                        pltpu.VMEM((n_chunks, width), BF16)],
        compiler_params=_cparams("parallel", "parallel"),
        name="s5_mix",
    )(h3, perm, m_op, p_op, q_op, pw_tab, d_skip.astype(F32)[None, :])


def _threshold_body(a_ref, thr_ref, cgt_ref, *, capacity):
    def bits():
        return lax.bitcast_convert_type(a_ref[...], jnp.int32)

    def step(i, thr):
        cand = thr | jnp.left_shift(jnp.int32(1), 30 - i)
        cnt = jnp.sum((bits() >= cand).astype(jnp.int32), axis=1, keepdims=True)
        return jnp.where(cnt >= capacity, cand, thr)

    thr = lax.fori_loop(0, 31, step, jnp.zeros((a_ref.shape[0], 1), jnp.int32))
    cgt = jnp.sum((bits() > thr).astype(jnp.int32), axis=1, keepdims=True)
    thr_ref[...] = jnp.broadcast_to(lax.bitcast_convert_type(thr, F32), thr_ref.shape)
    cgt_ref[...] = jnp.broadcast_to(cgt, cgt_ref.shape)


def expert_thresholds(aff_t, capacity):
    rows, n = aff_t.shape
    return pl.pallas_call(
        functools.partial(_threshold_body, capacity=capacity),
        grid=(1,),
        in_specs=[pl.BlockSpec((rows, n), lambda i: (0, 0))],
        out_specs=[pl.BlockSpec((rows, LANES), lambda i: (0, 0))] * 2,
        out_shape=[jax.ShapeDtypeStruct((rows, LANES), F32), jax.ShapeDtypeStruct((rows, LANES), jnp.int32)],
        compiler_params=_cparams("arbitrary"),
        name="expert_thresholds",
    )(aff_t)


def _tile_counts_body(a_ref, thr_ref, cgt_ref, ceq_ref):
    t = pl.program_id(0)
    a, thr = a_ref[...], thr_ref[...]
    cgt_ref[pl.ds(t, 1), :] = jnp.sum((a > thr).astype(jnp.int32), axis=0, keepdims=True)
    ceq_ref[pl.ds(t, 1), :] = jnp.sum((a == thr).astype(jnp.int32), axis=0, keepdims=True)


def tile_counts(aff, thr_l, tm):
    n = aff.shape[0]
    n_tiles = n // tm
    return pl.pallas_call(
        _tile_counts_body,
        grid=(n_tiles,),
        in_specs=[pl.BlockSpec((tm, LANES), lambda t: (t, 0)), pl.BlockSpec((1, LANES), lambda t: (0, 0))],
        out_specs=[pl.BlockSpec((n_tiles, LANES), lambda t: (0, 0))] * 2,
        out_shape=[jax.ShapeDtypeStruct((n_tiles, LANES), jnp.int32)] * 2,
        compiler_params=_cparams("arbitrary"),
        name="tile_counts",
    )(aff, thr_l)


def _slots_body(a_ref, thr_ref, need_ref, eqb_ref, selb_ref, slot_ref, slott_ref, *, n_exp):
    t = pl.program_id(0)
    tm = a_ref.shape[0]
    a, thr = a_ref[...], thr_ref[...]
    lane = lax.broadcasted_iota(jnp.int32, (1, LANES), 1)
    row = lax.broadcasted_iota(jnp.int32, (tm, tm), 0)
    col = lax.broadcasted_iota(jnp.int32, (tm, tm), 1)
    before = (row > col).astype(BF16)
    eq = a == thr
    eq_rank = jnp.dot(before, eq.astype(BF16), preferred_element_type=F32) + eqb_ref[pl.ds(t, 1), :].astype(F32)
    sel = ((a > thr) | (eq & (eq_rank < need_ref[...].astype(F32)))) & (lane < n_exp)
    pos = jnp.dot(before, sel.astype(BF16), preferred_element_type=F32) + selb_ref[pl.ds(t, 1), :].astype(F32)
    slot = jnp.where(sel, pos, -1.0)
    slot_ref[...] = slot.astype(jnp.int32)
    slott_ref[...] = slot.T[:EXPERT_ROWS, :].astype(jnp.int32)


def token_slots(aff, thr_l, need_l, eq_base, sel_base, n_exp, tm):
    n = aff.shape[0]
    n_tiles = n // tm
    full = lambda r: pl.BlockSpec((r, LANES), lambda t: (0, 0))
    return pl.pallas_call(
        functools.partial(_slots_body, n_exp=n_exp),
        grid=(n_tiles,),
        in_specs=[pl.BlockSpec((tm, LANES), lambda t: (t, 0)), full(1), full(1), full(n_tiles), full(n_tiles)],
        out_specs=[pl.BlockSpec((tm, LANES), lambda t: (t, 0)), pl.BlockSpec((EXPERT_ROWS, tm), lambda t: (0, t))],
        out_shape=[jax.ShapeDtypeStruct((n, LANES), jnp.int32), jax.ShapeDtypeStruct((EXPERT_ROWS, n), jnp.int32)],
        compiler_params=_cparams("parallel"),
        name="token_slots",
    )(aff, thr_l, need_l, eq_base, sel_base)


def _window_start(s):
    return pl.multiple_of((s // ROW_ALIGN) * ROW_ALIGN, ROW_ALIGN)


def _dispatch_body(base_ref, cnt_ref, slott_ref, h_ref, xe_ref, stage, extra, carry, sems, xsem, *, n_exp, capacity):
    t = pl.program_id(0)
    n_tiles = pl.num_programs(0)
    par = t % 2
    wc = stage.shape[2]
    sub = lax.broadcasted_iota(jnp.int32, (wc, 1), 0)

    @pl.when(t == 0)
    def _():
        carry[...] = jnp.zeros_like(carry)

    starts = [_window_start(base_ref[t * n_exp + e]) for e in range(n_exp)]
    onehot = jnp.concatenate([(slott_ref[e:e + 1, :] - starts[e] == sub) for e in range(n_exp)], axis=0)
    rows = jnp.dot(onehot.astype(BF16), h_ref[...], preferred_element_type=F32)

    def window_copy(e, k):
        return pltpu.make_async_copy(stage.at[par, e], xe_ref.at[e, pl.ds(starts[e] + k * wc, wc)], sems.at[e])

    @pl.when(t > 0)
    def _():
        for e in range(n_exp):
            pltpu.make_async_copy(stage.at[1 - par, e], xe_ref.at[e, pl.ds(0, wc)], sems.at[e]).wait()

    for e in range(n_exp):
        s = base_ref[t * n_exp + e]
        end16 = _window_start(s + cnt_ref[t * n_exp + e])
        n_win = (end16 - starts[e]) // wc + 1
        stage[par, e] = rows[e * wc:(e + 1) * wc].astype(BF16)
        stage[par, e, 0:ROW_ALIGN, :] += carry[e]
        window_copy(e, 0).start()

        def more(k, c, e=e):
            oh = slott_ref[e:e + 1, :] - (starts[e] + k * wc) == sub
            extra[...] = jnp.dot(oh.astype(BF16), h_ref[...], preferred_element_type=F32).astype(BF16)
            cp = pltpu.make_async_copy(extra, xe_ref.at[e, pl.ds(starts[e] + k * wc, wc)], xsem)
            cp.start()
            cp.wait()
            return c

        lax.fori_loop(1, n_win, more, 0)
        off = pl.multiple_of(end16 - starts[e] - (n_win - 1) * wc, ROW_ALIGN)

        @pl.when(n_win == 1)
        def _(e=e, off=off):
            carry[e] = stage[par, e, pl.ds(off, ROW_ALIGN), :]

        @pl.when(n_win > 1)
        def _(e=e, off=off):
            carry[e] = extra[pl.ds(off, ROW_ALIGN), :]

    @pl.when(t == n_tiles - 1)
    def _():
        for e in range(n_exp):
            window_copy(e, 0).wait()
        extra[...] = jnp.zeros_like(extra)
        for e in range(n_exp):
            cp = pltpu.make_async_copy(extra, xe_ref.at[e, pl.ds(capacity, wc)], xsem)
            cp.start()
            cp.wait()


def dispatch_rows(h, slot_t, base, cnt, n_exp, capacity, tm):
    n, d = h.shape
    n_tiles = n // tm
    wc = DISPATCH_WINDOW
    grid_spec = pltpu.PrefetchScalarGridSpec(
        num_scalar_prefetch=2,
        grid=(n_tiles,),
        in_specs=[pl.BlockSpec((EXPERT_ROWS, tm), lambda t, b, c: (0, t)),
                  pl.BlockSpec((tm, d), lambda t, b, c: (t, 0))],
        out_specs=pl.BlockSpec(memory_space=pl.ANY),
        scratch_shapes=[pltpu.VMEM((2, n_exp, wc, d), BF16), pltpu.VMEM((wc, d), BF16),
                        pltpu.VMEM((n_exp, ROW_ALIGN, d), BF16),
                        pltpu.SemaphoreType.DMA((n_exp,)), pltpu.SemaphoreType.DMA(())],
    )
    return pl.pallas_call(
        functools.partial(_dispatch_body, n_exp=n_exp, capacity=capacity),
        grid_spec=grid_spec,
        out_shape=jax.ShapeDtypeStruct((n_exp, capacity + wc, d), BF16),
        compiler_params=_cparams("arbitrary"),
        name="dispatch_rows",
    )(base, cnt, slot_t, h)


def _expert_body(x_ref, wg_ref, wu_ref, wd_ref, o_ref, acc_ref):
    f = pl.program_id(2)
    x = x_ref[0]
    a = jnp.dot(x, wg_ref[0], preferred_element_type=F32)
    u = jnp.dot(x, wu_ref[0], preferred_element_type=F32)
    hid = (a * jax.nn.sigmoid(a) * u).astype(BF16)
    part = jnp.dot(hid, wd_ref[0], preferred_element_type=F32)

    @pl.when(f == 0)
    def _():
        acc_ref[...] = part

    @pl.when(f > 0)
    def _():
        acc_ref[...] += part

    @pl.when(f == pl.num_programs(2) - 1)
    def _():
        o_ref[0] = acc_ref[...].astype(o_ref.dtype)


def expert_ffn(xe, w_gate, w_up, w_down, capacity, tf=512):
    e, _, d = xe.shape
    f = w_gate.shape[2]
    tm = math.gcd(capacity, ROW_TILE)
    tf = math.gcd(f, tf)
    return pl.pallas_call(
        _expert_body,
        grid=(e, capacity // tm, f // tf),
        in_specs=[
            pl.BlockSpec((1, tm, d), lambda ei, i, j: (ei, i, 0)),
            pl.BlockSpec((1, d, tf), lambda ei, i, j: (ei, 0, j)),
            pl.BlockSpec((1, d, tf), lambda ei, i, j: (ei, 0, j)),
            pl.BlockSpec((1, tf, d), lambda ei, i, j: (ei, j, 0)),
        ],
        out_specs=pl.BlockSpec((1, tm, d), lambda ei, i, j: (ei, i, 0)),
        out_shape=jax.ShapeDtypeStruct((e, capacity, d), BF16),
        scratch_shapes=[pltpu.VMEM((tm, d), F32)],
        compiler_params=_cparams("parallel", "parallel", "arbitrary"),
        name="expert_ffn",
    )(xe, w_gate, w_up, w_down)


def _combine_body(base_ref, cnt_ref, slot_ref, aff_ref, x_ref, gate_ref, fg_ref, ye_ref, o_ref,
                  win, extra, acc, sems, xsem, *, n_exp, capacity, final_norm):
    t = pl.program_id(0)
    n_tiles = pl.num_programs(0)
    par = t % 2
    wc = win.shape[2]
    lanes = lax.broadcasted_iota(jnp.int32, (1, wc), 1)

    def start_of(tt, e, k):
        lo = _window_start(base_ref[tt * n_exp + e]) + k * wc
        return lo, pl.multiple_of(jnp.minimum(lo, capacity - wc), ROW_ALIGN)

    def fetch(tt, slot_par, e):
        _, st = start_of(tt, e, 0)
        return pltpu.make_async_copy(ye_ref.at[e, pl.ds(st, wc)], win.at[slot_par, e], sems.at[slot_par, e])

    @pl.when(t == 0)
    def _():
        for e in range(n_exp):
            fetch(0, 0, e).start()

    @pl.when(t + 1 < n_tiles)
    def _():
        for e in range(n_exp):
            fetch(t + 1, 1 - par, e).start()

    def spread(e, lo, st):
        col = slot_ref[:, e:e + 1]
        return jnp.where((col - st == lanes) & (col >= lo), aff_ref[:, e:e + 1], 0.0).astype(BF16)

    src_e = lax.broadcasted_iota(jnp.int32, (LANES, n_exp * wc), 0)
    dst_e = lax.broadcasted_iota(jnp.int32, (LANES, n_exp * wc), 1) // wc
    expand = (src_e == dst_e).astype(BF16)
    s1 = slot_ref[...] + 1
    rep = lambda v: jnp.dot(v.astype(BF16), expand, preferred_element_type=F32)
    slot_rep = rep(s1 // 64) * 64.0 + rep(s1 % 64) - 1.0
    aff_rep = rep(aff_ref[...])
    lo_vec = jnp.concatenate([jnp.full((1, wc), start_of(t, e, 0)[0], jnp.int32) for e in range(n_exp)], axis=1)
    st_vec = jnp.concatenate([jnp.full((1, wc), start_of(t, e, 0)[1], jnp.int32) for e in range(n_exp)], axis=1)
    lane_in_win = lax.broadcasted_iota(jnp.int32, (1, n_exp * wc), 1) % wc
    hit = (slot_rep == (st_vec + lane_in_win).astype(F32)) & (slot_rep >= lo_vec.astype(F32))
    onehot = jnp.where(hit, aff_rep, 0.0).astype(BF16)
    for e in range(n_exp):
        fetch(t, par, e).wait()
    acc[...] = jnp.dot(onehot, win[par].reshape(n_exp * wc, win.shape[3]), preferred_element_type=F32)
    for e in range(n_exp):
        s = base_ref[t * n_exp + e]
        n_win = (s - _window_start(s) + cnt_ref[t * n_exp + e] + wc - 1) // wc

        def more(k, c, e=e):
            lo_k, st_k = start_of(t, e, k)
            cp = pltpu.make_async_copy(ye_ref.at[e, pl.ds(st_k, wc)], extra, xsem)
            cp.start()
            cp.wait()
            acc[...] += jnp.dot(spread(e, lo_k, st_k), extra[...], preferred_element_type=F32)
            return c

        lax.fori_loop(1, n_win, more, 0)

    x = x_ref[...] + gate_ref[0] * acc[...]
    if final_norm:
        x = x * lax.rsqrt(jnp.mean(x * x, axis=-1, keepdims=True) + EPS) * fg_ref[...]
    o_ref[...] = x


def combine_rows(slot, aff, x2, gate, final_g, ye, base, cnt, seq, n_exp, capacity, tm, final_norm):
    n, d = x2.shape
    n_tiles = n // tm
    wc = DISPATCH_WINDOW
    assert wc <= capacity <= 64 * 256 and capacity % ROW_ALIGN == 0 and seq % tm == 0
    grid_spec = pltpu.PrefetchScalarGridSpec(
        num_scalar_prefetch=2,
        grid=(n_tiles,),
        in_specs=[pl.BlockSpec((tm, LANES), lambda t, b, c: (t, 0)),
                  pl.BlockSpec((tm, LANES), lambda t, b, c: (t, 0)),
                  pl.BlockSpec((tm, d), lambda t, b, c: (t, 0)),
                  pl.BlockSpec((1, 1, d), lambda t, b, c: ((t * tm) // seq, 0, 0)),
                  pl.BlockSpec((1, d), lambda t, b, c: (0, 0)),
                  pl.BlockSpec(memory_space=pl.ANY)],
        out_specs=pl.BlockSpec((tm, d), lambda t, b, c: (t, 0)),
        scratch_shapes=[pltpu.VMEM((2, n_exp, wc, d), BF16), pltpu.VMEM((wc, d), BF16), pltpu.VMEM((tm, d), F32),
                        pltpu.SemaphoreType.DMA((2, n_exp)), pltpu.SemaphoreType.DMA(())],
    )
    return pl.pallas_call(
        functools.partial(_combine_body, n_exp=n_exp, capacity=capacity, final_norm=final_norm),
        grid_spec=grid_spec,
        out_shape=jax.ShapeDtypeStruct((n, d), F32),
        compiler_params=_cparams("arbitrary"),
        name="combine_rows",
    )(base, cnt, slot, aff, x2, gate, final_g, ye)


def expert_choice_moe(x2, seq, norm_g, sc, sh, gate2, final_g, w_router, w_gate, w_up, w_down, final_norm):
    n, d = x2.shape
    n_exp = w_router.shape[1]
    capacity = CAPACITY_FACTOR * n // n_exp
    tm = _row_tile(seq, ROUTE_TILE)
    h, aff, aff_t = norm_router(x2, seq, norm_g, sc, sh, w_router)
    thr, cgt = expert_thresholds(aff_t, capacity)
    pad = LANES - thr.shape[0]
    thr_l = jnp.pad(thr[:, 0], (0, pad))[None, :]
    need_l = jnp.pad(capacity - cgt[:, 0], (0, pad))[None, :]
    t_gt, t_eq = tile_counts(aff, thr_l, tm)
    eq_base = jnp.cumsum(t_eq, axis=0) - t_eq
    t_sel = t_gt + jnp.clip(need_l - eq_base, 0, t_eq)
    sel_base = jnp.cumsum(t_sel, axis=0) - t_sel
    slot, slot_t = token_slots(aff, thr_l, need_l, eq_base, sel_base, n_exp, tm)
    base = sel_base[:, :n_exp].reshape(-1)
    cnt = t_sel[:, :n_exp].reshape(-1)
    xe = dispatch_rows(h, slot_t, base, cnt, n_exp, capacity, tm)
    ye = expert_ffn(xe, w_gate, w_up, w_down, capacity)
    return combine_rows(slot, aff, x2, gate2, final_g, ye, base, cnt, seq, n_exp, capacity, tm, final_norm)


def _trunk(x, c, params):
    (norm_mix_g, norm_ffn_g, ada_w, ada_b, hg_w_in, hg_w_out, hg_norm_g, lb_table,
     da_w_qkv, da_w_out, s5_ops, s5_d, s5_w_glu, moe_w_router, moe_w_gate, moe_w_up, moe_w_down, final_g) = params
    bsz, seq, d = x.shape
    depth = norm_mix_g.shape[0]
    x2 = x.reshape(bsz * seq, d)
    cond = jax.nn.silu(c)
    for layer in range(depth):
        kind, slot = layer % N_MIXERS, layer // N_MIXERS
        mod = (jnp.dot(cond, ada_w[layer], precision=HIGHEST) + ada_b[layer])[:, None, :]
        sh1, sc1, g1, sh2, sc2, g2 = jnp.split(mod, 6, axis=-1)
        gmix = norm_mix_g[layer][None, :]
        if kind == 0:
            proj = nm_matmul(x2, seq, gmix, sc1, sh1, hg_w_in[slot])
            lb = lb_table[layer].reshape(HG_HEADS, 1, -1)
            m = hgrn_recurrence(proj.reshape(bsz, seq, -1), lb, jnp.log1p(-lb), 1.0 - lb,
                                hg_norm_g[slot][None, :].astype(F32))
            x2 = proj_residual(m.reshape(bsz * seq, d), hg_w_out[slot], x2, g1, seq)
        elif kind == 1:
            proj = nm_matmul(x2, seq, gmix, sc1, sh1, da_w_qkv[slot])
            m = dilated_attention(proj.reshape(bsz, seq, -1))
            x2 = proj_residual(m.reshape(bsz * seq, d), da_w_out[slot], x2, g1, seq)
        else:
            h = norm_mod(x2, seq, gmix, sc1, sh1)
            z = s5_mix(h.reshape(bsz, seq, d), s5_ops[slot], s5_d[slot])
            x2 = proj_residual(z.reshape(bsz * seq, d), s5_w_glu[slot], x2, g1, seq, glu=True)
        x2 = expert_choice_moe(x2, seq, norm_ffn_g[layer][None, :], sc2, sh2, g2, final_g[None, :].astype(F32),
                               moe_w_router[layer], moe_w_gate[layer], moe_w_up[layer], moe_w_down[layer],
                               final_norm=(layer == depth - 1))
    return x2.reshape(bsz, seq, d)


def kernel(x_prompt, x_sample, c_prompt, c_sample, norm_mix_g, norm_ffn_g, ada_w, ada_b, hg_w_in, hg_w_out, hg_norm_g, hg_lb_logits, da_w_qkv, da_w_out, s5_a_re, s5_a_im, s5_log_dt, s5_b_re, s5_b_im, s5_c_re, s5_c_im, s5_d, s5_w_glu, moe_w_router, moe_w_gate, moe_w_up, moe_w_down, final_g):
    lb_table = jnp.cumsum(jax.nn.softmax(hg_lb_logits.astype(F32), axis=0), axis=0)
    lb_table = lb_table - lb_table[0:1]
    s5_ops = [_s5_operators(s5_a_re[s], s5_a_im[s], s5_log_dt[s], s5_b_re[s], s5_b_im[s],
                            s5_c_re[s], s5_c_im[s]) for s in range(s5_a_re.shape[0])]
    bf = lambda w: w.astype(BF16)
    params = (norm_mix_g.astype(F32), norm_ffn_g.astype(F32), ada_w, ada_b, bf(hg_w_in), bf(hg_w_out),
              hg_norm_g, lb_table, bf(da_w_qkv), bf(da_w_out), s5_ops, s5_d, bf(s5_w_glu), moe_w_router,
              bf(moe_w_gate), bf(moe_w_up), bf(moe_w_down), final_g)
    return (_trunk(x_prompt, c_prompt, params), _trunk(x_sample, c_sample, params))
```

```python
import functools
import math

import jax
import jax.numpy as jnp
from jax import lax
from jax.experimental import pallas as pl
from jax.experimental.pallas import tpu as pltpu

F32 = jnp.float32
BF16 = jnp.bfloat16
HIGHEST = lax.Precision.HIGHEST

EPS = 1e-6
N_MIXERS = 3
HG_HEADS = 8
HG_CHUNK = 64
HG_SUPER = 4
HG_FINISH = 8
DA_PATTERNS = ((128, 1), (512, 4), (2048, 16))
DA_HEADS = 16
DA_QBLOCK = 128
DA_SEGMENT = 8
DA_UNROLL = 8
ROPE_THETA = 10000.0
S5_GROUP = 16
S5_STATE = 64
S5_CHUNK = 16
S5_SEGMENTS = 8
CAPACITY_FACTOR = 2
EXPERT_ROWS = 16
ROUTE_TILE = 512
DISPATCH_WINDOW = 128
ROW_ALIGN = 16

LANES = 128
VMEM_LIMIT = 56 * 1024 * 1024
ROW_TILE = 1024


def _cparams(*sem):
    return pltpu.CompilerParams(dimension_semantics=sem, vmem_limit_bytes=VMEM_LIMIT)


def _row_tile(t, cap=ROW_TILE):
    return math.gcd(t, cap)


def _norm_mod(x, g, sc, sh):
    ms = jnp.mean(x * x, axis=-1, keepdims=True)
    return (x * lax.rsqrt(ms + EPS) * g) * (1.0 + sc) + sh


def _split2(x):
    hi = x.astype(BF16)
    return hi, (x - hi.astype(F32)).astype(BF16)


def _nm_matmul_body(x_ref, g_ref, sc_ref, sh_ref, w_ref, o_ref, h_scr):
    @pl.when(pl.program_id(1) == 0)
    def _():
        h_scr[...] = _norm_mod(x_ref[...], g_ref[...], sc_ref[0], sh_ref[0]).astype(BF16)

    o_ref[...] = jnp.dot(h_scr[...], w_ref[...], preferred_element_type=F32).astype(o_ref.dtype)


def nm_matmul(x2, seq, g, sc, sh, w, tn=512):
    n, d = x2.shape
    f = w.shape[1]
    tm = _row_tile(seq)
    tn = math.gcd(f, tn)
    return pl.pallas_call(
        _nm_matmul_body,
        grid=(n // tm, f // tn),
        in_specs=[
            pl.BlockSpec((tm, d), lambda i, j: (i, 0)),
            pl.BlockSpec((1, d), lambda i, j: (0, 0)),
            pl.BlockSpec((1, 1, d), lambda i, j: ((i * tm) // seq, 0, 0)),
            pl.BlockSpec((1, 1, d), lambda i, j: ((i * tm) // seq, 0, 0)),
            pl.BlockSpec((d, tn), lambda i, j: (0, j)),
        ],
        out_specs=pl.BlockSpec((tm, tn), lambda i, j: (i, j)),
        out_shape=jax.ShapeDtypeStruct((n, f), BF16),
        scratch_shapes=[pltpu.VMEM((tm, d), BF16)],
        compiler_params=_cparams("parallel", "arbitrary"),
        name="nm_matmul",
    )(x2, g, sc, sh, w)


def _norm_mod_body(x_ref, g_ref, sc_ref, sh_ref, o_ref):
    o_ref[...] = _norm_mod(x_ref[...], g_ref[...], sc_ref[0], sh_ref[0])


def norm_mod(x2, seq, g, sc, sh):
    n, d = x2.shape
    tm = _row_tile(seq)
    return pl.pallas_call(
        _norm_mod_body,
        grid=(n // tm,),
        in_specs=[
            pl.BlockSpec((tm, d), lambda i: (i, 0)),
            pl.BlockSpec((1, d), lambda i: (0, 0)),
            pl.BlockSpec((1, 1, d), lambda i: ((i * tm) // seq, 0, 0)),
            pl.BlockSpec((1, 1, d), lambda i: ((i * tm) // seq, 0, 0)),
        ],
        out_specs=pl.BlockSpec((tm, d), lambda i: (i, 0)),
        out_shape=jax.ShapeDtypeStruct((n, d), F32),
        compiler_params=_cparams("parallel"),
        name="norm_mod",
    )(x2, g, sc, sh)


def _norm_router_body(x_ref, g_ref, sc_ref, sh_ref, wh_ref, wl_ref, h_ref, aff_ref, afft_ref, *, n_exp):
    h = _norm_mod(x_ref[...], g_ref[...], sc_ref[0], sh_ref[0])
    h_hi, h_lo = _split2(h)
    h_ref[...] = h_hi
    logits = (jnp.dot(h_hi, wh_ref[...], preferred_element_type=F32)
              + jnp.dot(h_lo, wh_ref[...], preferred_element_type=F32)
              + jnp.dot(h_hi, wl_ref[...], preferred_element_type=F32))
    lane = lax.broadcasted_iota(jnp.int32, (1, LANES), 1)
    logits = jnp.where(lane < n_exp, logits, -jnp.inf)
    ex = jnp.exp(logits - jnp.max(logits, axis=-1, keepdims=True))
    aff = ex / jnp.sum(ex, axis=-1, keepdims=True)
    aff_ref[...] = aff
    afft_ref[...] = aff.T[:EXPERT_ROWS, :]


def norm_router(x2, seq, g, sc, sh, w_router):
    n, d = x2.shape
    n_exp = w_router.shape[1]
    assert n_exp <= EXPERT_ROWS
    wr = jnp.zeros((d, LANES), F32).at[:, :n_exp].set(w_router.astype(F32))
    w_hi = wr.astype(BF16)
    w_lo = (wr - w_hi.astype(F32)).astype(BF16)
    tm = _row_tile(seq, ROUTE_TILE)
    return pl.pallas_call(
        functools.partial(_norm_router_body, n_exp=n_exp),
        grid=(n // tm,),
        in_specs=[
            pl.BlockSpec((tm, d), lambda i: (i, 0)),
            pl.BlockSpec((1, d), lambda i: (0, 0)),
            pl.BlockSpec((1, 1, d), lambda i: ((i * tm) // seq, 0, 0)),
            pl.BlockSpec((1, 1, d), lambda i: ((i * tm) // seq, 0, 0)),
            pl.BlockSpec((d, LANES), lambda i: (0, 0)),
            pl.BlockSpec((d, LANES), lambda i: (0, 0)),
        ],
        out_specs=[pl.BlockSpec((tm, d), lambda i: (i, 0)),
                   pl.BlockSpec((tm, LANES), lambda i: (i, 0)),
                   pl.BlockSpec((EXPERT_ROWS, tm), lambda i: (0, i))],
        out_shape=[jax.ShapeDtypeStruct((n, d), BF16), jax.ShapeDtypeStruct((n, LANES), F32),
                   jax.ShapeDtypeStruct((EXPERT_ROWS, n), F32)],
        compiler_params=_cparams("parallel"),
        name="norm_router",
    )(x2, g, sc, sh, w_hi, w_lo)


def _proj_res_body(m_ref, w_ref, x_ref, gate_ref, o_ref):
    y = jnp.dot(m_ref[...], w_ref[...], preferred_element_type=F32)
    o_ref[...] = x_ref[...] + gate_ref[0] * y


def _glu_res_body(m_ref, w_ref, x_ref, gate_ref, o_ref):
    d = x_ref.shape[-1]
    y = jnp.dot(m_ref[...].astype(BF16), w_ref[...], preferred_element_type=F32)
    o_ref[...] = x_ref[...] + gate_ref[0] * (y[:, :d] * jax.nn.sigmoid(y[:, d:]))


def proj_residual(m2, w, x2, gate, seq, glu=False):
    n, d = x2.shape
    f = w.shape[1]
    tm = _row_tile(seq, 512)
    return pl.pallas_call(
        _glu_res_body if glu else _proj_res_body,
        grid=(n // tm,),
        in_specs=[
            pl.BlockSpec((tm, d), lambda i: (i, 0)),
            pl.BlockSpec((d, f), lambda i: (0, 0)),
            pl.BlockSpec((tm, d), lambda i: (i, 0)),
            pl.BlockSpec((1, 1, d), lambda i: ((i * tm) // seq, 0, 0)),
        ],
        out_specs=pl.BlockSpec((tm, d), lambda i: (i, 0)),
        out_shape=jax.ShapeDtypeStruct((n, d), F32),
        compiler_params=_cparams("parallel"),
        name="glu_residual" if glu else "proj_residual",
    )(m2, w, x2, gate)


def _dot_nt(a, b):
    return lax.dot_general(a, b, (((1,), (1,)), ((), ())), preferred_element_type=F32)


def _hgrn_gates(z, lb, l1, om):
    e = jnp.exp(-jnp.abs(z))
    r = 1.0 / (1.0 + e)
    pos = z >= 0.0
    k = om * (jnp.where(pos, e, 1.0) * r)
    f = lb + om * (jnp.where(pos, 1.0, e) * r)
    log_sig = jnp.minimum(z, 0.0) + jnp.log(r)
    return jnp.maximum(jnp.log(f), l1 + log_sig), k


def _bcast_rows(x, rows, c, n_sub):
    return jnp.concatenate(
        [jnp.broadcast_to(x[rows[j]:rows[j] + 1, :], (c, x.shape[1])) for j in range(n_sub)], axis=0)


def _hgrn_intra(q, v, z, lb, l1, om, cum_mat, causal, ref_row, last_row, c, n_sub):
    logf, k = _hgrn_gates(z, lb, l1, om)
    l_hi, l_lo = _split2(logf)
    b = (jnp.dot(cum_mat, l_hi, preferred_element_type=F32)
         + jnp.dot(cum_mat, l_lo, preferred_element_type=F32))
    b_ref = _bcast_rows(b, [j * c + ref_row for j in range(n_sub)], c, n_sub)
    b_last = _bcast_rows(b, [j * c + last_row for j in range(n_sub)], c, n_sub)
    up, down = jnp.exp(b - b_ref), jnp.exp(b_ref - b)
    qd = (q * up).astype(BF16)
    kd = (k * down).astype(BF16)
    s = jnp.where(causal, _dot_nt(qd, kd), 0.0).astype(BF16)
    o = jnp.dot(s, v.astype(BF16), preferred_element_type=F32)
    qe = (q * (up * jnp.exp(b_ref))).astype(BF16)
    kl = (k * (down * jnp.exp(b_last - b_ref))).astype(BF16)
    kvs, decs = [], []
    for j in range(n_sub):
        r = slice(j * c, (j + 1) * c)
        kvs.append(jnp.dot(v[r].T.astype(BF16), kl[r], preferred_element_type=F32))
        decs.append(jnp.exp(b[j * c + last_row:j * c + last_row + 1, :]))
    return o, qe, kvs, decs


def _hgrn_body(q_ref, v_ref, zf_ref, zb_ref, g_ref, lb_ref, l1_ref, om_ref, ng_ref, o_ref,
               oi_scr, qe_scr, kv_scr, dec_scr, st_scr):
    seq = q_ref.shape[1]
    c = min(HG_CHUNK, seq)
    n_chunks = seq // c
    n_sub = math.gcd(n_chunks, HG_SUPER)
    sc = n_sub * c
    lb, l1, om = lb_ref[0], l1_ref[0], om_ref[0]
    row = lax.broadcasted_iota(jnp.int32, (sc, sc), 0)
    col = lax.broadcasted_iota(jnp.int32, (sc, sc), 1)
    same = (row // c) == (col // c)
    lower, upper = same & (row >= col), same & (row <= col)
    tril, triu = lower.astype(BF16), upper.astype(BF16)
    mid = c // 2

    def intra(i, carry):
        r = pl.ds(pl.multiple_of(i * sc, sc), sc)
        q, v = q_ref[0, r, :].astype(F32), v_ref[0, r, :].astype(F32)
        of, qf, kvf, decf = _hgrn_intra(q, v, zf_ref[0, r, :].astype(F32), lb, l1, om, tril, lower, mid - 1, c - 1, c, n_sub)
        ob, qb, kvb, decb = _hgrn_intra(q, v, zb_ref[0, r, :].astype(F32), lb, l1, om, triu, upper, c - mid, 0, c, n_sub)
        oi_scr[r, :] = of + ob
        qe_scr[r, 0:LANES] = qf
        qe_scr[r, LANES:2 * LANES] = qb
        for j in range(n_sub):
            n = i * n_sub + j
            kv_scr[0, n], kv_scr[1, n] = kvf[j], kvb[j]
            dec_scr[0, n], dec_scr[1, n] = jnp.broadcast_to(decf[j], (8, LANES)), jnp.broadcast_to(decb[j], (8, LANES))
        return carry

    lax.fori_loop(0, n_chunks // n_sub, intra, 0, unroll=2)

    def carry_state(n, carry):
        sf, sb = carry
        nb = n_chunks - 1 - n
        st_scr[n, :, 0:LANES] = sf.astype(BF16)
        st_scr[nb, :, LANES:2 * LANES] = sb.astype(BF16)
        sf = sf * dec_scr[0, n, 0:1, :] + kv_scr[0, n]
        sb = sb * dec_scr[1, nb, 0:1, :] + kv_scr[1, nb]
        return sf, sb

    zero = jnp.zeros((LANES, LANES), F32)
    lax.fori_loop(0, n_chunks, carry_state, (zero, zero))

    n_fin = math.gcd(n_chunks, HG_FINISH)
    fc = n_fin * c

    def finish(i, carry):
        parts = []
        for j in range(n_fin):
            n = i * n_fin + j
            parts.append(_dot_nt(qe_scr[pl.ds(pl.multiple_of(n * c, c), c), :], st_scr[n]))
        r = pl.ds(pl.multiple_of(i * fc, fc), fc)
        o = oi_scr[r, :] + jnp.concatenate(parts, axis=0)
        o = o * lax.rsqrt(jnp.mean(o * o, axis=-1, keepdims=True) + EPS) * ng_ref[...]
        o_ref[0, r, :] = (o * jax.nn.sigmoid(g_ref[0, r, :].astype(F32))).astype(o_ref.dtype)
        return carry

    lax.fori_loop(0, n_chunks // n_fin, finish, 0)


def hgrn_recurrence(proj, lb, l1, om, norm_g):
    bsz, seq, d5 = proj.shape
    d = d5 // 5
    h = HG_HEADS
    dk = d // h
    assert dk == LANES
    n_chunks = seq // min(HG_CHUNK, seq)

    def col(section):
        return pl.BlockSpec((1, seq, dk), lambda b, hh: (b, 0, section * h + hh))

    par = pl.BlockSpec((1, 1, dk), lambda b, hh: (hh, 0, 0))
    return pl.pallas_call(
        _hgrn_body,
        grid=(bsz, h),
        in_specs=[col(0), col(1), col(2), col(3), col(4), par, par, par,
                  pl.BlockSpec((1, dk), lambda b, hh: (0, 0))],
        out_specs=pl.BlockSpec((1, seq, dk), lambda b, hh: (b, 0, hh)),
        out_shape=jax.ShapeDtypeStruct((bsz, seq, d), BF16),
        scratch_shapes=[pltpu.VMEM((seq, dk), F32), pltpu.VMEM((seq, 2 * dk), BF16),
                        pltpu.VMEM((2, n_chunks, dk, dk), F32), pltpu.VMEM((2, n_chunks, 8, dk), F32),
                        pltpu.VMEM((n_chunks, dk, 2 * dk), BF16)],
        compiler_params=_cparams("parallel", "parallel"),
        name="hgrn_recurrence",
    )(proj, proj, proj, proj, proj, lb, l1, om, norm_g)


def _rope_pair(x, cos, sin_signed, swap):
    partner = jnp.dot(x.astype(BF16), swap, preferred_element_type=F32)
    return x * cos + partner * sin_signed


def _attn_group(q_ref, k_ref, v_ref, cos_ref, sin_ref, qs, ks, vs, ot, mt, lt, s_scr, p_scr, wide, dil, half):
    seq = q_ref.shape[1]
    length = seq // dil
    dh = LANES // 2
    lane = lax.broadcasted_iota(jnp.int32, (1, LANES), 1)
    src_lane = lax.broadcasted_iota(jnp.int32, (LANES, LANES), 0)
    dst_lane = lax.broadcasted_iota(jnp.int32, (LANES, LANES), 1)
    quarter = dh // 2
    swap = (src_lane == jnp.where((dst_lane % dh) < quarter, dst_lane + quarter, dst_lane - quarter)).astype(BF16)
    head0 = lane < dh
    tile = math.gcd(length, 256)
    if dil > 1:
        for j, ref in enumerate((q_ref, k_ref, v_ref)):
            for t0 in range(0, seq, 512):
                rows = pl.ds(t0, min(512, seq))
                wide[j, rows, :] = ref[0, rows, :].astype(F32)
    for r in range(dil):
        for t0 in range(0, length, tile):
            dst = pl.ds(r * length + t0, tile)
            if dil == 1:
                q, k, v = (ref[0, dst, :].astype(F32) for ref in (q_ref, k_ref, v_ref))
            else:
                src = pl.ds(r + t0 * dil, tile, stride=dil)
                q, k, v = wide[0, src, :], wide[1, src, :], wide[2, src, :]
            cos, sin = cos_ref[0, dst, :], sin_ref[0, dst, :]
            qs[dst, :] = (_rope_pair(q, cos, sin, swap) * (dh ** -0.5)).astype(BF16)
            ks[dst, :] = _rope_pair(k, cos, sin, swap).astype(BF16)
            vs[dst, :] = v.astype(BF16)

    qb = min(DA_QBLOCK, length)
    span = min(qb + 2 * half, length)
    n_blocks = seq // qb
    seg = math.gcd(n_blocks, DA_SEGMENT)
    delta = lax.broadcasted_iota(jnp.int32, (qb, span), 1) - lax.broadcasted_iota(jnp.int32, (qb, span), 0)

    def place(bi):
        row0 = pl.multiple_of(bi * qb, qb)
        r = row0 // length
        m0 = row0 - r * length
        k0 = jnp.clip(m0 - half, 0, length - span)
        return pl.ds(row0, qb), pl.ds(pl.multiple_of(r * length + k0, 16), span), k0 - m0

    def segment(si, carry):
        def scores(j, c):
            rows, krows, off = place(si * seg + j)
            q, kk = qs[rows, :], ks[krows, :]
            bias = jnp.where((delta >= -half - off) & (delta <= half - off), 0.0, -jnp.inf)
            zero = jnp.zeros_like(q)
            s_scr[j, 0, 0:qb, 0:span] = _dot_nt(jnp.where(head0, q, zero), kk) + bias
            s_scr[j, 1, 0:qb, 0:span] = _dot_nt(jnp.where(head0, zero, q), kk) + bias
            return c

        def softmax(j, c):
            rows, _, _ = place(si * seg + j)
            ms = []
            for h in range(2):
                s = s_scr[j, h, 0:qb, 0:span]
                m = jnp.max(s, axis=-1, keepdims=True)
                p_scr[j, h, 0:qb, 0:span] = jnp.exp(s - m).astype(BF16)
                ms.append(jnp.broadcast_to(m, (qb, LANES)))
            mt[rows, :] = jnp.where(head0, ms[0], ms[1])
            return c

        def values(j, c):
            rows, krows, _ = place(si * seg + j)
            vv = vs[krows, :]
            one = jnp.ones_like(vv)
            r0 = jnp.dot(p_scr[j, 0, 0:qb, 0:span], jnp.where(head0, vv, one), preferred_element_type=F32)
            r1 = jnp.dot(p_scr[j, 1, 0:qb, 0:span], jnp.where(head0, one, vv), preferred_element_type=F32)
            ot[rows, :] = jnp.where(head0, r0, r1)
            lt[rows, :] = pltpu.roll(jnp.where(head0, r1, r0), dh, 1)
            return c

        lax.fori_loop(0, seg, scores, 0, unroll=DA_UNROLL)
        lax.fori_loop(0, seg, softmax, 0, unroll=DA_UNROLL)
        lax.fori_loop(0, seg, values, 0, unroll=DA_UNROLL)
        return carry

    lax.fori_loop(0, n_blocks // seg, segment, 0)


def _attn_body(q_ref, k_ref, v_ref, cos_ref, sin_ref, o_ref, qs, ks, vs, ot, mt, lt, acc, mrun, lrun, s_scr, p_scr, wide):
    g = pl.program_id(2)
    seq = q_ref.shape[1]
    for gi, (window, dil) in enumerate(DA_PATTERNS):
        half = window // (2 * dil)
        assert half % 16 == 0
        length = seq // dil
        tile = math.gcd(length, 256)

        @pl.when(g == gi)
        def _(dil=dil, half=half, gi=gi, length=length, tile=tile):
            _attn_group(q_ref, k_ref, v_ref, cos_ref, sin_ref, qs, ks, vs, ot, mt, lt, s_scr, p_scr, wide, dil, half)
            for r in range(dil):
                for t0 in range(0, length, tile):
                    src = pl.ds(r * length + t0, tile)
                    dst = pl.ds(t0, tile) if dil == 1 else pl.ds(r + t0 * dil, tile, stride=dil)
                    o_new, m_new, l_new = ot[src, :], mt[src, :], lt[src, :]
                    if gi == 0:
                        acc[dst, :], mrun[dst, :], lrun[dst, :] = o_new, m_new, l_new
                    else:
                        m_old = mrun[dst, :]
                        m_all = jnp.maximum(m_old, m_new)
                        w_old, w_new = jnp.exp(m_old - m_all), jnp.exp(m_new - m_all)
                        acc[dst, :] = acc[dst, :] * w_old + o_new * w_new
                        lrun[dst, :] = lrun[dst, :] * w_old + l_new * w_new
                        mrun[dst, :] = m_all

    @pl.when(g == len(DA_PATTERNS) - 1)
    def _():
        tile = math.gcd(seq, 256)

        def finish(i, carry):
            r = pl.ds(pl.multiple_of(i * tile, tile), tile)
            o_ref[0, r, :] = (acc[r, :] / lrun[r, :]).astype(o_ref.dtype)
            return carry

        lax.fori_loop(0, seq // tile, finish, 0)


def _rope_tables(seq):
    dh = LANES // 2
    halfd = dh // 2
    inv = ROPE_THETA ** (-jnp.arange(halfd, dtype=F32) / halfd)
    cos_t, sin_t = [], []
    for _, dil in DA_PATTERNS:
        pos = jnp.arange(seq, dtype=F32).reshape(seq // dil, dil).T.reshape(seq)
        ang = pos[:, None] * inv[None, :]
        cos, sin = jnp.cos(ang), jnp.sin(ang)
        cos_t.append(jnp.tile(cos, (1, 4)))
        sin_t.append(jnp.concatenate([-sin, sin, -sin, sin], axis=1))
    return jnp.stack(cos_t), jnp.stack(sin_t)


def dilated_attention(proj):
    bsz, seq, d9 = proj.shape
    n_groups = len(DA_PATTERNS)
    d = d9 // (3 * n_groups)
    assert d // DA_HEADS == LANES // 2
    pairs = d // LANES
    cos, sin = _rope_tables(seq)
    qb_max = max(min(DA_QBLOCK, seq // dl) for _, dl in DA_PATTERNS)
    span_max = max(min(min(DA_QBLOCK, seq // dl) + 2 * (w // (2 * dl)), seq // dl) for w, dl in DA_PATTERNS)

    def col(part):
        return pl.BlockSpec((1, seq, LANES), lambda b, hp, g: (b, 0, (g * 3 + part) * pairs + hp))

    table = pl.BlockSpec((1, seq, LANES), lambda b, hp, g: (g, 0, 0))
    return pl.pallas_call(
        _attn_body,
        grid=(bsz, pairs, n_groups),
        in_specs=[col(0), col(1), col(2), table, table],
        out_specs=pl.BlockSpec((1, seq, LANES), lambda b, hp, g: (b, 0, hp)),
        out_shape=jax.ShapeDtypeStruct((bsz, seq, d), BF16),
        scratch_shapes=[pltpu.VMEM((seq, LANES), BF16)] * 3 + [pltpu.VMEM((seq, LANES), F32)] * 6
        + [pltpu.VMEM((DA_SEGMENT, 2, qb_max, span_max), F32),
           pltpu.VMEM((DA_SEGMENT, 2, qb_max, span_max), BF16), pltpu.VMEM((3, seq, LANES), F32)],
        compiler_params=_cparams("parallel", "parallel", "arbitrary"),
        name="dilated_attention",
    )(proj, proj, proj, cos, sin)


def _s5_operators(a_re, a_im, log_dt, b_re, b_im, c_re, c_im):
    lc, i_dim, p_dim = S5_CHUNK, S5_GROUP, S5_STATE
    n_groups = a_re.shape[1]
    hp = dict(precision=HIGHEST)
    a = lax.complex(a_re.astype(F32), a_im.astype(F32))
    lam = a * jnp.exp(log_dt.astype(F32))[..., None]
    a_bar = jnp.exp(lam)
    bmat = lax.complex(b_re.astype(F32), b_im.astype(F32))
    cmat = lax.complex(c_re.astype(F32), c_im.astype(F32))
    b_bar = ((a_bar - 1.0) / a)[..., None] * bmat[None]
    tau = jnp.arange(lc + 1, dtype=F32)
    apow = jnp.exp(lam[:, :, None, :] * tau[None, None, :, None])
    kern = jnp.real(jnp.einsum('gip,dgtp,dgpj->dgtij', cmat, apow[:, :, :lc], b_bar, **hp))
    s_idx = jnp.arange(lc)[:, None]
    t_idx = jnp.arange(lc)[None, :]
    lag = t_idx - s_idx
    m_f = jnp.where((lag >= 0)[None, :, :, None, None], kern[0][:, jnp.clip(lag, 0)], 0.0)
    m_b = jnp.where((lag <= 0)[None, :, :, None, None], kern[1][:, jnp.clip(-lag, 0)], 0.0)
    m_op = (m_f + m_b).transpose(0, 1, 4, 2, 3).reshape(n_groups, lc * i_dim, lc * i_dim)
    rev = jnp.arange(lc - 1, -1, -1)
    p_f = apow[0][:, rev][:, :, :, None] * b_bar[0][:, None]
    p_b = apow[1][:, :lc][:, :, :, None] * b_bar[1][:, None]
    p_op = jnp.stack([jnp.real(p_f), jnp.imag(p_f), jnp.real(p_b), jnp.imag(p_b)], axis=0)
    p_op = p_op.transpose(1, 2, 4, 0, 3).reshape(n_groups, lc * i_dim, 4, p_dim)
    q_f = cmat[:, None] * apow[0][:, 1:lc + 1][:, :, None, :]
    q_b = cmat[:, None] * apow[1][:, lc - jnp.arange(lc)][:, :, None, :]
    q_op = jnp.stack([jnp.real(q_f), -jnp.imag(q_f), jnp.real(q_b), -jnp.imag(q_b)], axis=0)
    q_op = q_op.transpose(1, 0, 4, 2, 3).reshape(n_groups, 4, p_dim, lc * i_dim)
    n_pairs = n_groups // 2
    eye = jnp.eye(2, dtype=F32)
    w = lc * i_dim
    m_pair = m_op.reshape(n_pairs, 2, w, w)
    p_pair = jnp.einsum('narqp,ab->narqbp', p_op.reshape(n_pairs, 2, w, 4, p_dim), eye)
    p_pair = p_pair.reshape(n_pairs, 2 * w, 8 * p_dim)
    q_pair = jnp.einsum('nbqpc,ab->nqbpac', q_op.reshape(n_pairs, 2, 4, p_dim, w), eye)
    q_pair = q_pair.reshape(n_pairs, 8 * p_dim, 2 * w)
    lam_chunk = (lam * lc).reshape(2, n_groups // 8, 8 * p_dim)
    tiles = n_groups // 8
    return (m_pair.astype(BF16).reshape(tiles, 4, 2, w, w), p_pair.astype(BF16).reshape(tiles, 4, 2 * w, 8 * p_dim),
            q_pair.astype(BF16).reshape(tiles, 4, 8 * p_dim, 2 * w), lam_chunk)


def _cmul(ar, ai, xr, xi):
    return ar * xr - ai * xi, ar * xi + ai * xr


def _s5_body(h_ref, perm_ref, m_ref, p_ref, q_ref, pw_ref, d_ref, o_ref, u_scr, uc_scr, v_scr, s_scr, yc_scr):
    seq = h_ref.shape[1]
    lc = S5_CHUNK
    n_chunks = seq // lc
    n_seg = S5_SEGMENTS
    ns = n_chunks // n_seg
    n_pairs = m_ref.shape[1]
    pw = p_ref.shape[2]
    sw = pw // 4
    qw = n_pairs * sw

    for s in range(n_seg):
        for t in range(lc):
            u_scr[t, pl.ds(s, ns, stride=n_seg), :] = h_ref[0, pl.ds(lc * s * ns + t, ns, stride=lc), :]
    u_all = jnp.concatenate([u_scr[t].astype(BF16) for t in range(lc)], axis=1)
    uc_scr[...] = jnp.dot(u_all, perm_ref[...], preferred_element_type=F32).astype(BF16)
    for p in range(n_pairs):
        vp = jnp.dot(uc_scr[:, p * pw:(p + 1) * pw], p_ref[0, p], preferred_element_type=F32)
        for c in range(4):
            v_scr[:, c * qw + p * sw:c * qw + (p + 1) * sw] = vp[:, c * sw:(c + 1) * sw]

    a_f = (pw_ref[0, 0, 1:2, :], pw_ref[0, 1, 1:2, :])
    a_b = (pw_ref[0, 2, 1:2, :], pw_ref[0, 3, 1:2, :])
    zero = jnp.zeros((n_seg, qw), F32)

    def scan(k, carry):
        f_re, f_im, b_re, b_im = carry
        rf = pl.ds(pl.multiple_of(k * n_seg, n_seg), n_seg)
        s_scr[rf, 0:qw] = f_re
        s_scr[rf, qw:2 * qw] = f_im
        n_re, n_im = _cmul(*a_f, f_re, f_im)
        rb = pl.ds(pl.multiple_of((ns - 1 - k) * n_seg, n_seg), n_seg)
        s_scr[rb, 2 * qw:3 * qw] = b_re
        s_scr[rb, 3 * qw:4 * qw] = b_im
        m_re, m_im = _cmul(*a_b, b_re, b_im)
        return (n_re + v_scr[rf, 0:qw], n_im + v_scr[rf, qw:2 * qw],
                m_re + v_scr[rb, 2 * qw:3 * qw], m_im + v_scr[rb, 3 * qw:4 * qw])

    f_re, f_im, b_re, b_im = lax.fori_loop(0, ns, scan, (zero, zero, zero, zero))

    a_seg_f = (pw_ref[0, 0, ns:ns + 1, :], pw_ref[0, 1, ns:ns + 1, :])
    a_seg_b = (pw_ref[0, 2, ns:ns + 1, :], pw_ref[0, 3, ns:ns + 1, :])
    row0 = jnp.zeros((1, qw), F32)
    cf = [(row0, row0)]
    for s in range(1, n_seg):
        xr, xi = _cmul(*a_seg_f, *cf[-1])
        cf.append((xr + f_re[s - 1:s, :], xi + f_im[s - 1:s, :]))
    cb = [(row0, row0)]
    for s in range(n_seg - 2, -1, -1):
        xr, xi = _cmul(*a_seg_b, *cb[0])
        cb.insert(0, (xr + b_re[s + 1:s + 2, :], xi + b_im[s + 1:s + 2, :]))
    cf_re, cf_im = (jnp.concatenate([c[j] for c in cf], axis=0) for j in range(2))
    cb_re, cb_im = (jnp.concatenate([c[j] for c in cb], axis=0) for j in range(2))

    def correct(k, carry):
        rf = pl.ds(pl.multiple_of(k * n_seg, n_seg), n_seg)
        xr, xi = _cmul(pw_ref[0, 0, pl.ds(k, 1), :], pw_ref[0, 1, pl.ds(k, 1), :], cf_re, cf_im)
        s_scr[rf, 0:qw] += xr
        s_scr[rf, qw:2 * qw] += xi
        kb = ns - 1 - k
        yr, yi = _cmul(pw_ref[0, 2, pl.ds(kb, 1), :], pw_ref[0, 3, pl.ds(kb, 1), :], cb_re, cb_im)
        s_scr[rf, 2 * qw:3 * qw] += yr
        s_scr[rf, 3 * qw:4 * qw] += yi
        return carry

    lax.fori_loop(0, ns, correct, 0)

    half = pw // 2
    for p in range(n_pairs):
        uc = uc_scr[:, p * pw:(p + 1) * pw]
        st = jnp.concatenate([s_scr[:, c * qw + p * sw:c * qw + (p + 1) * sw] for c in range(4)], axis=1)
        y = jnp.dot(st.astype(BF16), q_ref[0, p], preferred_element_type=F32)
        y = y + jnp.concatenate([jnp.dot(uc[:, :half], m_ref[0, p, 0], preferred_element_type=F32),
                                 jnp.dot(uc[:, half:], m_ref[0, p, 1], preferred_element_type=F32)], axis=1)
        yc_scr[:, p * pw:(p + 1) * pw] = y.astype(BF16)
    y_all = _dot_nt(yc_scr[...], perm_ref[...])
    for t in range(lc):
        u_scr[t] = y_all[:, t * LANES:(t + 1) * LANES]
    for s in range(n_seg):
        for t in range(lc):
            rows = pl.ds(lc * s * ns + t, ns, stride=lc)
            y = u_scr[t, pl.ds(s, ns, stride=n_seg), :] + d_ref[...] * h_ref[0, rows, :]
            o_ref[0, rows, :] = jax.nn.gelu(y)


def s5_mix(h3, ops, d_skip):
    m_op, p_op, q_op, lam_chunk = ops
    bsz, seq, d = h3.shape
    lc = S5_CHUNK
    tiles = d // LANES
    n_chunks = seq // lc
    width = lc * LANES
    assert seq % (lc * S5_SEGMENTS) == 0 and m_op.shape[0] == tiles
    ns = n_chunks // S5_SEGMENTS
    powers = jnp.exp(lam_chunk[:, :, None, :] * jnp.arange(ns + 1, dtype=F32)[None, None, :, None])
    pw_tab = jnp.stack([jnp.real(powers[0]), jnp.imag(powers[0]), jnp.real(powers[1]), jnp.imag(powers[1])], axis=1)
    src = jnp.arange(width)
    t_idx, g_idx, i_idx = src // LANES, (src % LANES) // S5_GROUP, src % S5_GROUP
    perm = (jnp.arange(width)[None, :] == (g_idx * (lc * S5_GROUP) + t_idx * S5_GROUP + i_idx)[:, None]).astype(BF16)
    return pl.pallas_call(
        _s5_body,
        grid=(tiles, bsz),
        in_specs=[
            pl.BlockSpec((1, seq, LANES), lambda l, b: (b, 0, l)),
            pl.BlockSpec((width, width), lambda l, b: (0, 0)),
            pl.BlockSpec((1,) + m_op.shape[1:], lambda l, b: (l, 0, 0, 0, 0)),
            pl.BlockSpec((1,) + p_op.shape[1:], lambda l, b: (l, 0, 0, 0)),
            pl.BlockSpec((1,) + q_op.shape[1:], lambda l, b: (l, 0, 0, 0)),
            pl.BlockSpec((1,) + pw_tab.shape[1:], lambda l, b: (l, 0, 0, 0)),
            pl.BlockSpec((1, LANES), lambda l, b: (0, l)),
        ],
        out_specs=pl.BlockSpec((1, seq, LANES), lambda l, b: (b, 0, l)),
        out_shape=jax.ShapeDtypeStruct((bsz, seq, d), F32),
        scratch_shapes=[pltpu.VMEM((lc, n_chunks, LANES), F32), pltpu.VMEM((n_chunks, width), BF16),
                        pltpu.VMEM((n_chunks, width), F32), pltpu.VMEM((n_chunks, width), F32),
                        pltpu.VMEM((n_chunks, width), BF16)],
        compiler_params=_cparams("parallel", "parallel"),
        name="s5_mix",
    )(h3, perm, m_op, p_op, q_op, pw_tab, d_skip.astype(F32)[None, :])


def _threshold_body(a_ref, thr_ref, cgt_ref, *, capacity):
    def bits():
        return lax.bitcast_convert_type(a_ref[...], jnp.int32)

    def step(i, thr):
        cand = thr | jnp.left_shift(jnp.int32(1), 30 - i)
        cnt = jnp.sum((bits() >= cand).astype(jnp.int32), axis=1, keepdims=True)
        return jnp.where(cnt >= capacity, cand, thr)

    thr = lax.fori_loop(0, 31, step, jnp.zeros((a_ref.shape[0], 1), jnp.int32))
    cgt = jnp.sum((bits() > thr).astype(jnp.int32), axis=1, keepdims=True)
    thr_ref[...] = jnp.broadcast_to(lax.bitcast_convert_type(thr, F32), thr_ref.shape)
    cgt_ref[...] = jnp.broadcast_to(cgt, cgt_ref.shape)


def expert_thresholds(aff_t, capacity):
    rows, n = aff_t.shape
    return pl.pallas_call(
        functools.partial(_threshold_body, capacity=capacity),
        grid=(1,),
        in_specs=[pl.BlockSpec((rows, n), lambda i: (0, 0))],
        out_specs=[pl.BlockSpec((rows, LANES), lambda i: (0, 0))] * 2,
        out_shape=[jax.ShapeDtypeStruct((rows, LANES), F32), jax.ShapeDtypeStruct((rows, LANES), jnp.int32)],
        compiler_params=_cparams("arbitrary"),
        name="expert_thresholds",
    )(aff_t)


def _tile_counts_body(a_ref, thr_ref, cgt_ref, ceq_ref):
    t = pl.program_id(0)
    a, thr = a_ref[...], thr_ref[...]
    cgt_ref[pl.ds(t, 1), :] = jnp.sum((a > thr).astype(jnp.int32), axis=0, keepdims=True)
    ceq_ref[pl.ds(t, 1), :] = jnp.sum((a == thr).astype(jnp.int32), axis=0, keepdims=True)


def tile_counts(aff, thr_l, tm):
    n = aff.shape[0]
    n_tiles = n // tm
    return pl.pallas_call(
        _tile_counts_body,
        grid=(n_tiles,),
        in_specs=[pl.BlockSpec((tm, LANES), lambda t: (t, 0)), pl.BlockSpec((1, LANES), lambda t: (0, 0))],
        out_specs=[pl.BlockSpec((n_tiles, LANES), lambda t: (0, 0))] * 2,
        out_shape=[jax.ShapeDtypeStruct((n_tiles, LANES), jnp.int32)] * 2,
        compiler_params=_cparams("arbitrary"),
        name="tile_counts",
    )(aff, thr_l)


def _slots_body(a_ref, thr_ref, need_ref, eqb_ref, selb_ref, slot_ref, slott_ref, *, n_exp):
    t = pl.program_id(0)
    tm = a_ref.shape[0]
    a, thr = a_ref[...], thr_ref[...]
    lane = lax.broadcasted_iota(jnp.int32, (1, LANES), 1)
    row = lax.broadcasted_iota(jnp.int32, (tm, tm), 0)
    col = lax.broadcasted_iota(jnp.int32, (tm, tm), 1)
    before = (row > col).astype(BF16)
    eq = a == thr
    eq_rank = jnp.dot(before, eq.astype(BF16), preferred_element_type=F32) + eqb_ref[pl.ds(t, 1), :].astype(F32)
    sel = ((a > thr) | (eq & (eq_rank < need_ref[...].astype(F32)))) & (lane < n_exp)
    pos = jnp.dot(before, sel.astype(BF16), preferred_element_type=F32) + selb_ref[pl.ds(t, 1), :].astype(F32)
    slot = jnp.where(sel, pos, -1.0)
    slot_ref[...] = slot.astype(jnp.int32)
    slott_ref[...] = slot.T[:EXPERT_ROWS, :].astype(jnp.int32)


def token_slots(aff, thr_l, need_l, eq_base, sel_base, n_exp, tm):
    n = aff.shape[0]
    n_tiles = n // tm
    full = lambda r: pl.BlockSpec((r, LANES), lambda t: (0, 0))
    return pl.pallas_call(
        functools.partial(_slots_body, n_exp=n_exp),
        grid=(n_tiles,),
        in_specs=[pl.BlockSpec((tm, LANES), lambda t: (t, 0)), full(1), full(1), full(n_tiles), full(n_tiles)],
        out_specs=[pl.BlockSpec((tm, LANES), lambda t: (t, 0)), pl.BlockSpec((EXPERT_ROWS, tm), lambda t: (0, t))],
        out_shape=[jax.ShapeDtypeStruct((n, LANES), jnp.int32), jax.ShapeDtypeStruct((EXPERT_ROWS, n), jnp.int32)],
        compiler_params=_cparams("parallel"),
        name="token_slots",
    )(aff, thr_l, need_l, eq_base, sel_base)


def _window_start(s):
    return pl.multiple_of((s // ROW_ALIGN) * ROW_ALIGN, ROW_ALIGN)


def _dispatch_body(base_ref, cnt_ref, slott_ref, h_ref, xe_ref, stage, extra, carry, sems, xsem, *, n_exp, capacity):
    t = pl.program_id(0)
    n_tiles = pl.num_programs(0)
    par = t % 2
    wc = stage.shape[2]
    sub = lax.broadcasted_iota(jnp.int32, (wc, 1), 0)

    @pl.when(t == 0)
    def _():
        carry[...] = jnp.zeros_like(carry)

    starts = [_window_start(base_ref[t * n_exp + e]) for e in range(n_exp)]
    onehot = jnp.concatenate([(slott_ref[e:e + 1, :] - starts[e] == sub) for e in range(n_exp)], axis=0)
    rows = jnp.dot(onehot.astype(BF16), h_ref[...], preferred_element_type=F32)

    def window_copy(e, k):
        return pltpu.make_async_copy(stage.at[par, e], xe_ref.at[e, pl.ds(starts[e] + k * wc, wc)], sems.at[e])

    @pl.when(t > 0)
    def _():
        for e in range(n_exp):
            pltpu.make_async_copy(stage.at[1 - par, e], xe_ref.at[e, pl.ds(0, wc)], sems.at[e]).wait()

    for e in range(n_exp):
        s = base_ref[t * n_exp + e]
        end16 = _window_start(s + cnt_ref[t * n_exp + e])
        n_win = (end16 - starts[e]) // wc + 1
        stage[par, e] = rows[e * wc:(e + 1) * wc].astype(BF16)
        stage[par, e, 0:ROW_ALIGN, :] += carry[e]
        window_copy(e, 0).start()

        def more(k, c, e=e):
            oh = slott_ref[e:e + 1, :] - (starts[e] + k * wc) == sub
            extra[...] = jnp.dot(oh.astype(BF16), h_ref[...], preferred_element_type=F32).astype(BF16)
            cp = pltpu.make_async_copy(extra, xe_ref.at[e, pl.ds(starts[e] + k * wc, wc)], xsem)
            cp.start()
            cp.wait()
            return c

        lax.fori_loop(1, n_win, more, 0)
        off = pl.multiple_of(end16 - starts[e] - (n_win - 1) * wc, ROW_ALIGN)

        @pl.when(n_win == 1)
        def _(e=e, off=off):
            carry[e] = stage[par, e, pl.ds(off, ROW_ALIGN), :]

        @pl.when(n_win > 1)
        def _(e=e, off=off):
            carry[e] = extra[pl.ds(off, ROW_ALIGN), :]

    @pl.when(t == n_tiles - 1)
    def _():
        for e in range(n_exp):
            window_copy(e, 0).wait()
        extra[...] = jnp.zeros_like(extra)
        for e in range(n_exp):
            cp = pltpu.make_async_copy(extra, xe_ref.at[e, pl.ds(capacity, wc)], xsem)
            cp.start()
            cp.wait()


def dispatch_rows(h, slot_t, base, cnt, n_exp, capacity, tm):
    n, d = h.shape
    n_tiles = n // tm
    wc = DISPATCH_WINDOW
    grid_spec = pltpu.PrefetchScalarGridSpec(
        num_scalar_prefetch=2,
        grid=(n_tiles,),
        in_specs=[pl.BlockSpec((EXPERT_ROWS, tm), lambda t, b, c: (0, t)),
                  pl.BlockSpec((tm, d), lambda t, b, c: (t, 0))],
        out_specs=pl.BlockSpec(memory_space=pl.ANY),
        scratch_shapes=[pltpu.VMEM((2, n_exp, wc, d), BF16), pltpu.VMEM((wc, d), BF16),
                        pltpu.VMEM((n_exp, ROW_ALIGN, d), BF16),
                        pltpu.SemaphoreType.DMA((n_exp,)), pltpu.SemaphoreType.DMA(())],
    )
    return pl.pallas_call(
        functools.partial(_dispatch_body, n_exp=n_exp, capacity=capacity),
        grid_spec=grid_spec,
        out_shape=jax.ShapeDtypeStruct((n_exp, capacity + wc, d), BF16),
        compiler_params=_cparams("arbitrary"),
        name="dispatch_rows",
    )(base, cnt, slot_t, h)


def _expert_body(x_ref, wg_ref, wu_ref, wd_ref, o_ref, acc_ref):
    f = pl.program_id(2)
    x = x_ref[0]
    a = jnp.dot(x, wg_ref[0], preferred_element_type=F32)
    u = jnp.dot(x, wu_ref[0], preferred_element_type=F32)
    hid = (a * jax.nn.sigmoid(a) * u).astype(BF16)
    part = jnp.dot(hid, wd_ref[0], preferred_element_type=F32)

    @pl.when(f == 0)
    def _():
        acc_ref[...] = part

    @pl.when(f > 0)
    def _():
        acc_ref[...] += part

    @pl.when(f == pl.num_programs(2) - 1)
    def _():
        o_ref[0] = acc_ref[...].astype(o_ref.dtype)


def expert_ffn(xe, w_gate, w_up, w_down, capacity, tf=1024):
    e, _, d = xe.shape
    f = w_gate.shape[2]
    tm = math.gcd(capacity, ROW_TILE)
    tf = math.gcd(f, tf)
    return pl.pallas_call(
        _expert_body,
        grid=(e, capacity // tm, f // tf),
        in_specs=[
            pl.BlockSpec((1, tm, d), lambda ei, i, j: (ei, i, 0)),
            pl.BlockSpec((1, d, tf), lambda ei, i, j: (ei, 0, j)),
            pl.BlockSpec((1, d, tf), lambda ei, i, j: (ei, 0, j)),
            pl.BlockSpec((1, tf, d), lambda ei, i, j: (ei, j, 0)),
        ],
        out_specs=pl.BlockSpec((1, tm, d), lambda ei, i, j: (ei, i, 0)),
        out_shape=jax.ShapeDtypeStruct((e, capacity, d), BF16),
        scratch_shapes=[pltpu.VMEM((tm, d), F32)],
        compiler_params=_cparams("parallel", "parallel", "arbitrary"),
        name="expert_ffn",
    )(xe, w_gate, w_up, w_down)


def _combine_body(base_ref, cnt_ref, slot_ref, aff_ref, x_ref, gate_ref, fg_ref, ye_ref, o_ref,
                  win, extra, acc, sems, xsem, *, n_exp, capacity, final_norm):
    t = pl.program_id(0)
    n_tiles = pl.num_programs(0)
    par = t % 2
    wc = win.shape[2]
    lanes = lax.broadcasted_iota(jnp.int32, (1, wc), 1)

    def start_of(tt, e, k):
        lo = _window_start(base_ref[tt * n_exp + e]) + k * wc
        return lo, pl.multiple_of(jnp.minimum(lo, capacity - wc), ROW_ALIGN)

    def fetch(tt, slot_par, e):
        _, st = start_of(tt, e, 0)
        return pltpu.make_async_copy(ye_ref.at[e, pl.ds(st, wc)], win.at[slot_par, e], sems.at[slot_par, e])

    @pl.when(t == 0)
    def _():
        for e in range(n_exp):
            fetch(0, 0, e).start()

    @pl.when(t + 1 < n_tiles)
    def _():
        for e in range(n_exp):
            fetch(t + 1, 1 - par, e).start()

    def spread(e, lo, st):
        col = slot_ref[:, e:e + 1]
        return jnp.where((col - st == lanes) & (col >= lo), aff_ref[:, e:e + 1], 0.0).astype(BF16)

    src_e = lax.broadcasted_iota(jnp.int32, (LANES, n_exp * wc), 0)
    dst_e = lax.broadcasted_iota(jnp.int32, (LANES, n_exp * wc), 1) // wc
    expand = (src_e == dst_e).astype(BF16)
    s1 = slot_ref[...] + 1
    rep = lambda v: jnp.dot(v.astype(BF16), expand, preferred_element_type=F32)
    slot_rep = rep(s1 // 64) * 64.0 + rep(s1 % 64) - 1.0
    aff_rep = rep(aff_ref[...])
    lo_vec = jnp.concatenate([jnp.full((1, wc), start_of(t, e, 0)[0], jnp.int32) for e in range(n_exp)], axis=1)
    st_vec = jnp.concatenate([jnp.full((1, wc), start_of(t, e, 0)[1], jnp.int32) for e in range(n_exp)], axis=1)
    lane_in_win = lax.broadcasted_iota(jnp.int32, (1, n_exp * wc), 1) % wc
    hit = (slot_rep == (st_vec + lane_in_win).astype(F32)) & (slot_rep >= lo_vec.astype(F32))
    onehot = jnp.where(hit, aff_rep, 0.0).astype(BF16)
    for e in range(n_exp):
        fetch(t, par, e).wait()
    acc[...] = jnp.dot(onehot, win[par].reshape(n_exp * wc, win.shape[3]), preferred_element_type=F32)
    for e in range(n_exp):
        s = base_ref[t * n_exp + e]
        n_win = (s - _window_start(s) + cnt_ref[t * n_exp + e] + wc - 1) // wc

        def more(k, c, e=e):
            lo_k, st_k = start_of(t, e, k)
            cp = pltpu.make_async_copy(ye_ref.at[e, pl.ds(st_k, wc)], extra, xsem)
            cp.start()
            cp.wait()
            acc[...] += jnp.dot(spread(e, lo_k, st_k), extra[...], preferred_element_type=F32)
            return c

        lax.fori_loop(1, n_win, more, 0)

    x = x_ref[...] + gate_ref[0] * acc[...]
    if final_norm:
        x = x * lax.rsqrt(jnp.mean(x * x, axis=-1, keepdims=True) + EPS) * fg_ref[...]
    o_ref[...] = x


def combine_rows(slot, aff, x2, gate, final_g, ye, base, cnt, seq, n_exp, capacity, tm, final_norm):
    n, d = x2.shape
    n_tiles = n // tm
    wc = DISPATCH_WINDOW
    assert wc <= capacity <= 64 * 256 and capacity % ROW_ALIGN == 0 and seq % tm == 0
    grid_spec = pltpu.PrefetchScalarGridSpec(
        num_scalar_prefetch=2,
        grid=(n_tiles,),
        in_specs=[pl.BlockSpec((tm, LANES), lambda t, b, c: (t, 0)),
                  pl.BlockSpec((tm, LANES), lambda t, b, c: (t, 0)),
                  pl.BlockSpec((tm, d), lambda t, b, c: (t, 0)),
                  pl.BlockSpec((1, 1, d), lambda t, b, c: ((t * tm) // seq, 0, 0)),
                  pl.BlockSpec((1, d), lambda t, b, c: (0, 0)),
                  pl.BlockSpec(memory_space=pl.ANY)],
        out_specs=pl.BlockSpec((tm, d), lambda t, b, c: (t, 0)),
        scratch_shapes=[pltpu.VMEM((2, n_exp, wc, d), BF16), pltpu.VMEM((wc, d), BF16), pltpu.VMEM((tm, d), F32),
                        pltpu.SemaphoreType.DMA((2, n_exp)), pltpu.SemaphoreType.DMA(())],
    )
    return pl.pallas_call(
        functools.partial(_combine_body, n_exp=n_exp, capacity=capacity, final_norm=final_norm),
        grid_spec=grid_spec,
        out_shape=jax.ShapeDtypeStruct((n, d), F32),
        compiler_params=_cparams("arbitrary"),
        name="combine_rows",
    )(base, cnt, slot, aff, x2, gate, final_g, ye)


def expert_choice_moe(x2, seq, norm_g, sc, sh, gate2, final_g, w_router, w_gate, w_up, w_down, final_norm):
    n, d = x2.shape
    n_exp = w_router.shape[1]
    capacity = CAPACITY_FACTOR * n // n_exp
    tm = _row_tile(seq, ROUTE_TILE)
    h, aff, aff_t = norm_router(x2, seq, norm_g, sc, sh, w_router)
    thr, cgt = expert_thresholds(aff_t, capacity)
    pad = LANES - thr.shape[0]
    thr_l = jnp.pad(thr[:, 0], (0, pad))[None, :]
    need_l = jnp.pad(capacity - cgt[:, 0], (0, pad))[None, :]
    t_gt, t_eq = tile_counts(aff, thr_l, tm)
    eq_base = jnp.cumsum(t_eq, axis=0) - t_eq
    t_sel = t_gt + jnp.clip(need_l - eq_base, 0, t_eq)
    sel_base = jnp.cumsum(t_sel, axis=0) - t_sel
    slot, slot_t = token_slots(aff, thr_l, need_l, eq_base, sel_base, n_exp, tm)
    base = sel_base[:, :n_exp].reshape(-1)
    cnt = t_sel[:, :n_exp].reshape(-1)
    xe = dispatch_rows(h, slot_t, base, cnt, n_exp, capacity, tm)
    ye = expert_ffn(xe, w_gate, w_up, w_down, capacity)
    return combine_rows(slot, aff, x2, gate2, final_g, ye, base, cnt, seq, n_exp, capacity, tm, final_norm)


def _trunk(x, c, params):
    (norm_mix_g, norm_ffn_g, ada_w, ada_b, hg_w_in, hg_w_out, hg_norm_g, lb_table,
     da_w_qkv, da_w_out, s5_ops, s5_d, s5_w_glu, moe_w_router, moe_w_gate, moe_w_up, moe_w_down, final_g) = params
    bsz, seq, d = x.shape
    depth = norm_mix_g.shape[0]
    x2 = x.reshape(bsz * seq, d)
    cond = jax.nn.silu(c)
    for layer in range(depth):
        kind, slot = layer % N_MIXERS, layer // N_MIXERS
        mod = (jnp.dot(cond, ada_w[layer], precision=HIGHEST) + ada_b[layer])[:, None, :]
        sh1, sc1, g1, sh2, sc2, g2 = jnp.split(mod, 6, axis=-1)
        gmix = norm_mix_g[layer][None, :]
        if kind == 0:
            proj = nm_matmul(x2, seq, gmix, sc1, sh1, hg_w_in[slot])
            lb = lb_table[layer].reshape(HG_HEADS, 1, -1)
            m = hgrn_recurrence(proj.reshape(bsz, seq, -1), lb, jnp.log1p(-lb), 1.0 - lb,
                                hg_norm_g[slot][None, :].astype(F32))
            x2 = proj_residual(m.reshape(bsz * seq, d), hg_w_out[slot], x2, g1, seq)
        elif kind == 1:
            proj = nm_matmul(x2, seq, gmix, sc1, sh1, da_w_qkv[slot])
            m = dilated_attention(proj.reshape(bsz, seq, -1))
            x2 = proj_residual(m.reshape(bsz * seq, d), da_w_out[slot], x2, g1, seq)
        else:
            h = norm_mod(x2, seq, gmix, sc1, sh1)
            z = s5_mix(h.reshape(bsz, seq, d), s5_ops[slot], s5_d[slot])
            x2 = proj_residual(z.reshape(bsz * seq, d), s5_w_glu[slot], x2, g1, seq, glu=True)
        x2 = expert_choice_moe(x2, seq, norm_ffn_g[layer][None, :], sc2, sh2, g2, final_g[None, :].astype(F32),
                               moe_w_router[layer], moe_w_gate[layer], moe_w_up[layer], moe_w_down[layer],
                               final_norm=(layer == depth - 1))
    return x2.reshape(bsz, seq, d)


def kernel(x_prompt, x_sample, c_prompt, c_sample, norm_mix_g, norm_ffn_g, ada_w, ada_b, hg_w_in, hg_w_out, hg_norm_g, hg_lb_logits, da_w_qkv, da_w_out, s5_a_re, s5_a_im, s5_log_dt, s5_b_re, s5_b_im, s5_c_re, s5_c_im, s5_d, s5_w_glu, moe_w_router, moe_w_gate, moe_w_up, moe_w_down, final_g):
    lb_table = jnp.cumsum(jax.nn.softmax(hg_lb_logits.astype(F32), axis=0), axis=0)
    lb_table = lb_table - lb_table[0:1]
    s5_ops = [_s5_operators(s5_a_re[s], s5_a_im[s], s5_log_dt[s], s5_b_re[s], s5_b_im[s],
                            s5_c_re[s], s5_c_im[s]) for s in range(s5_a_re.shape[0])]
    bf = lambda w: w.astype(BF16)
    params = (norm_mix_g.astype(F32), norm_ffn_g.astype(F32), ada_w, ada_b, bf(hg_w_in), bf(hg_w_out),
              hg_norm_g, lb_table, bf(da_w_qkv), bf(da_w_out), s5_ops, s5_d, bf(s5_w_glu), moe_w_router,
              bf(moe_w_gate), bf(moe_w_up), bf(moe_w_down), final_g)
    return (_trunk(x_prompt, c_prompt, params), _trunk(x_sample, c_sample, params))
```

```python
import functools
import math

import jax
import jax.numpy as jnp
from jax import lax
from jax.experimental import pallas as pl
from jax.experimental.pallas import tpu as pltpu

F32 = jnp.float32
BF16 = jnp.bfloat16
HIGHEST = lax.Precision.HIGHEST

EPS = 1e-6
N_MIXERS = 3
HG_HEADS = 8
HG_CHUNK = 64
HG_SUPER = 4
HG_FINISH = 8
DA_PATTERNS = ((128, 1), (512, 4), (2048, 16))
DA_HEADS = 16
DA_QBLOCK = 128
DA_SEGMENT = 8
DA_UNROLL = 8
ROPE_THETA = 10000.0
S5_GROUP = 16
S5_STATE = 64
S5_CHUNK = 16
S5_SEGMENTS = 8
CAPACITY_FACTOR = 2
EXPERT_ROWS = 16
ROUTE_TILE = 512
DISPATCH_WINDOW = 96
ROW_ALIGN = 16

LANES = 128
VMEM_LIMIT = 56 * 1024 * 1024
ROW_TILE = 1024


def _cparams(*sem):
    return pltpu.CompilerParams(dimension_semantics=sem, vmem_limit_bytes=VMEM_LIMIT)


def _row_tile(t, cap=ROW_TILE):
    return math.gcd(t, cap)


def _norm_mod(x, g, sc, sh):
    ms = jnp.mean(x * x, axis=-1, keepdims=True)
    return (x * lax.rsqrt(ms + EPS) * g) * (1.0 + sc) + sh


def _split2(x):
    hi = x.astype(BF16)
    return hi, (x - hi.astype(F32)).astype(BF16)


def _nm_matmul_body(x_ref, g_ref, sc_ref, sh_ref, w_ref, o_ref, h_scr):
    @pl.when(pl.program_id(1) == 0)
    def _():
        h_scr[...] = _norm_mod(x_ref[...], g_ref[...], sc_ref[0], sh_ref[0]).astype(BF16)

    o_ref[...] = jnp.dot(h_scr[...], w_ref[...], preferred_element_type=F32).astype(o_ref.dtype)


def nm_matmul(x2, seq, g, sc, sh, w, tn=1024):
    n, d = x2.shape
    f = w.shape[1]
    tm = _row_tile(seq)
    tn = math.gcd(f, tn)
    return pl.pallas_call(
        _nm_matmul_body,
        grid=(n // tm, f // tn),
        in_specs=[
            pl.BlockSpec((tm, d), lambda i, j: (i, 0)),
            pl.BlockSpec((1, d), lambda i, j: (0, 0)),
            pl.BlockSpec((1, 1, d), lambda i, j: ((i * tm) // seq, 0, 0)),
            pl.BlockSpec((1, 1, d), lambda i, j: ((i * tm) // seq, 0, 0)),
            pl.BlockSpec((d, tn), lambda i, j: (0, j)),
        ],
        out_specs=pl.BlockSpec((tm, tn), lambda i, j: (i, j)),
        out_shape=jax.ShapeDtypeStruct((n, f), BF16),
        scratch_shapes=[pltpu.VMEM((tm, d), BF16)],
        compiler_params=_cparams("parallel", "arbitrary"),
        name="nm_matmul",
    )(x2, g, sc, sh, w)


def _norm_mod_body(x_ref, g_ref, sc_ref, sh_ref, o_ref):
    o_ref[...] = _norm_mod(x_ref[...], g_ref[...], sc_ref[0], sh_ref[0])


def norm_mod(x2, seq, g, sc, sh):
    n, d = x2.shape
    tm = _row_tile(seq)
    return pl.pallas_call(
        _norm_mod_body,
        grid=(n // tm,),
        in_specs=[
            pl.BlockSpec((tm, d), lambda i: (i, 0)),
            pl.BlockSpec((1, d), lambda i: (0, 0)),
            pl.BlockSpec((1, 1, d), lambda i: ((i * tm) // seq, 0, 0)),
            pl.BlockSpec((1, 1, d), lambda i: ((i * tm) // seq, 0, 0)),
        ],
        out_specs=pl.BlockSpec((tm, d), lambda i: (i, 0)),
        out_shape=jax.ShapeDtypeStruct((n, d), F32),
        compiler_params=_cparams("parallel"),
        name="norm_mod",
    )(x2, g, sc, sh)


def _norm_router_body(x_ref, g_ref, sc_ref, sh_ref, wh_ref, wl_ref, h_ref, aff_ref, afft_ref, *, n_exp):
    h = _norm_mod(x_ref[...], g_ref[...], sc_ref[0], sh_ref[0])
    h_hi, h_lo = _split2(h)
    h_ref[...] = h_hi
    logits = (jnp.dot(h_hi, wh_ref[...], preferred_element_type=F32)
              + jnp.dot(h_lo, wh_ref[...], preferred_element_type=F32)
              + jnp.dot(h_hi, wl_ref[...], preferred_element_type=F32))
    lane = lax.broadcasted_iota(jnp.int32, (1, LANES), 1)
    logits = jnp.where(lane < n_exp, logits, -jnp.inf)
    ex = jnp.exp(logits - jnp.max(logits, axis=-1, keepdims=True))
    aff = ex / jnp.sum(ex, axis=-1, keepdims=True)
    aff_ref[...] = aff
    afft_ref[...] = aff.T[:EXPERT_ROWS, :]


def norm_router(x2, seq, g, sc, sh, w_router):
    n, d = x2.shape
    n_exp = w_router.shape[1]
    assert n_exp <= EXPERT_ROWS
    wr = jnp.zeros((d, LANES), F32).at[:, :n_exp].set(w_router.astype(F32))
    w_hi = wr.astype(BF16)
    w_lo = (wr - w_hi.astype(F32)).astype(BF16)
    tm = _row_tile(seq, ROUTE_TILE)
    return pl.pallas_call(
        functools.partial(_norm_router_body, n_exp=n_exp),
        grid=(n // tm,),
        in_specs=[
            pl.BlockSpec((tm, d), lambda i: (i, 0)),
            pl.BlockSpec((1, d), lambda i: (0, 0)),
            pl.BlockSpec((1, 1, d), lambda i: ((i * tm) // seq, 0, 0)),
            pl.BlockSpec((1, 1, d), lambda i: ((i * tm) // seq, 0, 0)),
            pl.BlockSpec((d, LANES), lambda i: (0, 0)),
            pl.BlockSpec((d, LANES), lambda i: (0, 0)),
        ],
        out_specs=[pl.BlockSpec((tm, d), lambda i: (i, 0)),
                   pl.BlockSpec((tm, LANES), lambda i: (i, 0)),
                   pl.BlockSpec((EXPERT_ROWS, tm), lambda i: (0, i))],
        out_shape=[jax.ShapeDtypeStruct((n, d), BF16), jax.ShapeDtypeStruct((n, LANES), F32),
                   jax.ShapeDtypeStruct((EXPERT_ROWS, n), F32)],
        compiler_params=_cparams("parallel"),
        name="norm_router",
    )(x2, g, sc, sh, w_hi, w_lo)


def _proj_res_body(m_ref, w_ref, x_ref, gate_ref, o_ref):
    y = jnp.dot(m_ref[...], w_ref[...], preferred_element_type=F32)
    o_ref[...] = x_ref[...] + gate_ref[0] * y


def _glu_res_body(m_ref, w_ref, x_ref, gate_ref, o_ref):
    d = x_ref.shape[-1]
    y = jnp.dot(m_ref[...].astype(BF16), w_ref[...], preferred_element_type=F32)
    o_ref[...] = x_ref[...] + gate_ref[0] * (y[:, :d] * jax.nn.sigmoid(y[:, d:]))


def proj_residual(m2, w, x2, gate, seq, glu=False):
    n, d = x2.shape
    f = w.shape[1]
    tm = _row_tile(seq, 512)
    return pl.pallas_call(
        _glu_res_body if glu else _proj_res_body,
        grid=(n // tm,),
        in_specs=[
            pl.BlockSpec((tm, d), lambda i: (i, 0)),
            pl.BlockSpec((d, f), lambda i: (0, 0)),
            pl.BlockSpec((tm, d), lambda i: (i, 0)),
            pl.BlockSpec((1, 1, d), lambda i: ((i * tm) // seq, 0, 0)),
        ],
        out_specs=pl.BlockSpec((tm, d), lambda i: (i, 0)),
        out_shape=jax.ShapeDtypeStruct((n, d), F32),
        compiler_params=_cparams("parallel"),
        name="glu_residual" if glu else "proj_residual",
    )(m2, w, x2, gate)


def _dot_nt(a, b):
    return lax.dot_general(a, b, (((1,), (1,)), ((), ())), preferred_element_type=F32)


def _hgrn_gates(z, lb, l1, om):
    e = jnp.exp(-jnp.abs(z))
    r = 1.0 / (1.0 + e)
    pos = z >= 0.0
    k = om * (jnp.where(pos, e, 1.0) * r)
    f = lb + om * (jnp.where(pos, 1.0, e) * r)
    log_sig = jnp.minimum(z, 0.0) + jnp.log(r)
    return jnp.maximum(jnp.log(f), l1 + log_sig), k


def _bcast_rows(x, rows, c, n_sub):
    return jnp.concatenate(
        [jnp.broadcast_to(x[rows[j]:rows[j] + 1, :], (c, x.shape[1])) for j in range(n_sub)], axis=0)


def _hgrn_intra(q, v, z, lb, l1, om, cum_mat, causal, ref_row, last_row, c, n_sub):
    logf, k = _hgrn_gates(z, lb, l1, om)
    l_hi, l_lo = _split2(logf)
    b = (jnp.dot(cum_mat, l_hi, preferred_element_type=F32)
         + jnp.dot(cum_mat, l_lo, preferred_element_type=F32))
    b_ref = _bcast_rows(b, [j * c + ref_row for j in range(n_sub)], c, n_sub)
    b_last = _bcast_rows(b, [j * c + last_row for j in range(n_sub)], c, n_sub)
    up, down = jnp.exp(b - b_ref), jnp.exp(b_ref - b)
    qd = (q * up).astype(BF16)
    kd = (k * down).astype(BF16)
    s = jnp.where(causal, _dot_nt(qd, kd), 0.0).astype(BF16)
    o = jnp.dot(s, v.astype(BF16), preferred_element_type=F32)
    qe = (q * (up * jnp.exp(b_ref))).astype(BF16)
    kl = (k * (down * jnp.exp(b_last - b_ref))).astype(BF16)
    kvs, decs = [], []
    for j in range(n_sub):
        r = slice(j * c, (j + 1) * c)
        kvs.append(jnp.dot(v[r].T.astype(BF16), kl[r], preferred_element_type=F32))
        decs.append(jnp.exp(b[j * c + last_row:j * c + last_row + 1, :]))
    return o, qe, kvs, decs


def _hgrn_body(q_ref, v_ref, zf_ref, zb_ref, g_ref, lb_ref, l1_ref, om_ref, ng_ref, o_ref,
               oi_scr, qe_scr, kv_scr, dec_scr, st_scr):
    seq = q_ref.shape[1]
    c = min(HG_CHUNK, seq)
    n_chunks = seq // c
    n_sub = math.gcd(n_chunks, HG_SUPER)
    sc = n_sub * c
    lb, l1, om = lb_ref[0], l1_ref[0], om_ref[0]
    row = lax.broadcasted_iota(jnp.int32, (sc, sc), 0)
    col = lax.broadcasted_iota(jnp.int32, (sc, sc), 1)
    same = (row // c) == (col // c)
    lower, upper = same & (row >= col), same & (row <= col)
    tril, triu = lower.astype(BF16), upper.astype(BF16)
    mid = c // 2

    def intra(i, carry):
        r = pl.ds(pl.multiple_of(i * sc, sc), sc)
        q, v = q_ref[0, r, :].astype(F32), v_ref[0, r, :].astype(F32)
        of, qf, kvf, decf = _hgrn_intra(q, v, zf_ref[0, r, :].astype(F32), lb, l1, om, tril, lower, mid - 1, c - 1, c, n_sub)
        ob, qb, kvb, decb = _hgrn_intra(q, v, zb_ref[0, r, :].astype(F32), lb, l1, om, triu, upper, c - mid, 0, c, n_sub)
        oi_scr[r, :] = of + ob
        qe_scr[r, 0:LANES] = qf
        qe_scr[r, LANES:2 * LANES] = qb
        for j in range(n_sub):
            n = i * n_sub + j
            kv_scr[0, n], kv_scr[1, n] = kvf[j], kvb[j]
            dec_scr[0, n], dec_scr[1, n] = jnp.broadcast_to(decf[j], (8, LANES)), jnp.broadcast_to(decb[j], (8, LANES))
        return carry

    lax.fori_loop(0, n_chunks // n_sub, intra, 0, unroll=2)

    def carry_state(n, carry):
        sf, sb = carry
        nb = n_chunks - 1 - n
        st_scr[n, :, 0:LANES] = sf.astype(BF16)
        st_scr[nb, :, LANES:2 * LANES] = sb.astype(BF16)
        sf = sf * dec_scr[0, n, 0:1, :] + kv_scr[0, n]
        sb = sb * dec_scr[1, nb, 0:1, :] + kv_scr[1, nb]
        return sf, sb

    zero = jnp.zeros((LANES, LANES), F32)
    lax.fori_loop(0, n_chunks, carry_state, (zero, zero))

    n_fin = math.gcd(n_chunks, HG_FINISH)
    fc = n_fin * c

    def finish(i, carry):
        parts = []
        for j in range(n_fin):
            n = i * n_fin + j
            parts.append(_dot_nt(qe_scr[pl.ds(pl.multiple_of(n * c, c), c), :], st_scr[n]))
        r = pl.ds(pl.multiple_of(i * fc, fc), fc)
        o = oi_scr[r, :] + jnp.concatenate(parts, axis=0)
        o = o * lax.rsqrt(jnp.mean(o * o, axis=-1, keepdims=True) + EPS) * ng_ref[...]
        o_ref[0, r, :] = (o * jax.nn.sigmoid(g_ref[0, r, :].astype(F32))).astype(o_ref.dtype)
        return carry

    lax.fori_loop(0, n_chunks // n_fin, finish, 0)


def hgrn_recurrence(proj, lb, l1, om, norm_g):
    bsz, seq, d5 = proj.shape
    d = d5 // 5
    h = HG_HEADS
    dk = d // h
    assert dk == LANES
    n_chunks = seq // min(HG_CHUNK, seq)

    def col(section):
        return pl.BlockSpec((1, seq, dk), lambda b, hh: (b, 0, section * h + hh))

    par = pl.BlockSpec((1, 1, dk), lambda b, hh: (hh, 0, 0))
    return pl.pallas_call(
        _hgrn_body,
        grid=(bsz, h),
        in_specs=[col(0), col(1), col(2), col(3), col(4), par, par, par,
                  pl.BlockSpec((1, dk), lambda b, hh: (0, 0))],
        out_specs=pl.BlockSpec((1, seq, dk), lambda b, hh: (b, 0, hh)),
        out_shape=jax.ShapeDtypeStruct((bsz, seq, d), BF16),
        scratch_shapes=[pltpu.VMEM((seq, dk), F32), pltpu.VMEM((seq, 2 * dk), BF16),
                        pltpu.VMEM((2, n_chunks, dk, dk), F32), pltpu.VMEM((2, n_chunks, 8, dk), F32),
                        pltpu.VMEM((n_chunks, dk, 2 * dk), BF16)],
        compiler_params=_cparams("parallel", "parallel"),
        name="hgrn_recurrence",
    )(proj, proj, proj, proj, proj, lb, l1, om, norm_g)


def _rope_pair(x, cos, sin_signed, swap):
    partner = jnp.dot(x.astype(BF16), swap, preferred_element_type=F32)
    return x * cos + partner * sin_signed


def _attn_group(q_ref, k_ref, v_ref, cos_ref, sin_ref, qs, ks, vs, ot, mt, lt, s_scr, p_scr, wide, dil, half):
    seq = q_ref.shape[1]
    length = seq // dil
    dh = LANES // 2
    lane = lax.broadcasted_iota(jnp.int32, (1, LANES), 1)
    src_lane = lax.broadcasted_iota(jnp.int32, (LANES, LANES), 0)
    dst_lane = lax.broadcasted_iota(jnp.int32, (LANES, LANES), 1)
    quarter = dh // 2
    swap = (src_lane == jnp.where((dst_lane % dh) < quarter, dst_lane + quarter, dst_lane - quarter)).astype(BF16)
    head0 = lane < dh
    tile = math.gcd(length, 256)
    if dil > 1:
        for j, ref in enumerate((q_ref, k_ref, v_ref)):
            for t0 in range(0, seq, 512):
                rows = pl.ds(t0, min(512, seq))
                wide[j, rows, :] = ref[0, rows, :].astype(F32)
    for r in range(dil):
        for t0 in range(0, length, tile):
            dst = pl.ds(r * length + t0, tile)
            if dil == 1:
                q, k, v = (ref[0, dst, :].astype(F32) for ref in (q_ref, k_ref, v_ref))
            else:
                src = pl.ds(r + t0 * dil, tile, stride=dil)
                q, k, v = wide[0, src, :], wide[1, src, :], wide[2, src, :]
            cos, sin = cos_ref[0, dst, :], sin_ref[0, dst, :]
            qs[dst, :] = (_rope_pair(q, cos, sin, swap) * (dh ** -0.5)).astype(BF16)
            ks[dst, :] = _rope_pair(k, cos, sin, swap).astype(BF16)
            vs[dst, :] = v.astype(BF16)

    qb = min(DA_QBLOCK, length)
    span = min(qb + 2 * half, length)
    n_blocks = seq // qb
    seg = math.gcd(n_blocks, DA_SEGMENT)
    delta = lax.broadcasted_iota(jnp.int32, (qb, span), 1) - lax.broadcasted_iota(jnp.int32, (qb, span), 0)

    def place(bi):
        row0 = pl.multiple_of(bi * qb, qb)
        r = row0 // length
        m0 = row0 - r * length
        k0 = jnp.clip(m0 - half, 0, length - span)
        return pl.ds(row0, qb), pl.ds(pl.multiple_of(r * length + k0, 16), span), k0 - m0

    def segment(si, carry):
        def scores(j, c):
            rows, krows, off = place(si * seg + j)
            q, kk = qs[rows, :], ks[krows, :]
            bias = jnp.where((delta >= -half - off) & (delta <= half - off), 0.0, -jnp.inf)
            zero = jnp.zeros_like(q)
            s_scr[j, 0, 0:qb, 0:span] = _dot_nt(jnp.where(head0, q, zero), kk) + bias
            s_scr[j, 1, 0:qb, 0:span] = _dot_nt(jnp.where(head0, zero, q), kk) + bias
            return c

        def softmax(j, c):
            rows, _, _ = place(si * seg + j)
            ms = []
            for h in range(2):
                s = s_scr[j, h, 0:qb, 0:span]
                m = jnp.max(s, axis=-1, keepdims=True)
                p_scr[j, h, 0:qb, 0:span] = jnp.exp(s - m).astype(BF16)
                ms.append(jnp.broadcast_to(m, (qb, LANES)))
            mt[rows, :] = jnp.where(head0, ms[0], ms[1])
            return c

        def values(j, c):
            rows, krows, _ = place(si * seg + j)
            vv = vs[krows, :]
            one = jnp.ones_like(vv)
            r0 = jnp.dot(p_scr[j, 0, 0:qb, 0:span], jnp.where(head0, vv, one), preferred_element_type=F32)
            r1 = jnp.dot(p_scr[j, 1, 0:qb, 0:span], jnp.where(head0, one, vv), preferred_element_type=F32)
            ot[rows, :] = jnp.where(head0, r0, r1)
            lt[rows, :] = pltpu.roll(jnp.where(head0, r1, r0), dh, 1)
            return c

        lax.fori_loop(0, seg, scores, 0, unroll=DA_UNROLL)
        lax.fori_loop(0, seg, softmax, 0, unroll=DA_UNROLL)
        lax.fori_loop(0, seg, values, 0, unroll=DA_UNROLL)
        return carry

    lax.fori_loop(0, n_blocks // seg, segment, 0)


def _attn_body(q_ref, k_ref, v_ref, cos_ref, sin_ref, o_ref, qs, ks, vs, ot, mt, lt, acc, mrun, lrun, s_scr, p_scr, wide):
    g = pl.program_id(2)
    seq = q_ref.shape[1]
    for gi, (window, dil) in enumerate(DA_PATTERNS):
        half = window // (2 * dil)
        assert half % 16 == 0
        length = seq // dil
        tile = math.gcd(length, 256)

        @pl.when(g == gi)
        def _(dil=dil, half=half, gi=gi, length=length, tile=tile):
            _attn_group(q_ref, k_ref, v_ref, cos_ref, sin_ref, qs, ks, vs, ot, mt, lt, s_scr, p_scr, wide, dil, half)
            for r in range(dil):
                for t0 in range(0, length, tile):
                    src = pl.ds(r * length + t0, tile)
                    dst = pl.ds(t0, tile) if dil == 1 else pl.ds(r + t0 * dil, tile, stride=dil)
                    o_new, m_new, l_new = ot[src, :], mt[src, :], lt[src, :]
                    if gi == 0:
                        acc[dst, :], mrun[dst, :], lrun[dst, :] = o_new, m_new, l_new
                    else:
                        m_old = mrun[dst, :]
                        m_all = jnp.maximum(m_old, m_new)
                        w_old, w_new = jnp.exp(m_old - m_all), jnp.exp(m_new - m_all)
                        acc[dst, :] = acc[dst, :] * w_old + o_new * w_new
                        lrun[dst, :] = lrun[dst, :] * w_old + l_new * w_new
                        mrun[dst, :] = m_all

    @pl.when(g == len(DA_PATTERNS) - 1)
    def _():
        tile = math.gcd(seq, 256)

        def finish(i, carry):
            r = pl.ds(pl.multiple_of(i * tile, tile), tile)
            o_ref[0, r, :] = (acc[r, :] / lrun[r, :]).astype(o_ref.dtype)
            return carry

        lax.fori_loop(0, seq // tile, finish, 0)


def _rope_tables(seq):
    dh = LANES // 2
    halfd = dh // 2
    inv = ROPE_THETA ** (-jnp.arange(halfd, dtype=F32) / halfd)
    cos_t, sin_t = [], []
    for _, dil in DA_PATTERNS:
        pos = jnp.arange(seq, dtype=F32).reshape(seq // dil, dil).T.reshape(seq)
        ang = pos[:, None] * inv[None, :]
        cos, sin = jnp.cos(ang), jnp.sin(ang)
        cos_t.append(jnp.tile(cos, (1, 4)))
        sin_t.append(jnp.concatenate([-sin, sin, -sin, sin], axis=1))
    return jnp.stack(cos_t), jnp.stack(sin_t)


def dilated_attention(proj):
    bsz, seq, d9 = proj.shape
    n_groups = len(DA_PATTERNS)
    d = d9 // (3 * n_groups)
    assert d // DA_HEADS == LANES // 2
    pairs = d // LANES
    cos, sin = _rope_tables(seq)
    qb_max = max(min(DA_QBLOCK, seq // dl) for _, dl in DA_PATTERNS)
    span_max = max(min(min(DA_QBLOCK, seq // dl) + 2 * (w // (2 * dl)), seq // dl) for w, dl in DA_PATTERNS)

    def col(part):
        return pl.BlockSpec((1, seq, LANES), lambda b, hp, g: (b, 0, (g * 3 + part) * pairs + hp))

    table = pl.BlockSpec((1, seq, LANES), lambda b, hp, g: (g, 0, 0))
    return pl.pallas_call(
        _attn_body,
        grid=(bsz, pairs, n_groups),
        in_specs=[col(0), col(1), col(2), table, table],
        out_specs=pl.BlockSpec((1, seq, LANES), lambda b, hp, g: (b, 0, hp)),
        out_shape=jax.ShapeDtypeStruct((bsz, seq, d), BF16),
        scratch_shapes=[pltpu.VMEM((seq, LANES), BF16)] * 3 + [pltpu.VMEM((seq, LANES), F32)] * 6
        + [pltpu.VMEM((DA_SEGMENT, 2, qb_max, span_max), F32),
           pltpu.VMEM((DA_SEGMENT, 2, qb_max, span_max), BF16), pltpu.VMEM((3, seq, LANES), F32)],
        compiler_params=_cparams("parallel", "parallel", "arbitrary"),
        name="dilated_attention",
    )(proj, proj, proj, cos, sin)


def _s5_operators(a_re, a_im, log_dt, b_re, b_im, c_re, c_im):
    lc, i_dim, p_dim = S5_CHUNK, S5_GROUP, S5_STATE
    n_groups = a_re.shape[1]
    hp = dict(precision=HIGHEST)
    a = lax.complex(a_re.astype(F32), a_im.astype(F32))
    lam = a * jnp.exp(log_dt.astype(F32))[..., None]
    a_bar = jnp.exp(lam)
    bmat = lax.complex(b_re.astype(F32), b_im.astype(F32))
    cmat = lax.complex(c_re.astype(F32), c_im.astype(F32))
    b_bar = ((a_bar - 1.0) / a)[..., None] * bmat[None]
    tau = jnp.arange(lc + 1, dtype=F32)
    apow = jnp.exp(lam[:, :, None, :] * tau[None, None, :, None])
    kern = jnp.real(jnp.einsum('gip,dgtp,dgpj->dgtij', cmat, apow[:, :, :lc], b_bar, **hp))
    s_idx = jnp.arange(lc)[:, None]
    t_idx = jnp.arange(lc)[None, :]
    lag = t_idx - s_idx
    m_f = jnp.where((lag >= 0)[None, :, :, None, None], kern[0][:, jnp.clip(lag, 0)], 0.0)
    m_b = jnp.where((lag <= 0)[None, :, :, None, None], kern[1][:, jnp.clip(-lag, 0)], 0.0)
    m_op = (m_f + m_b).transpose(0, 1, 4, 2, 3).reshape(n_groups, lc * i_dim, lc * i_dim)
    rev = jnp.arange(lc - 1, -1, -1)
    p_f = apow[0][:, rev][:, :, :, None] * b_bar[0][:, None]
    p_b = apow[1][:, :lc][:, :, :, None] * b_bar[1][:, None]
    p_op = jnp.stack([jnp.real(p_f), jnp.imag(p_f), jnp.real(p_b), jnp.imag(p_b)], axis=0)
    p_op = p_op.transpose(1, 2, 4, 0, 3).reshape(n_groups, lc * i_dim, 4, p_dim)
    q_f = cmat[:, None] * apow[0][:, 1:lc + 1][:, :, None, :]
    q_b = cmat[:, None] * apow[1][:, lc - jnp.arange(lc)][:, :, None, :]
    q_op = jnp.stack([jnp.real(q_f), -jnp.imag(q_f), jnp.real(q_b), -jnp.imag(q_b)], axis=0)
    q_op = q_op.transpose(1, 0, 4, 2, 3).reshape(n_groups, 4, p_dim, lc * i_dim)
    n_pairs = n_groups // 2
    eye = jnp.eye(2, dtype=F32)
    w = lc * i_dim
    m_pair = m_op.reshape(n_pairs, 2, w, w)
    p_pair = jnp.einsum('narqp,ab->narqbp', p_op.reshape(n_pairs, 2, w, 4, p_dim), eye)
    p_pair = p_pair.reshape(n_pairs, 2 * w, 8 * p_dim)
    q_pair = jnp.einsum('nbqpc,ab->nqbpac', q_op.reshape(n_pairs, 2, 4, p_dim, w), eye)
    q_pair = q_pair.reshape(n_pairs, 8 * p_dim, 2 * w)
    lam_chunk = (lam * lc).reshape(2, n_groups // 8, 8 * p_dim)
    tiles = n_groups // 8
    return (m_pair.astype(BF16).reshape(tiles, 4, 2, w, w), p_pair.astype(BF16).reshape(tiles, 4, 2 * w, 8 * p_dim),
            q_pair.astype(BF16).reshape(tiles, 4, 8 * p_dim, 2 * w), lam_chunk)


def _cmul(ar, ai, xr, xi):
    return ar * xr - ai * xi, ar * xi + ai * xr


def _s5_body(h_ref, perm_ref, m_ref, p_ref, q_ref, pw_ref, d_ref, o_ref, u_scr, uc_scr, v_scr, s_scr, yc_scr):
    seq = h_ref.shape[1]
    lc = S5_CHUNK
    n_chunks = seq // lc
    n_seg = S5_SEGMENTS
    ns = n_chunks // n_seg
    n_pairs = m_ref.shape[1]
    pw = p_ref.shape[2]
    sw = pw // 4
    qw = n_pairs * sw

    for s in range(n_seg):
        for t in range(lc):
            u_scr[t, pl.ds(s, ns, stride=n_seg), :] = h_ref[0, pl.ds(lc * s * ns + t, ns, stride=lc), :]
    u_all = jnp.concatenate([u_scr[t].astype(BF16) for t in range(lc)], axis=1)
    uc_scr[...] = jnp.dot(u_all, perm_ref[...], preferred_element_type=F32).astype(BF16)
    for p in range(n_pairs):
        vp = jnp.dot(uc_scr[:, p * pw:(p + 1) * pw], p_ref[0, p], preferred_element_type=F32)
        for c in range(4):
            v_scr[:, c * qw + p * sw:c * qw + (p + 1) * sw] = vp[:, c * sw:(c + 1) * sw]

    a_f = (pw_ref[0, 0, 1:2, :], pw_ref[0, 1, 1:2, :])
    a_b = (pw_ref[0, 2, 1:2, :], pw_ref[0, 3, 1:2, :])
    zero = jnp.zeros((n_seg, qw), F32)

    def scan(k, carry):
        f_re, f_im, b_re, b_im = carry
        rf = pl.ds(pl.multiple_of(k * n_seg, n_seg), n_seg)
        s_scr[rf, 0:qw] = f_re
        s_scr[rf, qw:2 * qw] = f_im
        n_re, n_im = _cmul(*a_f, f_re, f_im)
        rb = pl.ds(pl.multiple_of((ns - 1 - k) * n_seg, n_seg), n_seg)
        s_scr[rb, 2 * qw:3 * qw] = b_re
        s_scr[rb, 3 * qw:4 * qw] = b_im
        m_re, m_im = _cmul(*a_b, b_re, b_im)
        return (n_re + v_scr[rf, 0:qw], n_im + v_scr[rf, qw:2 * qw],
                m_re + v_scr[rb, 2 * qw:3 * qw], m_im + v_scr[rb, 3 * qw:4 * qw])

    f_re, f_im, b_re, b_im = lax.fori_loop(0, ns, scan, (zero, zero, zero, zero))

    a_seg_f = (pw_ref[0, 0, ns:ns + 1, :], pw_ref[0, 1, ns:ns + 1, :])
    a_seg_b = (pw_ref[0, 2, ns:ns + 1, :], pw_ref[0, 3, ns:ns + 1, :])
    row0 = jnp.zeros((1, qw), F32)
    cf = [(row0, row0)]
    for s in range(1, n_seg):
        xr, xi = _cmul(*a_seg_f, *cf[-1])
        cf.append((xr + f_re[s - 1:s, :], xi + f_im[s - 1:s, :]))
    cb = [(row0, row0)]
    for s in range(n_seg - 2, -1, -1):
        xr, xi = _cmul(*a_seg_b, *cb[0])
        cb.insert(0, (xr + b_re[s + 1:s + 2, :], xi + b_im[s + 1:s + 2, :]))
    cf_re, cf_im = (jnp.concatenate([c[j] for c in cf], axis=0) for j in range(2))
    cb_re, cb_im = (jnp.concatenate([c[j] for c in cb], axis=0) for j in range(2))

    def correct(k, carry):
        rf = pl.ds(pl.multiple_of(k * n_seg, n_seg), n_seg)
        xr, xi = _cmul(pw_ref[0, 0, pl.ds(k, 1), :], pw_ref[0, 1, pl.ds(k, 1), :], cf_re, cf_im)
        s_scr[rf, 0:qw] += xr
        s_scr[rf, qw:2 * qw] += xi
        kb = ns - 1 - k
        yr, yi = _cmul(pw_ref[0, 2, pl.ds(kb, 1), :], pw_ref[0, 3, pl.ds(kb, 1), :], cb_re, cb_im)
        s_scr[rf, 2 * qw:3 * qw] += yr
        s_scr[rf, 3 * qw:4 * qw] += yi
        return carry

    lax.fori_loop(0, ns, correct, 0)

    half = pw // 2
    for p in range(n_pairs):
        uc = uc_scr[:, p * pw:(p + 1) * pw]
        st = jnp.concatenate([s_scr[:, c * qw + p * sw:c * qw + (p + 1) * sw] for c in range(4)], axis=1)
        y = jnp.dot(st.astype(BF16), q_ref[0, p], preferred_element_type=F32)
        y = y + jnp.concatenate([jnp.dot(uc[:, :half], m_ref[0, p, 0], preferred_element_type=F32),
                                 jnp.dot(uc[:, half:], m_ref[0, p, 1], preferred_element_type=F32)], axis=1)
        yc_scr[:, p * pw:(p + 1) * pw] = y.astype(BF16)
    y_all = _dot_nt(yc_scr[...], perm_ref[...])
    for t in range(lc):
        u_scr[t] = y_all[:, t * LANES:(t + 1) * LANES]
    for s in range(n_seg):
        for t in range(lc):
            rows = pl.ds(lc * s * ns + t, ns, stride=lc)
            y = u_scr[t, pl.ds(s, ns, stride=n_seg), :] + d_ref[...] * h_ref[0, rows, :]
            o_ref[0, rows, :] = jax.nn.gelu(y)


def s5_mix(h3, ops, d_skip):
    m_op, p_op, q_op, lam_chunk = ops
    bsz, seq, d = h3.shape
    lc = S5_CHUNK
    tiles = d // LANES
    n_chunks = seq // lc
    width = lc * LANES
    assert seq % (lc * S5_SEGMENTS) == 0 and m_op.shape[0] == tiles
    ns = n_chunks // S5_SEGMENTS
    powers = jnp.exp(lam_chunk[:, :, None, :] * jnp.arange(ns + 1, dtype=F32)[None, None, :, None])
    pw_tab = jnp.stack([jnp.real(powers[0]), jnp.imag(powers[0]), jnp.real(powers[1]), jnp.imag(powers[1])], axis=1)
    src = jnp.arange(width)
    t_idx, g_idx, i_idx = src // LANES, (src % LANES) // S5_GROUP, src % S5_GROUP
    perm = (jnp.arange(width)[None, :] == (g_idx * (lc * S5_GROUP) + t_idx * S5_GROUP + i_idx)[:, None]).astype(BF16)
    return pl.pallas_call(
        _s5_body,
        grid=(tiles, bsz),
        in_specs=[
            pl.BlockSpec((1, seq, LANES), lambda l, b: (b, 0, l)),
            pl.BlockSpec((width, width), lambda l, b: (0, 0)),
            pl.BlockSpec((1,) + m_op.shape[1:], lambda l, b: (l, 0, 0, 0, 0)),
            pl.BlockSpec((1,) + p_op.shape[1:], lambda l, b: (l, 0, 0, 0)),
            pl.BlockSpec((1,) + q_op.shape[1:], lambda l, b: (l, 0, 0, 0)),
            pl.BlockSpec((1,) + pw_tab.shape[1:], lambda l, b: (l, 0, 0, 0)),
            pl.BlockSpec((1, LANES), lambda l, b: (0, l)),
        ],
        out_specs=pl.BlockSpec((1, seq, LANES), lambda l, b: (b, 0, l)),
        out_shape=jax.ShapeDtypeStruct((bsz, seq, d), F32),
        scratch_shapes=[pltpu.VMEM((lc, n_chunks, LANES), F32), pltpu.VMEM((n_chunks, width), BF16),
                        pltpu.VMEM((n_chunks, width), F32), pltpu.VMEM((n_chunks, width), F32),
                        pltpu.VMEM((n_chunks, width), BF16)],
        compiler_params=_cparams("parallel", "parallel"),
        name="s5_mix",
    )(h3, perm, m_op, p_op, q_op, pw_tab, d_skip.astype(F32)[None, :])


def _threshold_body(a_ref, thr_ref, cgt_ref, *, capacity):
    def bits():
        return lax.bitcast_convert_type(a_ref[...], jnp.int32)

    def step(i, thr):
        cand = thr | jnp.left_shift(jnp.int32(1), 30 - i)
        cnt = jnp.sum((bits() >= cand).astype(jnp.int32), axis=1, keepdims=True)
        return jnp.where(cnt >= capacity, cand, thr)

    thr = lax.fori_loop(0, 31, step, jnp.zeros((a_ref.shape[0], 1), jnp.int32))
    cgt = jnp.sum((bits() > thr).astype(jnp.int32), axis=1, keepdims=True)
    thr_ref[...] = jnp.broadcast_to(lax.bitcast_convert_type(thr, F32), thr_ref.shape)
    cgt_ref[...] = jnp.broadcast_to(cgt, cgt_ref.shape)


def expert_thresholds(aff_t, capacity):
    rows, n = aff_t.shape
    return pl.pallas_call(
        functools.partial(_threshold_body, capacity=capacity),
        grid=(1,),
        in_specs=[pl.BlockSpec((rows, n), lambda i: (0, 0))],
        out_specs=[pl.BlockSpec((rows, LANES), lambda i: (0, 0))] * 2,
        out_shape=[jax.ShapeDtypeStruct((rows, LANES), F32), jax.ShapeDtypeStruct((rows, LANES), jnp.int32)],
        compiler_params=_cparams("arbitrary"),
        name="expert_thresholds",
    )(aff_t)


def _tile_counts_body(a_ref, thr_ref, cgt_ref, ceq_ref):
    t = pl.program_id(0)
    a, thr = a_ref[...], thr_ref[...]
    cgt_ref[pl.ds(t, 1), :] = jnp.sum((a > thr).astype(jnp.int32), axis=0, keepdims=True)
    ceq_ref[pl.ds(t, 1), :] = jnp.sum((a == thr).astype(jnp.int32), axis=0, keepdims=True)


def tile_counts(aff, thr_l, tm):
    n = aff.shape[0]
    n_tiles = n // tm
    return pl.pallas_call(
        _tile_counts_body,
        grid=(n_tiles,),
        in_specs=[pl.BlockSpec((tm, LANES), lambda t: (t, 0)), pl.BlockSpec((1, LANES), lambda t: (0, 0))],
        out_specs=[pl.BlockSpec((n_tiles, LANES), lambda t: (0, 0))] * 2,
        out_shape=[jax.ShapeDtypeStruct((n_tiles, LANES), jnp.int32)] * 2,
        compiler_params=_cparams("arbitrary"),
        name="tile_counts",
    )(aff, thr_l)


def _slots_body(a_ref, thr_ref, need_ref, eqb_ref, selb_ref, slot_ref, slott_ref, *, n_exp):
    t = pl.program_id(0)
    tm = a_ref.shape[0]
    a, thr = a_ref[...], thr_ref[...]
    lane = lax.broadcasted_iota(jnp.int32, (1, LANES), 1)
    row = lax.broadcasted_iota(jnp.int32, (tm, tm), 0)
    col = lax.broadcasted_iota(jnp.int32, (tm, tm), 1)
    before = (row > col).astype(BF16)
    eq = a == thr
    eq_rank = jnp.dot(before, eq.astype(BF16), preferred_element_type=F32) + eqb_ref[pl.ds(t, 1), :].astype(F32)
    sel = ((a > thr) | (eq & (eq_rank < need_ref[...].astype(F32)))) & (lane < n_exp)
    pos = jnp.dot(before, sel.astype(BF16), preferred_element_type=F32) + selb_ref[pl.ds(t, 1), :].astype(F32)
    slot = jnp.where(sel, pos, -1.0)
    slot_ref[...] = slot.astype(jnp.int32)
    slott_ref[...] = slot.T[:EXPERT_ROWS, :].astype(jnp.int32)


def token_slots(aff, thr_l, need_l, eq_base, sel_base, n_exp, tm):
    n = aff.shape[0]
    n_tiles = n // tm
    full = lambda r: pl.BlockSpec((r, LANES), lambda t: (0, 0))
    return pl.pallas_call(
        functools.partial(_slots_body, n_exp=n_exp),
        grid=(n_tiles,),
        in_specs=[pl.BlockSpec((tm, LANES), lambda t: (t, 0)), full(1), full(1), full(n_tiles), full(n_tiles)],
        out_specs=[pl.BlockSpec((tm, LANES), lambda t: (t, 0)), pl.BlockSpec((EXPERT_ROWS, tm), lambda t: (0, t))],
        out_shape=[jax.ShapeDtypeStruct((n, LANES), jnp.int32), jax.ShapeDtypeStruct((EXPERT_ROWS, n), jnp.int32)],
        compiler_params=_cparams("parallel"),
        name="token_slots",
    )(aff, thr_l, need_l, eq_base, sel_base)


def _window_start(s):
    return pl.multiple_of((s // ROW_ALIGN) * ROW_ALIGN, ROW_ALIGN)


def _dispatch_body(base_ref, cnt_ref, slott_ref, h_ref, xe_ref, stage, extra, carry, sems, xsem, *, n_exp, capacity):
    t = pl.program_id(0)
    n_tiles = pl.num_programs(0)
    par = t % 2
    wc = stage.shape[2]
    sub = lax.broadcasted_iota(jnp.int32, (wc, 1), 0)

    @pl.when(t == 0)
    def _():
        carry[...] = jnp.zeros_like(carry)

    starts = [_window_start(base_ref[t * n_exp + e]) for e in range(n_exp)]
    onehot = jnp.concatenate([(slott_ref[e:e + 1, :] - starts[e] == sub) for e in range(n_exp)], axis=0)
    rows = jnp.dot(onehot.astype(BF16), h_ref[...], preferred_element_type=F32)

    def window_copy(e, k):
        return pltpu.make_async_copy(stage.at[par, e], xe_ref.at[e, pl.ds(starts[e] + k * wc, wc)], sems.at[e])

    @pl.when(t > 0)
    def _():
        for e in range(n_exp):
            pltpu.make_async_copy(stage.at[1 - par, e], xe_ref.at[e, pl.ds(0, wc)], sems.at[e]).wait()

    for e in range(n_exp):
        s = base_ref[t * n_exp + e]
        end16 = _window_start(s + cnt_ref[t * n_exp + e])
        n_win = (end16 - starts[e]) // wc + 1
        stage[par, e] = rows[e * wc:(e + 1) * wc].astype(BF16)
        stage[par, e, 0:ROW_ALIGN, :] += carry[e]
        window_copy(e, 0).start()

        def more(k, c, e=e):
            oh = slott_ref[e:e + 1, :] - (starts[e] + k * wc) == sub
            extra[...] = jnp.dot(oh.astype(BF16), h_ref[...], preferred_element_type=F32).astype(BF16)
            cp = pltpu.make_async_copy(extra, xe_ref.at[e, pl.ds(starts[e] + k * wc, wc)], xsem)
            cp.start()
            cp.wait()
            return c

        lax.fori_loop(1, n_win, more, 0)
        off = pl.multiple_of(end16 - starts[e] - (n_win - 1) * wc, ROW_ALIGN)

        @pl.when(n_win == 1)
        def _(e=e, off=off):
            carry[e] = stage[par, e, pl.ds(off, ROW_ALIGN), :]

        @pl.when(n_win > 1)
        def _(e=e, off=off):
            carry[e] = extra[pl.ds(off, ROW_ALIGN), :]

    @pl.when(t == n_tiles - 1)
    def _():
        for e in range(n_exp):
            window_copy(e, 0).wait()
        extra[...] = jnp.zeros_like(extra)
        for e in range(n_exp):
            cp = pltpu.make_async_copy(extra, xe_ref.at[e, pl.ds(capacity, wc)], xsem)
            cp.start()
            cp.wait()


def dispatch_rows(h, slot_t, base, cnt, n_exp, capacity, tm):
    n, d = h.shape
    n_tiles = n // tm
    wc = DISPATCH_WINDOW
    grid_spec = pltpu.PrefetchScalarGridSpec(
        num_scalar_prefetch=2,
        grid=(n_tiles,),
        in_specs=[pl.BlockSpec((EXPERT_ROWS, tm), lambda t, b, c: (0, t)),
                  pl.BlockSpec((tm, d), lambda t, b, c: (t, 0))],
        out_specs=pl.BlockSpec(memory_space=pl.ANY),
        scratch_shapes=[pltpu.VMEM((2, n_exp, wc, d), BF16), pltpu.VMEM((wc, d), BF16),
                        pltpu.VMEM((n_exp, ROW_ALIGN, d), BF16),
                        pltpu.SemaphoreType.DMA((n_exp,)), pltpu.SemaphoreType.DMA(())],
    )
    return pl.pallas_call(
        functools.partial(_dispatch_body, n_exp=n_exp, capacity=capacity),
        grid_spec=grid_spec,
        out_shape=jax.ShapeDtypeStruct((n_exp, capacity + wc, d), BF16),
        compiler_params=_cparams("arbitrary"),
        name="dispatch_rows",
    )(base, cnt, slot_t, h)


def _expert_body(x_ref, wg_ref, wu_ref, wd_ref, o_ref, *, tf):
    x = x_ref[0]
    acc = None
    for f0 in range(0, wg_ref.shape[2], tf):
        a = jnp.dot(x, wg_ref[0, :, f0:f0 + tf], preferred_element_type=F32)
        u = jnp.dot(x, wu_ref[0, :, f0:f0 + tf], preferred_element_type=F32)
        hid = (a * jax.nn.sigmoid(a) * u).astype(BF16)
        part = jnp.dot(hid, wd_ref[0, f0:f0 + tf, :], preferred_element_type=F32)
        acc = part if acc is None else acc + part
    o_ref[0] = acc.astype(o_ref.dtype)


def expert_ffn(xe, w_gate, w_up, w_down, capacity, tf=1024):
    e, _, d = xe.shape
    f = w_gate.shape[2]
    tm = math.gcd(capacity, ROW_TILE)
    tf = math.gcd(f, tf)
    return pl.pallas_call(
        functools.partial(_expert_body, tf=tf),
        grid=(e, capacity // tm),
        in_specs=[
            pl.BlockSpec((1, tm, d), lambda ei, i: (ei, i, 0)),
            pl.BlockSpec((1, d, f), lambda ei, i: (ei, 0, 0)),
            pl.BlockSpec((1, d, f), lambda ei, i: (ei, 0, 0)),
            pl.BlockSpec((1, f, d), lambda ei, i: (ei, 0, 0)),
        ],
        out_specs=pl.BlockSpec((1, tm, d), lambda ei, i: (ei, i, 0)),
        out_shape=jax.ShapeDtypeStruct((e, capacity, d), BF16),
        compiler_params=_cparams("parallel", "parallel"),
        name="expert_ffn",
    )(xe, w_gate, w_up, w_down)


def _combine_body(base_ref, cnt_ref, slot_ref, aff_ref, x_ref, gate_ref, fg_ref, ye_ref, o_ref,
                  win, extra, acc, sems, xsem, *, n_exp, capacity, final_norm):
    t = pl.program_id(0)
    n_tiles = pl.num_programs(0)
    par = t % 2
    wc = win.shape[2]
    lanes = lax.broadcasted_iota(jnp.int32, (1, wc), 1)

    def start_of(tt, e, k):
        lo = _window_start(base_ref[tt * n_exp + e]) + k * wc
        return lo, pl.multiple_of(jnp.minimum(lo, capacity - wc), ROW_ALIGN)

    def fetch(tt, slot_par, e):
        _, st = start_of(tt, e, 0)
        return pltpu.make_async_copy(ye_ref.at[e, pl.ds(st, wc)], win.at[slot_par, e], sems.at[slot_par, e])

    @pl.when(t == 0)
    def _():
        for e in range(n_exp):
            fetch(0, 0, e).start()

    @pl.when(t + 1 < n_tiles)
    def _():
        for e in range(n_exp):
            fetch(t + 1, 1 - par, e).start()

    def spread(e, lo, st):
        col = slot_ref[:, e:e + 1]
        return jnp.where((col - st == lanes) & (col >= lo), aff_ref[:, e:e + 1], 0.0).astype(BF16)

    src_e = lax.broadcasted_iota(jnp.int32, (LANES, n_exp * wc), 0)
    dst_e = lax.broadcasted_iota(jnp.int32, (LANES, n_exp * wc), 1) // wc
    expand = (src_e == dst_e).astype(BF16)
    s1 = slot_ref[...] + 1
    rep = lambda v: jnp.dot(v.astype(BF16), expand, preferred_element_type=F32)
    slot_rep = rep(s1 // 64) * 64.0 + rep(s1 % 64) - 1.0
    aff_rep = rep(aff_ref[...])
    lo_vec = jnp.concatenate([jnp.full((1, wc), start_of(t, e, 0)[0], jnp.int32) for e in range(n_exp)], axis=1)
    st_vec = jnp.concatenate([jnp.full((1, wc), start_of(t, e, 0)[1], jnp.int32) for e in range(n_exp)], axis=1)
    lane_in_win = lax.broadcasted_iota(jnp.int32, (1, n_exp * wc), 1) % wc
    hit = (slot_rep == (st_vec + lane_in_win).astype(F32)) & (slot_rep >= lo_vec.astype(F32))
    onehot = jnp.where(hit, aff_rep, 0.0).astype(BF16)
    for e in range(n_exp):
        fetch(t, par, e).wait()
    acc[...] = jnp.dot(onehot, win[par].reshape(n_exp * wc, win.shape[3]), preferred_element_type=F32)
    for e in range(n_exp):
        s = base_ref[t * n_exp + e]
        n_win = (s - _window_start(s) + cnt_ref[t * n_exp + e] + wc - 1) // wc

        def more(k, c, e=e):
            lo_k, st_k = start_of(t, e, k)
            cp = pltpu.make_async_copy(ye_ref.at[e, pl.ds(st_k, wc)], extra, xsem)
            cp.start()
            cp.wait()
            acc[...] += jnp.dot(spread(e, lo_k, st_k), extra[...], preferred_element_type=F32)
            return c

        lax.fori_loop(1, n_win, more, 0)

    x = x_ref[...] + gate_ref[0] * acc[...]
    if final_norm:
        x = x * lax.rsqrt(jnp.mean(x * x, axis=-1, keepdims=True) + EPS) * fg_ref[...]
    o_ref[...] = x


def combine_rows(slot, aff, x2, gate, final_g, ye, base, cnt, seq, n_exp, capacity, tm, final_norm):
    n, d = x2.shape
    n_tiles = n // tm
    wc = DISPATCH_WINDOW
    assert wc <= capacity <= 64 * 256 and capacity % ROW_ALIGN == 0 and seq % tm == 0
    grid_spec = pltpu.PrefetchScalarGridSpec(
        num_scalar_prefetch=2,
        grid=(n_tiles,),
        in_specs=[pl.BlockSpec((tm, LANES), lambda t, b, c: (t, 0)),
                  pl.BlockSpec((tm, LANES), lambda t, b, c: (t, 0)),
                  pl.BlockSpec((tm, d), lambda t, b, c: (t, 0)),
                  pl.BlockSpec((1, 1, d), lambda t, b, c: ((t * tm) // seq, 0, 0)),
                  pl.BlockSpec((1, d), lambda t, b, c: (0, 0)),
                  pl.BlockSpec(memory_space=pl.ANY)],
        out_specs=pl.BlockSpec((tm, d), lambda t, b, c: (t, 0)),
        scratch_shapes=[pltpu.VMEM((2, n_exp, wc, d), BF16), pltpu.VMEM((wc, d), BF16), pltpu.VMEM((tm, d), F32),
                        pltpu.SemaphoreType.DMA((2, n_exp)), pltpu.SemaphoreType.DMA(())],
    )
    return pl.pallas_call(
        functools.partial(_combine_body, n_exp=n_exp, capacity=capacity, final_norm=final_norm),
        grid_spec=grid_spec,
        out_shape=jax.ShapeDtypeStruct((n, d), F32),
        compiler_params=_cparams("arbitrary"),
        name="combine_rows",
    )(base, cnt, slot, aff, x2, gate, final_g, ye)


def expert_choice_moe(x2, seq, norm_g, sc, sh, gate2, final_g, w_router, w_gate, w_up, w_down, final_norm):
    n, d = x2.shape
    n_exp = w_router.shape[1]
    capacity = CAPACITY_FACTOR * n // n_exp
    tm = _row_tile(seq, ROUTE_TILE)
    h, aff, aff_t = norm_router(x2, seq, norm_g, sc, sh, w_router)
    thr, cgt = expert_thresholds(aff_t, capacity)
    pad = LANES - thr.shape[0]
    thr_l = jnp.pad(thr[:, 0], (0, pad))[None, :]
    need_l = jnp.pad(capacity - cgt[:, 0], (0, pad))[None, :]
    t_gt, t_eq = tile_counts(aff, thr_l, tm)
    eq_base = jnp.cumsum(t_eq, axis=0) - t_eq
    t_sel = t_gt + jnp.clip(need_l - eq_base, 0, t_eq)
    sel_base = jnp.cumsum(t_sel, axis=0) - t_sel
    slot, slot_t = token_slots(aff, thr_l, need_l, eq_base, sel_base, n_exp, tm)
    base = sel_base[:, :n_exp].reshape(-1)
    cnt = t_sel[:, :n_exp].reshape(-1)
    xe = dispatch_rows(h, slot_t, base, cnt, n_exp, capacity, tm)
    ye = expert_ffn(xe, w_gate, w_up, w_down, capacity)
    return combine_rows(slot, aff, x2, gate2, final_g, ye, base, cnt, seq, n_exp, capacity, tm, final_norm)


def _trunk(x, c, params):
    (norm_mix_g, norm_ffn_g, ada_w, ada_b, hg_w_in, hg_w_out, hg_norm_g, lb_table,
     da_w_qkv, da_w_out, s5_ops, s5_d, s5_w_glu, moe_w_router, moe_w_gate, moe_w_up, moe_w_down, final_g) = params
    bsz, seq, d = x.shape
    depth = norm_mix_g.shape[0]
    x2 = x.reshape(bsz * seq, d)
    cond = jax.nn.silu(c)
    for layer in range(depth):
        kind, slot = layer % N_MIXERS, layer // N_MIXERS
        mod = (jnp.dot(cond, ada_w[layer], precision=HIGHEST) + ada_b[layer])[:, None, :]
        sh1, sc1, g1, sh2, sc2, g2 = jnp.split(mod, 6, axis=-1)
        gmix = norm_mix_g[layer][None, :]
        if kind == 0:
            proj = nm_matmul(x2, seq, gmix, sc1, sh1, hg_w_in[slot])
            lb = lb_table[layer].reshape(HG_HEADS, 1, -1)
            m = hgrn_recurrence(proj.reshape(bsz, seq, -1), lb, jnp.log1p(-lb), 1.0 - lb,
                                hg_norm_g[slot][None, :].astype(F32))
            x2 = proj_residual(m.reshape(bsz * seq, d), hg_w_out[slot], x2, g1, seq)
        elif kind == 1:
            proj = nm_matmul(x2, seq, gmix, sc1, sh1, da_w_qkv[slot])
            m = dilated_attention(proj.reshape(bsz, seq, -1))
            x2 = proj_residual(m.reshape(bsz * seq, d), da_w_out[slot], x2, g1, seq)
        else:
            h = norm_mod(x2, seq, gmix, sc1, sh1)
            z = s5_mix(h.reshape(bsz, seq, d), s5_ops[slot], s5_d[slot])
            x2 = proj_residual(z.reshape(bsz * seq, d), s5_w_glu[slot], x2, g1, seq, glu=True)
        x2 = expert_choice_moe(x2, seq, norm_ffn_g[layer][None, :], sc2, sh2, g2, final_g[None, :].astype(F32),
                               moe_w_router[layer], moe_w_gate[layer], moe_w_up[layer], moe_w_down[layer],
                               final_norm=(layer == depth - 1))
    return x2.reshape(bsz, seq, d)


def kernel(x_prompt, x_sample, c_prompt, c_sample, norm_mix_g, norm_ffn_g, ada_w, ada_b, hg_w_in, hg_w_out, hg_norm_g, hg_lb_logits, da_w_qkv, da_w_out, s5_a_re, s5_a_im, s5_log_dt, s5_b_re, s5_b_im, s5_c_re, s5_c_im, s5_d, s5_w_glu, moe_w_router, moe_w_gate, moe_w_up, moe_w_down, final_g):
    lb_table = jnp.cumsum(jax.nn.softmax(hg_lb_logits.astype(F32), axis=0), axis=0)
    lb_table = lb_table - lb_table[0:1]
    s5_ops = [_s5_operators(s5_a_re[s], s5_a_im[s], s5_log_dt[s], s5_b_re[s], s5_b_im[s],
                            s5_c_re[s], s5_c_im[s]) for s in range(s5_a_re.shape[0])]
    bf = lambda w: w.astype(BF16)
    params = (norm_mix_g.astype(F32), norm_ffn_g.astype(F32), ada_w, ada_b, bf(hg_w_in), bf(hg_w_out),
              hg_norm_g, lb_table, bf(da_w_qkv), bf(da_w_out), s5_ops, s5_d, bf(s5_w_glu), moe_w_router,
              bf(moe_w_gate), bf(moe_w_up), bf(moe_w_down), final_g)
    return (_trunk(x_prompt, c_prompt, params), _trunk(x_sample, c_sample, params))
```

```python
import functools
import math

import jax
import jax.numpy as jnp
from jax import lax
from jax.experimental import pallas as pl
from jax.experimental.pallas import tpu as pltpu

F32 = jnp.float32
BF16 = jnp.bfloat16
HIGHEST = lax.Precision.HIGHEST

EPS = 1e-6
N_MIXERS = 3
HG_HEADS = 8
HG_CHUNK = 64
HG_SUPER = 4
HG_FINISH = 8
DA_PATTERNS = ((128, 1), (512, 4), (2048, 16))
DA_HEADS = 16
DA_QBLOCK = 128
DA_SEGMENT = 8
DA_UNROLL = 8
ROPE_THETA = 10000.0
S5_GROUP = 16
S5_STATE = 64
S5_CHUNK = 16
S5_SEGMENTS = 8
CAPACITY_FACTOR = 2
EXPERT_ROWS = 16
ROUTE_TILE = 512
DISPATCH_WINDOW = 128
ROW_ALIGN = 16

LANES = 128
VMEM_LIMIT = 56 * 1024 * 1024
ROW_TILE = 1024


def _cparams(*sem):
    return pltpu.CompilerParams(dimension_semantics=sem, vmem_limit_bytes=VMEM_LIMIT)


def _row_tile(t, cap=ROW_TILE):
    return math.gcd(t, cap)


def _norm_mod(x, g, sc, sh):
    ms = jnp.mean(x * x, axis=-1, keepdims=True)
    return (x * lax.rsqrt(ms + EPS) * g) * (1.0 + sc) + sh


def _split2(x):
    hi = x.astype(BF16)
    return hi, (x - hi.astype(F32)).astype(BF16)


def _nm_matmul_body(x_ref, g_ref, sc_ref, sh_ref, w_ref, o_ref, h_scr):
    @pl.when(pl.program_id(1) == 0)
    def _():
        h_scr[...] = _norm_mod(x_ref[...], g_ref[...], sc_ref[0], sh_ref[0]).astype(BF16)

    o_ref[...] = jnp.dot(h_scr[...], w_ref[...], preferred_element_type=F32).astype(o_ref.dtype)


def nm_matmul(x2, seq, g, sc, sh, w, tn=1024):
    n, d = x2.shape
    f = w.shape[1]
    tm = _row_tile(seq)
    tn = math.gcd(f, tn)
    return pl.pallas_call(
        _nm_matmul_body,
        grid=(n // tm, f // tn),
        in_specs=[
            pl.BlockSpec((tm, d), lambda i, j: (i, 0)),
            pl.BlockSpec((1, d), lambda i, j: (0, 0)),
            pl.BlockSpec((1, 1, d), lambda i, j: ((i * tm) // seq, 0, 0)),
            pl.BlockSpec((1, 1, d), lambda i, j: ((i * tm) // seq, 0, 0)),
            pl.BlockSpec((d, tn), lambda i, j: (0, j)),
        ],
        out_specs=pl.BlockSpec((tm, tn), lambda i, j: (i, j)),
        out_shape=jax.ShapeDtypeStruct((n, f), BF16),
        scratch_shapes=[pltpu.VMEM((tm, d), BF16)],
        compiler_params=_cparams("parallel", "arbitrary"),
        name="nm_matmul",
    )(x2, g, sc, sh, w)


def _norm_mod_body(x_ref, g_ref, sc_ref, sh_ref, o_ref):
    o_ref[...] = _norm_mod(x_ref[...], g_ref[...], sc_ref[0], sh_ref[0])


def norm_mod(x2, seq, g, sc, sh):
    n, d = x2.shape
    tm = _row_tile(seq)
    return pl.pallas_call(
        _norm_mod_body,
        grid=(n // tm,),
        in_specs=[
            pl.BlockSpec((tm, d), lambda i: (i, 0)),
            pl.BlockSpec((1, d), lambda i: (0, 0)),
            pl.BlockSpec((1, 1, d), lambda i: ((i * tm) // seq, 0, 0)),
            pl.BlockSpec((1, 1, d), lambda i: ((i * tm) // seq, 0, 0)),
        ],
        out_specs=pl.BlockSpec((tm, d), lambda i: (i, 0)),
        out_shape=jax.ShapeDtypeStruct((n, d), F32),
        compiler_params=_cparams("parallel"),
        name="norm_mod",
    )(x2, g, sc, sh)


def _norm_router_body(x_ref, g_ref, sc_ref, sh_ref, wh_ref, wl_ref, h_ref, aff_ref, afft_ref, *, n_exp):
    h = _norm_mod(x_ref[...], g_ref[...], sc_ref[0], sh_ref[0])
    h_hi, h_lo = _split2(h)
    h_ref[...] = h_hi
    logits = (jnp.dot(h_hi, wh_ref[...], preferred_element_type=F32)
              + jnp.dot(h_lo, wh_ref[...], preferred_element_type=F32)
              + jnp.dot(h_hi, wl_ref[...], preferred_element_type=F32))
    lane = lax.broadcasted_iota(jnp.int32, (1, LANES), 1)
    logits = jnp.where(lane < n_exp, logits, -jnp.inf)
    ex = jnp.exp(logits - jnp.max(logits, axis=-1, keepdims=True))
    aff = ex / jnp.sum(ex, axis=-1, keepdims=True)
    aff_ref[...] = aff
    afft_ref[...] = aff.T[:EXPERT_ROWS, :]


def norm_router(x2, seq, g, sc, sh, w_router):
    n, d = x2.shape
    n_exp = w_router.shape[1]
    assert n_exp <= EXPERT_ROWS
    wr = jnp.zeros((d, LANES), F32).at[:, :n_exp].set(w_router.astype(F32))
    w_hi = wr.astype(BF16)
    w_lo = (wr - w_hi.astype(F32)).astype(BF16)
    tm = _row_tile(seq, ROUTE_TILE)
    return pl.pallas_call(
        functools.partial(_norm_router_body, n_exp=n_exp),
        grid=(n // tm,),
        in_specs=[
            pl.BlockSpec((tm, d), lambda i: (i, 0)),
            pl.BlockSpec((1, d), lambda i: (0, 0)),
            pl.BlockSpec((1, 1, d), lambda i: ((i * tm) // seq, 0, 0)),
            pl.BlockSpec((1, 1, d), lambda i: ((i * tm) // seq, 0, 0)),
            pl.BlockSpec((d, LANES), lambda i: (0, 0)),
            pl.BlockSpec((d, LANES), lambda i: (0, 0)),
        ],
        out_specs=[pl.BlockSpec((tm, d), lambda i: (i, 0)),
                   pl.BlockSpec((tm, LANES), lambda i: (i, 0)),
                   pl.BlockSpec((EXPERT_ROWS, tm), lambda i: (0, i))],
        out_shape=[jax.ShapeDtypeStruct((n, d), BF16), jax.ShapeDtypeStruct((n, LANES), F32),
                   jax.ShapeDtypeStruct((EXPERT_ROWS, n), F32)],
        compiler_params=_cparams("parallel"),
        name="norm_router",
    )(x2, g, sc, sh, w_hi, w_lo)


def _proj_res_body(m_ref, w_ref, x_ref, gate_ref, o_ref):
    y = jnp.dot(m_ref[...], w_ref[...], preferred_element_type=F32)
    o_ref[...] = x_ref[...] + gate_ref[0] * y


def _glu_res_body(m_ref, w_ref, x_ref, gate_ref, o_ref):
    d = x_ref.shape[-1]
    y = jnp.dot(m_ref[...].astype(BF16), w_ref[...], preferred_element_type=F32)
    o_ref[...] = x_ref[...] + gate_ref[0] * (y[:, :d] * jax.nn.sigmoid(y[:, d:]))


def proj_residual(m2, w, x2, gate, seq, glu=False):
    n, d = x2.shape
    f = w.shape[1]
    tm = _row_tile(seq, 512)
    return pl.pallas_call(
        _glu_res_body if glu else _proj_res_body,
        grid=(n // tm,),
        in_specs=[
            pl.BlockSpec((tm, d), lambda i: (i, 0)),
            pl.BlockSpec((d, f), lambda i: (0, 0)),
            pl.BlockSpec((tm, d), lambda i: (i, 0)),
            pl.BlockSpec((1, 1, d), lambda i: ((i * tm) // seq, 0, 0)),
        ],
        out_specs=pl.BlockSpec((tm, d), lambda i: (i, 0)),
        out_shape=jax.ShapeDtypeStruct((n, d), F32),
        compiler_params=_cparams("parallel"),
        name="glu_residual" if glu else "proj_residual",
    )(m2, w, x2, gate)


def _dot_nt(a, b):
    return lax.dot_general(a, b, (((1,), (1,)), ((), ())), preferred_element_type=F32)


def _hgrn_gates(z, lb, l1, om):
    e = jnp.exp(-jnp.abs(z))
    r = 1.0 / (1.0 + e)
    pos = z >= 0.0
    k = om * (jnp.where(pos, e, 1.0) * r)
    f = lb + om * (jnp.where(pos, 1.0, e) * r)
    log_sig = jnp.minimum(z, 0.0) + jnp.log(r)
    return jnp.maximum(jnp.log(f), l1 + log_sig), k


def _bcast_rows(x, rows, c, n_sub):
    return jnp.concatenate(
        [jnp.broadcast_to(x[rows[j]:rows[j] + 1, :], (c, x.shape[1])) for j in range(n_sub)], axis=0)


def _hgrn_intra(q, v, z, lb, l1, om, cum_mat, causal, ref_row, last_row, c, n_sub):
    logf, k = _hgrn_gates(z, lb, l1, om)
    l_hi, l_lo = _split2(logf)
    b = (jnp.dot(cum_mat, l_hi, preferred_element_type=F32)
         + jnp.dot(cum_mat, l_lo, preferred_element_type=F32))
    b_ref = _bcast_rows(b, [j * c + ref_row for j in range(n_sub)], c, n_sub)
    b_last = _bcast_rows(b, [j * c + last_row for j in range(n_sub)], c, n_sub)
    up, down = jnp.exp(b - b_ref), jnp.exp(b_ref - b)
    qd = (q * up).astype(BF16)
    kd = (k * down).astype(BF16)
    s = jnp.where(causal, _dot_nt(qd, kd), 0.0).astype(BF16)
    o = jnp.dot(s, v.astype(BF16), preferred_element_type=F32)
    qe = (q * (up * jnp.exp(b_ref))).astype(BF16)
    kl = (k * (down * jnp.exp(b_last - b_ref))).astype(BF16)
    kvs, decs = [], []
    for j in range(n_sub):
        r = slice(j * c, (j + 1) * c)
        kvs.append(jnp.dot(v[r].T.astype(BF16), kl[r], preferred_element_type=F32))
        decs.append(jnp.exp(b[j * c + last_row:j * c + last_row + 1, :]))
    return o, qe, kvs, decs


def _hgrn_body(q_ref, v_ref, zf_ref, zb_ref, g_ref, lb_ref, l1_ref, om_ref, ng_ref, o_ref,
               oi_scr, qe_scr, kv_scr, dec_scr, st_scr):
    seq = q_ref.shape[1]
    c = min(HG_CHUNK, seq)
    n_chunks = seq // c
    n_sub = math.gcd(n_chunks, HG_SUPER)
    sc = n_sub * c
    lb, l1, om = lb_ref[0], l1_ref[0], om_ref[0]
    row = lax.broadcasted_iota(jnp.int32, (sc, sc), 0)
    col = lax.broadcasted_iota(jnp.int32, (sc, sc), 1)
    same = (row // c) == (col // c)
    lower, upper = same & (row >= col), same & (row <= col)
    tril, triu = lower.astype(BF16), upper.astype(BF16)
    mid = c // 2

    def intra(i, carry):
        r = pl.ds(pl.multiple_of(i * sc, sc), sc)
        q, v = q_ref[0, r, :].astype(F32), v_ref[0, r, :].astype(F32)
        of, qf, kvf, decf = _hgrn_intra(q, v, zf_ref[0, r, :].astype(F32), lb, l1, om, tril, lower, mid - 1, c - 1, c, n_sub)
        ob, qb, kvb, decb = _hgrn_intra(q, v, zb_ref[0, r, :].astype(F32), lb, l1, om, triu, upper, c - mid, 0, c, n_sub)
        oi_scr[r, :] = of + ob
        qe_scr[r, 0:LANES] = qf
        qe_scr[r, LANES:2 * LANES] = qb
        for j in range(n_sub):
            n = i * n_sub + j
            kv_scr[0, n], kv_scr[1, n] = kvf[j], kvb[j]
            dec_scr[0, n], dec_scr[1, n] = jnp.broadcast_to(decf[j], (8, LANES)), jnp.broadcast_to(decb[j], (8, LANES))
        return carry

    lax.fori_loop(0, n_chunks // n_sub, intra, 0, unroll=2)

    def carry_state(n, carry):
        sf, sb = carry
        nb = n_chunks - 1 - n
        st_scr[n, :, 0:LANES] = sf.astype(BF16)
        st_scr[nb, :, LANES:2 * LANES] = sb.astype(BF16)
        sf = sf * dec_scr[0, n, 0:1, :] + kv_scr[0, n]
        sb = sb * dec_scr[1, nb, 0:1, :] + kv_scr[1, nb]
        return sf, sb

    zero = jnp.zeros((LANES, LANES), F32)
    lax.fori_loop(0, n_chunks, carry_state, (zero, zero))

    n_fin = math.gcd(n_chunks, HG_FINISH)
    fc = n_fin * c

    def finish(i, carry):
        parts = []
        for j in range(n_fin):
            n = i * n_fin + j
            parts.append(_dot_nt(qe_scr[pl.ds(pl.multiple_of(n * c, c), c), :], st_scr[n]))
        r = pl.ds(pl.multiple_of(i * fc, fc), fc)
        o = oi_scr[r, :] + jnp.concatenate(parts, axis=0)
        o = o * lax.rsqrt(jnp.mean(o * o, axis=-1, keepdims=True) + EPS) * ng_ref[...]
        o_ref[0, r, :] = (o * jax.nn.sigmoid(g_ref[0, r, :].astype(F32))).astype(o_ref.dtype)
        return carry

    lax.fori_loop(0, n_chunks // n_fin, finish, 0)


def hgrn_recurrence(proj, lb, l1, om, norm_g):
    bsz, seq, d5 = proj.shape
    d = d5 // 5
    h = HG_HEADS
    dk = d // h
    assert dk == LANES
    n_chunks = seq // min(HG_CHUNK, seq)

    def col(section):
        return pl.BlockSpec((1, seq, dk), lambda b, hh: (b, 0, section * h + hh))

    par = pl.BlockSpec((1, 1, dk), lambda b, hh: (hh, 0, 0))
    return pl.pallas_call(
        _hgrn_body,
        grid=(bsz, h),
        in_specs=[col(0), col(1), col(2), col(3), col(4), par, par, par,
                  pl.BlockSpec((1, dk), lambda b, hh: (0, 0))],
        out_specs=pl.BlockSpec((1, seq, dk), lambda b, hh: (b, 0, hh)),
        out_shape=jax.ShapeDtypeStruct((bsz, seq, d), BF16),
        scratch_shapes=[pltpu.VMEM((seq, dk), F32), pltpu.VMEM((seq, 2 * dk), BF16),
                        pltpu.VMEM((2, n_chunks, dk, dk), F32), pltpu.VMEM((2, n_chunks, 8, dk), F32),
                        pltpu.VMEM((n_chunks, dk, 2 * dk), BF16)],
        compiler_params=_cparams("parallel", "parallel"),
        name="hgrn_recurrence",
    )(proj, proj, proj, proj, proj, lb, l1, om, norm_g)


def _rope_pair(x, cos, sin_signed, swap):
    partner = jnp.dot(x.astype(BF16), swap, preferred_element_type=F32)
    return x * cos + partner * sin_signed


def _attn_group(q_ref, k_ref, v_ref, cos_ref, sin_ref, qs, ks, vs, ot, mt, lt, s_scr, p_scr, wide, dil, half):
    seq = q_ref.shape[1]
    length = seq // dil
    dh = LANES // 2
    lane = lax.broadcasted_iota(jnp.int32, (1, LANES), 1)
    src_lane = lax.broadcasted_iota(jnp.int32, (LANES, LANES), 0)
    dst_lane = lax.broadcasted_iota(jnp.int32, (LANES, LANES), 1)
    quarter = dh // 2
    swap = (src_lane == jnp.where((dst_lane % dh) < quarter, dst_lane + quarter, dst_lane - quarter)).astype(BF16)
    head0 = lane < dh
    tile = math.gcd(length, 256)
    if dil > 1:
        for j, ref in enumerate((q_ref, k_ref, v_ref)):
            for t0 in range(0, seq, 512):
                rows = pl.ds(t0, min(512, seq))
                wide[j, rows, :] = ref[0, rows, :].astype(F32)
    for r in range(dil):
        for t0 in range(0, length, tile):
            dst = pl.ds(r * length + t0, tile)
            if dil == 1:
                q, k, v = (ref[0, dst, :].astype(F32) for ref in (q_ref, k_ref, v_ref))
            else:
                src = pl.ds(r + t0 * dil, tile, stride=dil)
                q, k, v = wide[0, src, :], wide[1, src, :], wide[2, src, :]
            cos, sin = cos_ref[0, dst, :], sin_ref[0, dst, :]
            qs[dst, :] = (_rope_pair(q, cos, sin, swap) * (dh ** -0.5)).astype(BF16)
            ks[dst, :] = _rope_pair(k, cos, sin, swap).astype(BF16)
            vs[dst, :] = v.astype(BF16)

    qb = min(DA_QBLOCK, length)
    span = min(qb + 2 * half, length)
    n_blocks = seq // qb
    seg = math.gcd(n_blocks, DA_SEGMENT)
    delta = lax.broadcasted_iota(jnp.int32, (qb, span), 1) - lax.broadcasted_iota(jnp.int32, (qb, span), 0)

    def place(bi):
        row0 = pl.multiple_of(bi * qb, qb)
        r = row0 // length
        m0 = row0 - r * length
        k0 = jnp.clip(m0 - half, 0, length - span)
        return pl.ds(row0, qb), pl.ds(pl.multiple_of(r * length + k0, 16), span), k0 - m0

    def segment(si, carry):
        def scores(j, c):
            rows, krows, off = place(si * seg + j)
            q, kk = qs[rows, :], ks[krows, :]
            bias = jnp.where((delta >= -half - off) & (delta <= half - off), 0.0, -jnp.inf)
            zero = jnp.zeros_like(q)
            s_scr[j, 0, 0:qb, 0:span] = _dot_nt(jnp.where(head0, q, zero), kk) + bias
            s_scr[j, 1, 0:qb, 0:span] = _dot_nt(jnp.where(head0, zero, q), kk) + bias
            return c

        def softmax(j, c):
            rows, _, _ = place(si * seg + j)
            ms = []
            for h in range(2):
                s = s_scr[j, h, 0:qb, 0:span]
                m = jnp.max(s, axis=-1, keepdims=True)
                p_scr[j, h, 0:qb, 0:span] = jnp.exp(s - m).astype(BF16)
                ms.append(jnp.broadcast_to(m, (qb, LANES)))
            mt[rows, :] = jnp.where(head0, ms[0], ms[1])
            return c

        def values(j, c):
            rows, krows, _ = place(si * seg + j)
            vv = vs[krows, :]
            one = jnp.ones_like(vv)
            r0 = jnp.dot(p_scr[j, 0, 0:qb, 0:span], jnp.where(head0, vv, one), preferred_element_type=F32)
            r1 = jnp.dot(p_scr[j, 1, 0:qb, 0:span], jnp.where(head0, one, vv), preferred_element_type=F32)
            ot[rows, :] = jnp.where(head0, r0, r1)
            lt[rows, :] = pltpu.roll(jnp.where(head0, r1, r0), dh, 1)
            return c

        lax.fori_loop(0, seg, scores, 0, unroll=DA_UNROLL)
        lax.fori_loop(0, seg, softmax, 0, unroll=DA_UNROLL)
        lax.fori_loop(0, seg, values, 0, unroll=DA_UNROLL)
        return carry

    lax.fori_loop(0, n_blocks // seg, segment, 0)


def _attn_body(q_ref, k_ref, v_ref, cos_ref, sin_ref, o_ref, qs, ks, vs, ot, mt, lt, acc, mrun, lrun, s_scr, p_scr, wide):
    g = pl.program_id(2)
    seq = q_ref.shape[1]
    for gi, (window, dil) in enumerate(DA_PATTERNS):
        half = window // (2 * dil)
        assert half % 16 == 0
        length = seq // dil
        tile = math.gcd(length, 256)

        @pl.when(g == gi)
        def _(dil=dil, half=half, gi=gi, length=length, tile=tile):
            _attn_group(q_ref, k_ref, v_ref, cos_ref, sin_ref, qs, ks, vs, ot, mt, lt, s_scr, p_scr, wide, dil, half)
            for r in range(dil):
                for t0 in range(0, length, tile):
                    src = pl.ds(r * length + t0, tile)
                    dst = pl.ds(t0, tile) if dil == 1 else pl.ds(r + t0 * dil, tile, stride=dil)
                    o_new, m_new, l_new = ot[src, :], mt[src, :], lt[src, :]
                    if gi == 0:
                        acc[dst, :], mrun[dst, :], lrun[dst, :] = o_new, m_new, l_new
                    else:
                        m_old = mrun[dst, :]
                        m_all = jnp.maximum(m_old, m_new)
                        w_old, w_new = jnp.exp(m_old - m_all), jnp.exp(m_new - m_all)
                        acc[dst, :] = acc[dst, :] * w_old + o_new * w_new
                        lrun[dst, :] = lrun[dst, :] * w_old + l_new * w_new
                        mrun[dst, :] = m_all

    @pl.when(g == len(DA_PATTERNS) - 1)
    def _():
        tile = math.gcd(seq, 256)

        def finish(i, carry):
            r = pl.ds(pl.multiple_of(i * tile, tile), tile)
            o_ref[0, r, :] = (acc[r, :] / lrun[r, :]).astype(o_ref.dtype)
            return carry

        lax.fori_loop(0, seq // tile, finish, 0)


def _rope_tables(seq):
    dh = LANES // 2
    halfd = dh // 2
    inv = ROPE_THETA ** (-jnp.arange(halfd, dtype=F32) / halfd)
    cos_t, sin_t = [], []
    for _, dil in DA_PATTERNS:
        pos = jnp.arange(seq, dtype=F32).reshape(seq // dil, dil).T.reshape(seq)
        ang = pos[:, None] * inv[None, :]
        cos, sin = jnp.cos(ang), jnp.sin(ang)
        cos_t.append(jnp.tile(cos, (1, 4)))
        sin_t.append(jnp.concatenate([-sin, sin, -sin, sin], axis=1))
    return jnp.stack(cos_t), jnp.stack(sin_t)


def dilated_attention(proj):
    bsz, seq, d9 = proj.shape
    n_groups = len(DA_PATTERNS)
    d = d9 // (3 * n_groups)
    assert d // DA_HEADS == LANES // 2
    pairs = d // LANES
    cos, sin = _rope_tables(seq)
    qb_max = max(min(DA_QBLOCK, seq // dl) for _, dl in DA_PATTERNS)
    span_max = max(min(min(DA_QBLOCK, seq // dl) + 2 * (w // (2 * dl)), seq // dl) for w, dl in DA_PATTERNS)

    def col(part):
        return pl.BlockSpec((1, seq, LANES), lambda b, hp, g: (b, 0, (g * 3 + part) * pairs + hp))

    table = pl.BlockSpec((1, seq, LANES), lambda b, hp, g: (g, 0, 0))
    return pl.pallas_call(
        _attn_body,
        grid=(bsz, pairs, n_groups),
        in_specs=[col(0), col(1), col(2), table, table],
        out_specs=pl.BlockSpec((1, seq, LANES), lambda b, hp, g: (b, 0, hp)),
        out_shape=jax.ShapeDtypeStruct((bsz, seq, d), BF16),
        scratch_shapes=[pltpu.VMEM((seq, LANES), BF16)] * 3 + [pltpu.VMEM((seq, LANES), F32)] * 6
        + [pltpu.VMEM((DA_SEGMENT, 2, qb_max, span_max), F32),
           pltpu.VMEM((DA_SEGMENT, 2, qb_max, span_max), BF16), pltpu.VMEM((3, seq, LANES), F32)],
        compiler_params=_cparams("parallel", "parallel", "arbitrary"),
        name="dilated_attention",
    )(proj, proj, proj, cos, sin)


def _s5_operators(a_re, a_im, log_dt, b_re, b_im, c_re, c_im):
    lc, i_dim, p_dim = S5_CHUNK, S5_GROUP, S5_STATE
    n_groups = a_re.shape[1]
    hp = dict(precision=HIGHEST)
    a = lax.complex(a_re.astype(F32), a_im.astype(F32))
    lam = a * jnp.exp(log_dt.astype(F32))[..., None]
    a_bar = jnp.exp(lam)
    bmat = lax.complex(b_re.astype(F32), b_im.astype(F32))
    cmat = lax.complex(c_re.astype(F32), c_im.astype(F32))
    b_bar = ((a_bar - 1.0) / a)[..., None] * bmat[None]
    tau = jnp.arange(lc + 1, dtype=F32)
    apow = jnp.exp(lam[:, :, None, :] * tau[None, None, :, None])
    kern = jnp.real(jnp.einsum('gip,dgtp,dgpj->dgtij', cmat, apow[:, :, :lc], b_bar, **hp))
    s_idx = jnp.arange(lc)[:, None]
    t_idx = jnp.arange(lc)[None, :]
    lag = t_idx - s_idx
    m_f = jnp.where((lag >= 0)[None, :, :, None, None], kern[0][:, jnp.clip(lag, 0)], 0.0)
    m_b = jnp.where((lag <= 0)[None, :, :, None, None], kern[1][:, jnp.clip(-lag, 0)], 0.0)
    m_op = (m_f + m_b).transpose(0, 1, 4, 2, 3).reshape(n_groups, lc * i_dim, lc * i_dim)
    rev = jnp.arange(lc - 1, -1, -1)
    p_f = apow[0][:, rev][:, :, :, None] * b_bar[0][:, None]
    p_b = apow[1][:, :lc][:, :, :, None] * b_bar[1][:, None]
    p_op = jnp.stack([jnp.real(p_f), jnp.imag(p_f), jnp.real(p_b), jnp.imag(p_b)], axis=0)
    p_op = p_op.transpose(1, 2, 4, 0, 3).reshape(n_groups, lc * i_dim, 4, p_dim)
    q_f = cmat[:, None] * apow[0][:, 1:lc + 1][:, :, None, :]
    q_b = cmat[:, None] * apow[1][:, lc - jnp.arange(lc)][:, :, None, :]
    q_op = jnp.stack([jnp.real(q_f), -jnp.imag(q_f), jnp.real(q_b), -jnp.imag(q_b)], axis=0)
    q_op = q_op.transpose(1, 0, 4, 2, 3).reshape(n_groups, 4, p_dim, lc * i_dim)
    n_pairs = n_groups // 2
    eye = jnp.eye(2, dtype=F32)
    w = lc * i_dim
    m_pair = m_op.reshape(n_pairs, 2, w, w)
    p_pair = jnp.einsum('narqp,ab->narqbp', p_op.reshape(n_pairs, 2, w, 4, p_dim), eye)
    p_pair = p_pair.reshape(n_pairs, 2 * w, 8 * p_dim)
    q_pair = jnp.einsum('nbqpc,ab->nqbpac', q_op.reshape(n_pairs, 2, 4, p_dim, w), eye)
    q_pair = q_pair.reshape(n_pairs, 8 * p_dim, 2 * w)
    lam_chunk = (lam * lc).reshape(2, n_groups // 8, 8 * p_dim)
    tiles = n_groups // 8
    return (m_pair.astype(BF16).reshape(tiles, 4, 2, w, w), p_pair.astype(BF16).reshape(tiles, 4, 2 * w, 8 * p_dim),
            q_pair.astype(BF16).reshape(tiles, 4, 8 * p_dim, 2 * w), lam_chunk)


def _cmul(ar, ai, xr, xi):
    return ar * xr - ai * xi, ar * xi + ai * xr


def _s5_body(h_ref, perm_ref, m_ref, p_ref, q_ref, pw_ref, d_ref, o_ref, u_scr, uc_scr, v_scr, s_scr, yc_scr):
    seq = h_ref.shape[1]
    lc = S5_CHUNK
    n_chunks = seq // lc
    n_seg = S5_SEGMENTS
    ns = n_chunks // n_seg
    n_pairs = m_ref.shape[1]
    pw = p_ref.shape[2]
    sw = pw // 4
    qw = n_pairs * sw

    for s in range(n_seg):
        for t in range(lc):
            u_scr[t, pl.ds(s, ns, stride=n_seg), :] = h_ref[0, pl.ds(lc * s * ns + t, ns, stride=lc), :]
    u_all = jnp.concatenate([u_scr[t].astype(BF16) for t in range(lc)], axis=1)
    uc_scr[...] = jnp.dot(u_all, perm_ref[...], preferred_element_type=F32).astype(BF16)
    for p in range(n_pairs):
        vp = jnp.dot(uc_scr[:, p * pw:(p + 1) * pw], p_ref[0, p], preferred_element_type=F32)
        for c in range(4):
            v_scr[:, c * qw + p * sw:c * qw + (p + 1) * sw] = vp[:, c * sw:(c + 1) * sw]

    a_f = (pw_ref[0, 0, 1:2, :], pw_ref[0, 1, 1:2, :])
    a_b = (pw_ref[0, 2, 1:2, :], pw_ref[0, 3, 1:2, :])
    zero = jnp.zeros((n_seg, qw), F32)

    def scan(k, carry):
        f_re, f_im, b_re, b_im = carry
        rf = pl.ds(pl.multiple_of(k * n_seg, n_seg), n_seg)
        s_scr[rf, 0:qw] = f_re
        s_scr[rf, qw:2 * qw] = f_im
        n_re, n_im = _cmul(*a_f, f_re, f_im)
        rb = pl.ds(pl.multiple_of((ns - 1 - k) * n_seg, n_seg), n_seg)
        s_scr[rb, 2 * qw:3 * qw] = b_re
        s_scr[rb, 3 * qw:4 * qw] = b_im
        m_re, m_im = _cmul(*a_b, b_re, b_im)
        return (n_re + v_scr[rf, 0:qw], n_im + v_scr[rf, qw:2 * qw],
                m_re + v_scr[rb, 2 * qw:3 * qw], m_im + v_scr[rb, 3 * qw:4 * qw])

    f_re, f_im, b_re, b_im = lax.fori_loop(0, ns, scan, (zero, zero, zero, zero))

    a_seg_f = (pw_ref[0, 0, ns:ns + 1, :], pw_ref[0, 1, ns:ns + 1, :])
    a_seg_b = (pw_ref[0, 2, ns:ns + 1, :], pw_ref[0, 3, ns:ns + 1, :])
    row0 = jnp.zeros((1, qw), F32)
    cf = [(row0, row0)]
    for s in range(1, n_seg):
        xr, xi = _cmul(*a_seg_f, *cf[-1])
        cf.append((xr + f_re[s - 1:s, :], xi + f_im[s - 1:s, :]))
    cb = [(row0, row0)]
    for s in range(n_seg - 2, -1, -1):
        xr, xi = _cmul(*a_seg_b, *cb[0])
        cb.insert(0, (xr + b_re[s + 1:s + 2, :], xi + b_im[s + 1:s + 2, :]))
    cf_re, cf_im = (jnp.concatenate([c[j] for c in cf], axis=0) for j in range(2))
    cb_re, cb_im = (jnp.concatenate([c[j] for c in cb], axis=0) for j in range(2))

    def correct(k, carry):
        rf = pl.ds(pl.multiple_of(k * n_seg, n_seg), n_seg)
        xr, xi = _cmul(pw_ref[0, 0, pl.ds(k, 1), :], pw_ref[0, 1, pl.ds(k, 1), :], cf_re, cf_im)
        s_scr[rf, 0:qw] += xr
        s_scr[rf, qw:2 * qw] += xi
        kb = ns - 1 - k
        yr, yi = _cmul(pw_ref[0, 2, pl.ds(kb, 1), :], pw_ref[0, 3, pl.ds(kb, 1), :], cb_re, cb_im)
        s_scr[rf, 2 * qw:3 * qw] += yr
        s_scr[rf, 3 * qw:4 * qw] += yi
        return carry

    lax.fori_loop(0, ns, correct, 0)

    half = pw // 2
    for p in range(n_pairs):
        uc = uc_scr[:, p * pw:(p + 1) * pw]
        st = jnp.concatenate([s_scr[:, c * qw + p * sw:c * qw + (p + 1) * sw] for c in range(4)], axis=1)
        y = jnp.dot(st.astype(BF16), q_ref[0, p], preferred_element_type=F32)
        y = y + jnp.concatenate([jnp.dot(uc[:, :half], m_ref[0, p, 0], preferred_element_type=F32),
                                 jnp.dot(uc[:, half:], m_ref[0, p, 1], preferred_element_type=F32)], axis=1)
        yc_scr[:, p * pw:(p + 1) * pw] = y.astype(BF16)
    y_all = _dot_nt(yc_scr[...], perm_ref[...])
    for t in range(lc):
        u_scr[t] = y_all[:, t * LANES:(t + 1) * LANES]
    for s in range(n_seg):
        for t in range(lc):
            rows = pl.ds(lc * s * ns + t, ns, stride=lc)
            y = u_scr[t, pl.ds(s, ns, stride=n_seg), :] + d_ref[...] * h_ref[0, rows, :]
            o_ref[0, rows, :] = jax.nn.gelu(y)


def s5_mix(h3, ops, d_skip):
    m_op, p_op, q_op, lam_chunk = ops
    bsz, seq, d = h3.shape
    lc = S5_CHUNK
    tiles = d // LANES
    n_chunks = seq // lc
    width = lc * LANES
    assert seq % (lc * S5_SEGMENTS) == 0 and m_op.shape[0] == tiles
    ns = n_chunks // S5_SEGMENTS
    powers = jnp.exp(lam_chunk[:, :, None, :] * jnp.arange(ns + 1, dtype=F32)[None, None, :, None])
    pw_tab = jnp.stack([jnp.real(powers[0]), jnp.imag(powers[0]), jnp.real(powers[1]), jnp.imag(powers[1])], axis=1)
    src = jnp.arange(width)
    t_idx, g_idx, i_idx = src // LANES, (src % LANES) // S5_GROUP, src % S5_GROUP
    perm = (jnp.arange(width)[None, :] == (g_idx * (lc * S5_GROUP) + t_idx * S5_GROUP + i_idx)[:, None]).astype(BF16)
    return pl.pallas_call(
        _s5_body,
        grid=(tiles, bsz),
        in_specs=[
            pl.BlockSpec((1, seq, LANES), lambda l, b: (b, 0, l)),
            pl.BlockSpec((width, width), lambda l, b: (0, 0)),
            pl.BlockSpec((1,) + m_op.shape[1:], lambda l, b: (l, 0, 0, 0, 0)),
            pl.BlockSpec((1,) + p_op.shape[1:], lambda l, b: (l, 0, 0, 0)),
            pl.BlockSpec((1,) + q_op.shape[1:], lambda l, b: (l, 0, 0, 0)),
            pl.BlockSpec((1,) + pw_tab.shape[1:], lambda l, b: (l, 0, 0, 0)),
            pl.BlockSpec((1, LANES), lambda l, b: (0, l)),
        ],
        out_specs=pl.BlockSpec((1, seq, LANES), lambda l, b: (b, 0, l)),
        out_shape=jax.ShapeDtypeStruct((bsz, seq, d), F32),
        scratch_shapes=[pltpu.VMEM((lc, n_chunks, LANES), F32), pltpu.VMEM((n_chunks, width), BF16),
                        pltpu.VMEM((n_chunks, width), F32), pltpu.VMEM((n_chunks, width), F32),
                        pltpu.VMEM((n_chunks, width), BF16)],
        compiler_params=_cparams("parallel", "parallel"),
        name="s5_mix",
    )(h3, perm, m_op, p_op, q_op, pw_tab, d_skip.astype(F32)[None, :])


def _threshold_body(a_ref, thr_ref, cgt_ref, *, capacity):
    def bits():
        return lax.bitcast_convert_type(a_ref[...], jnp.int32)

    def step(i, thr):
        cand = thr | jnp.left_shift(jnp.int32(1), 30 - i)
        cnt = jnp.sum((bits() >= cand).astype(jnp.int32), axis=1, keepdims=True)
        return jnp.where(cnt >= capacity, cand, thr)

    thr = lax.fori_loop(0, 31, step, jnp.zeros((a_ref.shape[0], 1), jnp.int32))
    cgt = jnp.sum((bits() > thr).astype(jnp.int32), axis=1, keepdims=True)
    thr_ref[...] = jnp.broadcast_to(lax.bitcast_convert_type(thr, F32), thr_ref.shape)
    cgt_ref[...] = jnp.broadcast_to(cgt, cgt_ref.shape)


def expert_thresholds(aff_t, capacity):
    rows, n = aff_t.shape
    return pl.pallas_call(
        functools.partial(_threshold_body, capacity=capacity),
        grid=(1,),
        in_specs=[pl.BlockSpec((rows, n), lambda i: (0, 0))],
        out_specs=[pl.BlockSpec((rows, LANES), lambda i: (0, 0))] * 2,
        out_shape=[jax.ShapeDtypeStruct((rows, LANES), F32), jax.ShapeDtypeStruct((rows, LANES), jnp.int32)],
        compiler_params=_cparams("arbitrary"),
        name="expert_thresholds",
    )(aff_t)


def _tile_counts_body(a_ref, thr_ref, cgt_ref, ceq_ref):
    t = pl.program_id(0)
    a, thr = a_ref[...], thr_ref[...]
    cgt_ref[pl.ds(t, 1), :] = jnp.sum((a > thr).astype(jnp.int32), axis=0, keepdims=True)
    ceq_ref[pl.ds(t, 1), :] = jnp.sum((a == thr).astype(jnp.int32), axis=0, keepdims=True)


def tile_counts(aff, thr_l, tm):
    n = aff.shape[0]
    n_tiles = n // tm
    return pl.pallas_call(
        _tile_counts_body,
        grid=(n_tiles,),
        in_specs=[pl.BlockSpec((tm, LANES), lambda t: (t, 0)), pl.BlockSpec((1, LANES), lambda t: (0, 0))],
        out_specs=[pl.BlockSpec((n_tiles, LANES), lambda t: (0, 0))] * 2,
        out_shape=[jax.ShapeDtypeStruct((n_tiles, LANES), jnp.int32)] * 2,
        compiler_params=_cparams("arbitrary"),
        name="tile_counts",
    )(aff, thr_l)


def _slots_body(a_ref, thr_ref, need_ref, eqb_ref, selb_ref, slot_ref, slott_ref, *, n_exp):
    t = pl.program_id(0)
    tm = a_ref.shape[0]
    a, thr = a_ref[...], thr_ref[...]
    lane = lax.broadcasted_iota(jnp.int32, (1, LANES), 1)
    row = lax.broadcasted_iota(jnp.int32, (tm, tm), 0)
    col = lax.broadcasted_iota(jnp.int32, (tm, tm), 1)
    before = (row > col).astype(BF16)
    eq = a == thr
    eq_rank = jnp.dot(before, eq.astype(BF16), preferred_element_type=F32) + eqb_ref[pl.ds(t, 1), :].astype(F32)
    sel = ((a > thr) | (eq & (eq_rank < need_ref[...].astype(F32)))) & (lane < n_exp)
    pos = jnp.dot(before, sel.astype(BF16), preferred_element_type=F32) + selb_ref[pl.ds(t, 1), :].astype(F32)
    slot = jnp.where(sel, pos, -1.0)
    slot_ref[...] = slot.astype(jnp.int32)
    slott_ref[...] = slot.T[:EXPERT_ROWS, :].astype(jnp.int32)


def token_slots(aff, thr_l, need_l, eq_base, sel_base, n_exp, tm):
    n = aff.shape[0]
    n_tiles = n // tm
    full = lambda r: pl.BlockSpec((r, LANES), lambda t: (0, 0))
    return pl.pallas_call(
        functools.partial(_slots_body, n_exp=n_exp),
        grid=(n_tiles,),
        in_specs=[pl.BlockSpec((tm, LANES), lambda t: (t, 0)), full(1), full(1), full(n_tiles), full(n_tiles)],
        out_specs=[pl.BlockSpec((tm, LANES), lambda t: (t, 0)), pl.BlockSpec((EXPERT_ROWS, tm), lambda t: (0, t))],
        out_shape=[jax.ShapeDtypeStruct((n, LANES), jnp.int32), jax.ShapeDtypeStruct((EXPERT_ROWS, n), jnp.int32)],
        compiler_params=_cparams("parallel"),
        name="token_slots",
    )(aff, thr_l, need_l, eq_base, sel_base)


def _window_start(s):
    return pl.multiple_of((s // ROW_ALIGN) * ROW_ALIGN, ROW_ALIGN)


def _dispatch_body(base_ref, cnt_ref, slott_ref, h_ref, xe_ref, stage, extra, carry, sems, xsem, *, n_exp, capacity):
    t = pl.program_id(0)
    n_tiles = pl.num_programs(0)
    par = t % 2
    wc = stage.shape[2]
    sub = lax.broadcasted_iota(jnp.int32, (wc, 1), 0)

    @pl.when(t == 0)
    def _():
        carry[...] = jnp.zeros_like(carry)

    starts = [_window_start(base_ref[t * n_exp + e]) for e in range(n_exp)]
    onehot = jnp.concatenate([(slott_ref[e:e + 1, :] - starts[e] == sub) for e in range(n_exp)], axis=0)
    rows = jnp.dot(onehot.astype(BF16), h_ref[...], preferred_element_type=F32)

    def window_copy(e, k):
        return pltpu.make_async_copy(stage.at[par, e], xe_ref.at[e, pl.ds(starts[e] + k * wc, wc)], sems.at[e])

    @pl.when(t > 0)
    def _():
        for e in range(n_exp):
            pltpu.make_async_copy(stage.at[1 - par, e], xe_ref.at[e, pl.ds(0, wc)], sems.at[e]).wait()

    for e in range(n_exp):
        s = base_ref[t * n_exp + e]
        end16 = _window_start(s + cnt_ref[t * n_exp + e])
        n_win = (end16 - starts[e]) // wc + 1
        stage[par, e] = rows[e * wc:(e + 1) * wc].astype(BF16)
        stage[par, e, 0:ROW_ALIGN, :] += carry[e]
        window_copy(e, 0).start()

        def more(k, c, e=e):
            oh = slott_ref[e:e + 1, :] - (starts[e] + k * wc) == sub
            extra[...] = jnp.dot(oh.astype(BF16), h_ref[...], preferred_element_type=F32).astype(BF16)
            cp = pltpu.make_async_copy(extra, xe_ref.at[e, pl.ds(starts[e] + k * wc, wc)], xsem)
            cp.start()
            cp.wait()
            return c

        lax.fori_loop(1, n_win, more, 0)
        off = pl.multiple_of(end16 - starts[e] - (n_win - 1) * wc, ROW_ALIGN)

        @pl.when(n_win == 1)
        def _(e=e, off=off):
            carry[e] = stage[par, e, pl.ds(off, ROW_ALIGN), :]

        @pl.when(n_win > 1)
        def _(e=e, off=off):
            carry[e] = extra[pl.ds(off, ROW_ALIGN), :]

    @pl.when(t == n_tiles - 1)
    def _():
        for e in range(n_exp):
            window_copy(e, 0).wait()
        extra[...] = jnp.zeros_like(extra)
        for e in range(n_exp):
            cp = pltpu.make_async_copy(extra, xe_ref.at[e, pl.ds(capacity, wc)], xsem)
            cp.start()
            cp.wait()


def dispatch_rows(h, slot_t, base, cnt, n_exp, capacity, tm):
    n, d = h.shape
    n_tiles = n // tm
    wc = DISPATCH_WINDOW
    grid_spec = pltpu.PrefetchScalarGridSpec(
        num_scalar_prefetch=2,
        grid=(n_tiles,),
        in_specs=[pl.BlockSpec((EXPERT_ROWS, tm), lambda t, b, c: (0, t)),
                  pl.BlockSpec((tm, d), lambda t, b, c: (t, 0))],
        out_specs=pl.BlockSpec(memory_space=pl.ANY),
        scratch_shapes=[pltpu.VMEM((2, n_exp, wc, d), BF16), pltpu.VMEM((wc, d), BF16),
                        pltpu.VMEM((n_exp, ROW_ALIGN, d), BF16),
                        pltpu.SemaphoreType.DMA((n_exp,)), pltpu.SemaphoreType.DMA(())],
    )
    return pl.pallas_call(
        functools.partial(_dispatch_body, n_exp=n_exp, capacity=capacity),
        grid_spec=grid_spec,
        out_shape=jax.ShapeDtypeStruct((n_exp, capacity + wc, d), BF16),
        compiler_params=_cparams("arbitrary"),
        name="dispatch_rows",
    )(base, cnt, slot_t, h)


def _expert_body(x_ref, wg_ref, wu_ref, wd_ref, o_ref, *, tf):
    x = x_ref[0]
    acc = None
    for f0 in range(0, wg_ref.shape[2], tf):
        a = jnp.dot(x, wg_ref[0, :, f0:f0 + tf], preferred_element_type=F32)
        u = jnp.dot(x, wu_ref[0, :, f0:f0 + tf], preferred_element_type=F32)
        hid = (a * jax.nn.sigmoid(a) * u).astype(BF16)
        part = jnp.dot(hid, wd_ref[0, f0:f0 + tf, :], preferred_element_type=F32)
        acc = part if acc is None else acc + part
    o_ref[0] = acc.astype(o_ref.dtype)


def expert_ffn(xe, w_gate, w_up, w_down, capacity, tf=1024):
    e, _, d = xe.shape
    f = w_gate.shape[2]
    tm = math.gcd(capacity, ROW_TILE)
    tf = math.gcd(f, tf)
    return pl.pallas_call(
        functools.partial(_expert_body, tf=tf),
        grid=(e, capacity // tm),
        in_specs=[
            pl.BlockSpec((1, tm, d), lambda ei, i: (ei, i, 0)),
            pl.BlockSpec((1, d, f), lambda ei, i: (ei, 0, 0)),
            pl.BlockSpec((1, d, f), lambda ei, i: (ei, 0, 0)),
            pl.BlockSpec((1, f, d), lambda ei, i: (ei, 0, 0)),
        ],
        out_specs=pl.BlockSpec((1, tm, d), lambda ei, i: (ei, i, 0)),
        out_shape=jax.ShapeDtypeStruct((e, capacity, d), BF16),
        compiler_params=_cparams("parallel", "parallel"),
        name="expert_ffn",
    )(xe, w_gate, w_up, w_down)


def _combine_body(base_ref, cnt_ref, slot_ref, aff_ref, x_ref, gate_ref, fg_ref, ye_ref, o_ref,
                  win, extra, acc, sems, xsem, *, n_exp, capacity, final_norm):
    t = pl.program_id(0)
    n_tiles = pl.num_programs(0)
    par = t % 2
    wc = win.shape[2]
    lanes = lax.broadcasted_iota(jnp.int32, (1, wc), 1)

    def start_of(tt, e, k):
        lo = _window_start(base_ref[tt * n_exp + e]) + k * wc
        return lo, pl.multiple_of(jnp.minimum(lo, capacity - wc), ROW_ALIGN)

    def fetch(tt, slot_par, e):
        _, st = start_of(tt, e, 0)
        return pltpu.make_async_copy(ye_ref.at[e, pl.ds(st, wc)], win.at[slot_par, e], sems.at[slot_par, e])

    @pl.when(t == 0)
    def _():
        for e in range(n_exp):
            fetch(0, 0, e).start()

    @pl.when(t + 1 < n_tiles)
    def _():
        for e in range(n_exp):
            fetch(t + 1, 1 - par, e).start()

    def spread(e, lo, st):
        col = slot_ref[:, e:e + 1]
        return jnp.where((col - st == lanes) & (col >= lo), aff_ref[:, e:e + 1], 0.0).astype(BF16)

    src_e = lax.broadcasted_iota(jnp.int32, (LANES, n_exp * wc), 0)
    dst_e = lax.broadcasted_iota(jnp.int32, (LANES, n_exp * wc), 1) // wc
    expand = (src_e == dst_e).astype(BF16)
    s1 = slot_ref[...] + 1
    parts = jnp.concatenate([(s1 // 64).astype(BF16), (s1 % 64).astype(BF16)], axis=1)
    slot_rep = jnp.dot(parts, jnp.concatenate([expand * 64, expand], axis=0), preferred_element_type=F32) - 1.0
    aff_rep = jnp.dot(aff_ref[...].astype(BF16), expand, preferred_element_type=F32)
    lo_vec = jnp.concatenate([jnp.full((1, wc), start_of(t, e, 0)[0], jnp.int32) for e in range(n_exp)], axis=1)
    st_vec = jnp.concatenate([jnp.full((1, wc), start_of(t, e, 0)[1], jnp.int32) for e in range(n_exp)], axis=1)
    lane_in_win = lax.broadcasted_iota(jnp.int32, (1, n_exp * wc), 1) % wc
    hit = (slot_rep == (st_vec + lane_in_win).astype(F32)) & (slot_rep >= lo_vec.astype(F32))
    onehot = jnp.where(hit, aff_rep, 0.0).astype(BF16)
    for e in range(n_exp):
        fetch(t, par, e).wait()
    acc[...] = jnp.dot(onehot, win[par].reshape(n_exp * wc, win.shape[3]), preferred_element_type=F32)
    for e in range(n_exp):
        s = base_ref[t * n_exp + e]
        n_win = (s - _window_start(s) + cnt_ref[t * n_exp + e] + wc - 1) // wc

        def more(k, c, e=e):
            lo_k, st_k = start_of(t, e, k)
            cp = pltpu.make_async_copy(ye_ref.at[e, pl.ds(st_k, wc)], extra, xsem)
            cp.start()
            cp.wait()
            acc[...] += jnp.dot(spread(e, lo_k, st_k), extra[...], preferred_element_type=F32)
            return c

        lax.fori_loop(1, n_win, more, 0)

    x = x_ref[...] + gate_ref[0] * acc[...]
    if final_norm:
        x = x * lax.rsqrt(jnp.mean(x * x, axis=-1, keepdims=True) + EPS) * fg_ref[...]
    o_ref[...] = x


def combine_rows(slot, aff, x2, gate, final_g, ye, base, cnt, seq, n_exp, capacity, tm, final_norm):
    n, d = x2.shape
    n_tiles = n // tm
    wc = DISPATCH_WINDOW
    assert wc <= capacity <= 64 * 256 and capacity % ROW_ALIGN == 0 and seq % tm == 0
    grid_spec = pltpu.PrefetchScalarGridSpec(
        num_scalar_prefetch=2,
        grid=(n_tiles,),
        in_specs=[pl.BlockSpec((tm, LANES), lambda t, b, c: (t, 0)),
                  pl.BlockSpec((tm, LANES), lambda t, b, c: (t, 0)),
                  pl.BlockSpec((tm, d), lambda t, b, c: (t, 0)),
                  pl.BlockSpec((1, 1, d), lambda t, b, c: ((t * tm) // seq, 0, 0)),
                  pl.BlockSpec((1, d), lambda t, b, c: (0, 0)),
                  pl.BlockSpec(memory_space=pl.ANY)],
        out_specs=pl.BlockSpec((tm, d), lambda t, b, c: (t, 0)),
        scratch_shapes=[pltpu.VMEM((2, n_exp, wc, d), BF16), pltpu.VMEM((wc, d), BF16), pltpu.VMEM((tm, d), F32),
                        pltpu.SemaphoreType.DMA((2, n_exp)), pltpu.SemaphoreType.DMA(())],
    )
    return pl.pallas_call(
        functools.partial(_combine_body, n_exp=n_exp, capacity=capacity, final_norm=final_norm),
        grid_spec=grid_spec,
        out_shape=jax.ShapeDtypeStruct((n, d), F32),
        compiler_params=_cparams("arbitrary"),
        name="combine_rows",
    )(base, cnt, slot, aff, x2, gate, final_g, ye)


def expert_choice_moe(x2, seq, norm_g, sc, sh, gate2, final_g, w_router, w_gate, w_up, w_down, final_norm):
    n, d = x2.shape
    n_exp = w_router.shape[1]
    capacity = CAPACITY_FACTOR * n // n_exp
    tm = _row_tile(seq, ROUTE_TILE)
    h, aff, aff_t = norm_router(x2, seq, norm_g, sc, sh, w_router)
    thr, cgt = expert_thresholds(aff_t, capacity)
    pad = LANES - thr.shape[0]
    thr_l = jnp.pad(thr[:, 0], (0, pad))[None, :]
    need_l = jnp.pad(capacity - cgt[:, 0], (0, pad))[None, :]
    t_gt, t_eq = tile_counts(aff, thr_l, tm)
    eq_base = jnp.cumsum(t_eq, axis=0) - t_eq
    t_sel = t_gt + jnp.clip(need_l - eq_base, 0, t_eq)
    sel_base = jnp.cumsum(t_sel, axis=0) - t_sel
    slot, slot_t = token_slots(aff, thr_l, need_l, eq_base, sel_base, n_exp, tm)
    base = sel_base[:, :n_exp].reshape(-1)
    cnt = t_sel[:, :n_exp].reshape(-1)
    xe = dispatch_rows(h, slot_t, base, cnt, n_exp, capacity, tm)
    ye = expert_ffn(xe, w_gate, w_up, w_down, capacity)
    return combine_rows(slot, aff, x2, gate2, final_g, ye, base, cnt, seq, n_exp, capacity, tm, final_norm)


def _trunk(x, c, params):
    (norm_mix_g, norm_ffn_g, ada_w, ada_b, hg_w_in, hg_w_out, hg_norm_g, lb_table,
     da_w_qkv, da_w_out, s5_ops, s5_d, s5_w_glu, moe_w_router, moe_w_gate, moe_w_up, moe_w_down, final_g) = params
    bsz, seq, d = x.shape
    depth = norm_mix_g.shape[0]
    x2 = x.reshape(bsz * seq, d)
    cond = jax.nn.silu(c)
    for layer in range(depth):
        kind, slot = layer % N_MIXERS, layer // N_MIXERS
        mod = (jnp.dot(cond, ada_w[layer], precision=HIGHEST) + ada_b[layer])[:, None, :]
        sh1, sc1, g1, sh2, sc2, g2 = jnp.split(mod, 6, axis=-1)
        gmix = norm_mix_g[layer][None, :]
        if kind == 0:
            proj = nm_matmul(x2, seq, gmix, sc1, sh1, hg_w_in[slot])
            lb = lb_table[layer].reshape(HG_HEADS, 1, -1)
            m = hgrn_recurrence(proj.reshape(bsz, seq, -1), lb, jnp.log1p(-lb), 1.0 - lb,
                                hg_norm_g[slot][None, :].astype(F32))
            x2 = proj_residual(m.reshape(bsz * seq, d), hg_w_out[slot], x2, g1, seq)
        elif kind == 1:
            proj = nm_matmul(x2, seq, gmix, sc1, sh1, da_w_qkv[slot])
            m = dilated_attention(proj.reshape(bsz, seq, -1))
            x2 = proj_residual(m.reshape(bsz * seq, d), da_w_out[slot], x2, g1, seq)
        else:
            h = norm_mod(x2, seq, gmix, sc1, sh1)
            z = s5_mix(h.reshape(bsz, seq, d), s5_ops[slot], s5_d[slot])
            x2 = proj_residual(z.reshape(bsz * seq, d), s5_w_glu[slot], x2, g1, seq, glu=True)
        x2 = expert_choice_moe(x2, seq, norm_ffn_g[layer][None, :], sc2, sh2, g2, final_g[None, :].astype(F32),
                               moe_w_router[layer], moe_w_gate[layer], moe_w_up[layer], moe_w_down[layer],
                               final_norm=(layer == depth - 1))
    return x2.reshape(bsz, seq, d)


def kernel(x_prompt, x_sample, c_prompt, c_sample, norm_mix_g, norm_ffn_g, ada_w, ada_b, hg_w_in, hg_w_out, hg_norm_g, hg_lb_logits, da_w_qkv, da_w_out, s5_a_re, s5_a_im, s5_log_dt, s5_b_re, s5_b_im, s5_c_re, s5_c_im, s5_d, s5_w_glu, moe_w_router, moe_w_gate, moe_w_up, moe_w_down, final_g):
    lb_table = jnp.cumsum(jax.nn.softmax(hg_lb_logits.astype(F32), axis=0), axis=0)
    lb_table = lb_table - lb_table[0:1]
    s5_ops = [_s5_operators(s5_a_re[s], s5_a_im[s], s5_log_dt[s], s5_b_re[s], s5_b_im[s],
                            s5_c_re[s], s5_c_im[s]) for s in range(s5_a_re.shape[0])]
    bf = lambda w: w.astype(BF16)
    params = (norm_mix_g.astype(F32), norm_ffn_g.astype(F32), ada_w, ada_b, bf(hg_w_in), bf(hg_w_out),
              hg_norm_g, lb_table, bf(da_w_qkv), bf(da_w_out), s5_ops, s5_d, bf(s5_w_glu), moe_w_router,
              bf(moe_w_gate), bf(moe_w_up), bf(moe_w_down), final_g)
    return (_trunk(x_prompt, c_prompt, params), _trunk(x_sample, c_sample, params))
```

```python
import functools
import math

import jax
import jax.numpy as jnp
from jax import lax
from jax.experimental import pallas as pl
from jax.experimental.pallas import tpu as pltpu

F32 = jnp.float32
BF16 = jnp.bfloat16
HIGHEST = lax.Precision.HIGHEST

EPS = 1e-6
N_MIXERS = 3
HG_HEADS = 8
HG_CHUNK = 64
HG_SUPER = 4
HG_FINISH = 8
DA_PATTERNS = ((128, 1), (512, 4), (2048, 16))
DA_HEADS = 16
DA_QBLOCK = 128
DA_SEGMENT = 8
DA_UNROLL = 8
ROPE_THETA = 10000.0
S5_GROUP = 16
S5_STATE = 64
S5_CHUNK = 16
S5_SEGMENTS = 8
CAPACITY_FACTOR = 2
EXPERT_ROWS = 16
ROUTE_TILE = 512
DISPATCH_WINDOW = 128
ROW_ALIGN = 16

LANES = 128
VMEM_LIMIT = 56 * 1024 * 1024
ROW_TILE = 1024


def _cparams(*sem):
    return pltpu.CompilerParams(dimension_semantics=sem, vmem_limit_bytes=VMEM_LIMIT)


def _row_tile(t, cap=ROW_TILE):
    return math.gcd(t, cap)


def _norm_mod(x, g, sc, sh):
    ms = jnp.mean(x * x, axis=-1, keepdims=True)
    return (x * lax.rsqrt(ms + EPS) * g) * (1.0 + sc) + sh


def _split2(x):
    hi = x.astype(BF16)
    return hi, (x - hi.astype(F32)).astype(BF16)


def _nm_matmul_body(x_ref, g_ref, sc_ref, sh_ref, w_ref, o_ref, h_scr):
    @pl.when(pl.program_id(1) == 0)
    def _():
        h_scr[...] = _norm_mod(x_ref[...], g_ref[...], sc_ref[0], sh_ref[0]).astype(BF16)

    o_ref[...] = jnp.dot(h_scr[...], w_ref[...], preferred_element_type=F32).astype(o_ref.dtype)


def nm_matmul(x2, seq, g, sc, sh, w, tn=1024):
    n, d = x2.shape
    f = w.shape[1]
    tm = _row_tile(seq)
    tn = math.gcd(f, tn)
    return pl.pallas_call(
        _nm_matmul_body,
        grid=(n // tm, f // tn),
        in_specs=[
            pl.BlockSpec((tm, d), lambda i, j: (i, 0)),
            pl.BlockSpec((1, d), lambda i, j: (0, 0)),
            pl.BlockSpec((1, 1, d), lambda i, j: ((i * tm) // seq, 0, 0)),
            pl.BlockSpec((1, 1, d), lambda i, j: ((i * tm) // seq, 0, 0)),
            pl.BlockSpec((d, tn), lambda i, j: (0, j)),
        ],
        out_specs=pl.BlockSpec((tm, tn), lambda i, j: (i, j)),
        out_shape=jax.ShapeDtypeStruct((n, f), BF16),
        scratch_shapes=[pltpu.VMEM((tm, d), BF16)],
        compiler_params=_cparams("parallel", "arbitrary"),
        name="nm_matmul",
    )(x2, g, sc, sh, w)


def _norm_mod_body(x_ref, g_ref, sc_ref, sh_ref, o_ref):
    o_ref[...] = _norm_mod(x_ref[...], g_ref[...], sc_ref[0], sh_ref[0])


def norm_mod(x2, seq, g, sc, sh):
    n, d = x2.shape
    tm = _row_tile(seq)
    return pl.pallas_call(
        _norm_mod_body,
        grid=(n // tm,),
        in_specs=[
            pl.BlockSpec((tm, d), lambda i: (i, 0)),
            pl.BlockSpec((1, d), lambda i: (0, 0)),
            pl.BlockSpec((1, 1, d), lambda i: ((i * tm) // seq, 0, 0)),
            pl.BlockSpec((1, 1, d), lambda i: ((i * tm) // seq, 0, 0)),
        ],
        out_specs=pl.BlockSpec((tm, d), lambda i: (i, 0)),
        out_shape=jax.ShapeDtypeStruct((n, d), F32),
        compiler_params=_cparams("parallel"),
        name="norm_mod",
    )(x2, g, sc, sh)


def _norm_router_body(x_ref, g_ref, sc_ref, sh_ref, wh_ref, wl_ref, h_ref, aff_ref, afft_ref, *, n_exp):
    h = _norm_mod(x_ref[...], g_ref[...], sc_ref[0], sh_ref[0])
    h_hi, h_lo = _split2(h)
    h_ref[...] = h_hi
    logits = (jnp.dot(h_hi, wh_ref[...], preferred_element_type=F32)
              + jnp.dot(h_lo, wh_ref[...], preferred_element_type=F32)
              + jnp.dot(h_hi, wl_ref[...], preferred_element_type=F32))
    lane = lax.broadcasted_iota(jnp.int32, (1, LANES), 1)
    logits = jnp.where(lane < n_exp, logits, -jnp.inf)
    ex = jnp.exp(logits - jnp.max(logits, axis=-1, keepdims=True))
    aff = ex / jnp.sum(ex, axis=-1, keepdims=True)
    aff_ref[...] = aff
    afft_ref[...] = aff.T[:EXPERT_ROWS, :]


def norm_router(x2, seq, g, sc, sh, w_router):
    n, d = x2.shape
    n_exp = w_router.shape[1]
    assert n_exp <= EXPERT_ROWS
    wr = jnp.zeros((d, LANES), F32).at[:, :n_exp].set(w_router.astype(F32))
    w_hi = wr.astype(BF16)
    w_lo = (wr - w_hi.astype(F32)).astype(BF16)
    tm = _row_tile(seq, ROUTE_TILE)
    return pl.pallas_call(
        functools.partial(_norm_router_body, n_exp=n_exp),
        grid=(n // tm,),
        in_specs=[
            pl.BlockSpec((tm, d), lambda i: (i, 0)),
            pl.BlockSpec((1, d), lambda i: (0, 0)),
            pl.BlockSpec((1, 1, d), lambda i: ((i * tm) // seq, 0, 0)),
            pl.BlockSpec((1, 1, d), lambda i: ((i * tm) // seq, 0, 0)),
            pl.BlockSpec((d, LANES), lambda i: (0, 0)),
            pl.BlockSpec((d, LANES), lambda i: (0, 0)),
        ],
        out_specs=[pl.BlockSpec((tm, d), lambda i: (i, 0)),
                   pl.BlockSpec((tm, LANES), lambda i: (i, 0)),
                   pl.BlockSpec((EXPERT_ROWS, tm), lambda i: (0, i))],
        out_shape=[jax.ShapeDtypeStruct((n, d), BF16), jax.ShapeDtypeStruct((n, LANES), F32),
                   jax.ShapeDtypeStruct((EXPERT_ROWS, n), F32)],
        compiler_params=_cparams("parallel"),
        name="norm_router",
    )(x2, g, sc, sh, w_hi, w_lo)


def _proj_res_body(m_ref, w_ref, x_ref, gate_ref, o_ref):
    y = jnp.dot(m_ref[...], w_ref[...], preferred_element_type=F32)
    o_ref[...] = x_ref[...] + gate_ref[0] * y


def _glu_res_body(m_ref, w_ref, x_ref, gate_ref, o_ref):
    d = x_ref.shape[-1]
    y = jnp.dot(m_ref[...].astype(BF16), w_ref[...], preferred_element_type=F32)
    o_ref[...] = x_ref[...] + gate_ref[0] * (y[:, :d] * jax.nn.sigmoid(y[:, d:]))


def proj_residual(m2, w, x2, gate, seq, glu=False):
    n, d = x2.shape
    f = w.shape[1]
    tm = _row_tile(seq, 512)
    return pl.pallas_call(
        _glu_res_body if glu else _proj_res_body,
        grid=(n // tm,),
        in_specs=[
            pl.BlockSpec((tm, d), lambda i: (i, 0)),
            pl.BlockSpec((d, f), lambda i: (0, 0)),
            pl.BlockSpec((tm, d), lambda i: (i, 0)),
            pl.BlockSpec((1, 1, d), lambda i: ((i * tm) // seq, 0, 0)),
        ],
        out_specs=pl.BlockSpec((tm, d), lambda i: (i, 0)),
        out_shape=jax.ShapeDtypeStruct((n, d), F32),
        compiler_params=_cparams("parallel"),
        name="glu_residual" if glu else "proj_residual",
    )(m2, w, x2, gate)


def _dot_nt(a, b):
    return lax.dot_general(a, b, (((1,), (1,)), ((), ())), preferred_element_type=F32)


def _hgrn_gates(z, lb, l1, om):
    e = jnp.exp(-jnp.abs(z))
    r = 1.0 / (1.0 + e)
    pos = z >= 0.0
    k = om * (jnp.where(pos, e, 1.0) * r)
    f = lb + om * (jnp.where(pos, 1.0, e) * r)
    log_sig = jnp.minimum(z, 0.0) + jnp.log(r)
    return jnp.maximum(jnp.log(f), l1 + log_sig), k


def _bcast_rows(x, rows, c, n_sub):
    return jnp.concatenate(
        [jnp.broadcast_to(x[rows[j]:rows[j] + 1, :], (c, x.shape[1])) for j in range(n_sub)], axis=0)


def _hgrn_intra(q, v, z, lb, l1, om, cum_mat, causal, ref_row, last_row, c, n_sub):
    logf, k = _hgrn_gates(z, lb, l1, om)
    l_hi, l_lo = _split2(logf)
    b = (jnp.dot(cum_mat, l_hi, preferred_element_type=F32)
         + jnp.dot(cum_mat, l_lo, preferred_element_type=F32))
    b_ref = _bcast_rows(b, [j * c + ref_row for j in range(n_sub)], c, n_sub)
    b_last = _bcast_rows(b, [j * c + last_row for j in range(n_sub)], c, n_sub)
    up, down = jnp.exp(b - b_ref), jnp.exp(b_ref - b)
    qd = (q * up).astype(BF16)
    kd = (k * down).astype(BF16)
    s = jnp.where(causal, _dot_nt(qd, kd), 0.0).astype(BF16)
    o = jnp.dot(s, v.astype(BF16), preferred_element_type=F32)
    qe = (q * (up * jnp.exp(b_ref))).astype(BF16)
    kl = (k * (down * jnp.exp(b_last - b_ref))).astype(BF16)
    kvs, decs = [], []
    for j in range(n_sub):
        r = slice(j * c, (j + 1) * c)
        kvs.append(jnp.dot(v[r].T.astype(BF16), kl[r], preferred_element_type=F32))
        decs.append(jnp.exp(b[j * c + last_row:j * c + last_row + 1, :]))
    return o, qe, kvs, decs


def _hgrn_body(q_ref, v_ref, zf_ref, zb_ref, g_ref, lb_ref, l1_ref, om_ref, ng_ref, o_ref,
               oi_scr, qe_scr, kv_scr, dec_scr, st_scr):
    seq = q_ref.shape[1]
    c = min(HG_CHUNK, seq)
    n_chunks = seq // c
    n_sub = math.gcd(n_chunks, HG_SUPER)
    sc = n_sub * c
    lb, l1, om = lb_ref[0], l1_ref[0], om_ref[0]
    row = lax.broadcasted_iota(jnp.int32, (sc, sc), 0)
    col = lax.broadcasted_iota(jnp.int32, (sc, sc), 1)
    same = (row // c) == (col // c)
    lower, upper = same & (row >= col), same & (row <= col)
    tril, triu = lower.astype(BF16), upper.astype(BF16)
    mid = c // 2

    def intra(i, carry):
        r = pl.ds(pl.multiple_of(i * sc, sc), sc)
        q, v = q_ref[0, r, :].astype(F32), v_ref[0, r, :].astype(F32)
        of, qf, kvf, decf = _hgrn_intra(q, v, zf_ref[0, r, :].astype(F32), lb, l1, om, tril, lower, mid - 1, c - 1, c, n_sub)
        ob, qb, kvb, decb = _hgrn_intra(q, v, zb_ref[0, r, :].astype(F32), lb, l1, om, triu, upper, c - mid, 0, c, n_sub)
        oi_scr[r, :] = of + ob
        qe_scr[r, 0:LANES] = qf
        qe_scr[r, LANES:2 * LANES] = qb
        for j in range(n_sub):
            n = i * n_sub + j
            kv_scr[0, n], kv_scr[1, n] = kvf[j], kvb[j]
            dec_scr[0, n], dec_scr[1, n] = jnp.broadcast_to(decf[j], (8, LANES)), jnp.broadcast_to(decb[j], (8, LANES))
        return carry

    lax.fori_loop(0, n_chunks // n_sub, intra, 0, unroll=2)

    def carry_state(n, carry):
        sf, sb = carry
        nb = n_chunks - 1 - n
        st_scr[n, :, 0:LANES] = sf.astype(BF16)
        st_scr[nb, :, LANES:2 * LANES] = sb.astype(BF16)
        sf = sf * dec_scr[0, n, 0:1, :] + kv_scr[0, n]
        sb = sb * dec_scr[1, nb, 0:1, :] + kv_scr[1, nb]
        return sf, sb

    zero = jnp.zeros((LANES, LANES), F32)
    lax.fori_loop(0, n_chunks, carry_state, (zero, zero))

    n_fin = math.gcd(n_chunks, HG_FINISH)
    fc = n_fin * c

    def finish(i, carry):
        parts = []
        for j in range(n_fin):
            n = i * n_fin + j
            parts.append(_dot_nt(qe_scr[pl.ds(pl.multiple_of(n * c, c), c), :], st_scr[n]))
        r = pl.ds(pl.multiple_of(i * fc, fc), fc)
        o = oi_scr[r, :] + jnp.concatenate(parts, axis=0)
        o = o * lax.rsqrt(jnp.mean(o * o, axis=-1, keepdims=True) + EPS) * ng_ref[...]
        o_ref[0, r, :] = (o * jax.nn.sigmoid(g_ref[0, r, :].astype(F32))).astype(o_ref.dtype)
        return carry

    lax.fori_loop(0, n_chunks // n_fin, finish, 0)


def hgrn_recurrence(proj, lb, l1, om, norm_g):
    bsz, seq, d5 = proj.shape
    d = d5 // 5
    h = HG_HEADS
    dk = d // h
    assert dk == LANES
    n_chunks = seq // min(HG_CHUNK, seq)

    def col(section):
        return pl.BlockSpec((1, seq, dk), lambda b, hh: (b, 0, section * h + hh))

    par = pl.BlockSpec((1, 1, dk), lambda b, hh: (hh, 0, 0))
    return pl.pallas_call(
        _hgrn_body,
        grid=(bsz, h),
        in_specs=[col(0), col(1), col(2), col(3), col(4), par, par, par,
                  pl.BlockSpec((1, dk), lambda b, hh: (0, 0))],
        out_specs=pl.BlockSpec((1, seq, dk), lambda b, hh: (b, 0, hh)),
        out_shape=jax.ShapeDtypeStruct((bsz, seq, d), BF16),
        scratch_shapes=[pltpu.VMEM((seq, dk), F32), pltpu.VMEM((seq, 2 * dk), BF16),
                        pltpu.VMEM((2, n_chunks, dk, dk), F32), pltpu.VMEM((2, n_chunks, 8, dk), F32),
                        pltpu.VMEM((n_chunks, dk, 2 * dk), BF16)],
        compiler_params=_cparams("parallel", "parallel"),
        name="hgrn_recurrence",
    )(proj, proj, proj, proj, proj, lb, l1, om, norm_g)


def _rope_pair(x, cos, sin_signed, swap):
    partner = jnp.dot(x.astype(BF16), swap, preferred_element_type=F32)
    return x * cos + partner * sin_signed


def _attn_group(q_ref, k_ref, v_ref, cos_ref, sin_ref, qs, ks, vs, ot, mt, lt, s_scr, p_scr, wide, dil, half):
    seq = q_ref.shape[1]
    length = seq // dil
    dh = LANES // 2
    lane = lax.broadcasted_iota(jnp.int32, (1, LANES), 1)
    src_lane = lax.broadcasted_iota(jnp.int32, (LANES, LANES), 0)
    dst_lane = lax.broadcasted_iota(jnp.int32, (LANES, LANES), 1)
    quarter = dh // 2
    swap = (src_lane == jnp.where((dst_lane % dh) < quarter, dst_lane + quarter, dst_lane - quarter)).astype(BF16)
    head0 = lane < dh
    tile = math.gcd(length, 256)
    if dil > 1:
        for j, ref in enumerate((q_ref, k_ref, v_ref)):
            for t0 in range(0, seq, 512):
                rows = pl.ds(t0, min(512, seq))
                wide[j, rows, :] = ref[0, rows, :].astype(F32)
    for r in range(dil):
        for t0 in range(0, length, tile):
            dst = pl.ds(r * length + t0, tile)
            if dil == 1:
                q, k, v = (ref[0, dst, :].astype(F32) for ref in (q_ref, k_ref, v_ref))
            else:
                src = pl.ds(r + t0 * dil, tile, stride=dil)
                q, k, v = wide[0, src, :], wide[1, src, :], wide[2, src, :]
            cos, sin = cos_ref[0, dst, :], sin_ref[0, dst, :]
            qs[dst, :] = (_rope_pair(q, cos, sin, swap) * (dh ** -0.5)).astype(BF16)
            ks[dst, :] = _rope_pair(k, cos, sin, swap).astype(BF16)
            vs[dst, :] = v.astype(BF16)

    qb = min(DA_QBLOCK, length)
    span = min(qb + 2 * half, length)
    n_blocks = seq // qb
    seg = math.gcd(n_blocks, DA_SEGMENT)
    delta = lax.broadcasted_iota(jnp.int32, (qb, span), 1) - lax.broadcasted_iota(jnp.int32, (qb, span), 0)

    def place(bi):
        row0 = pl.multiple_of(bi * qb, qb)
        r = row0 // length
        m0 = row0 - r * length
        k0 = jnp.clip(m0 - half, 0, length - span)
        return pl.ds(row0, qb), pl.ds(pl.multiple_of(r * length + k0, 16), span), k0 - m0

    def segment(si, carry):
        def scores(j, c):
            rows, krows, off = place(si * seg + j)
            q, kk = qs[rows, :], ks[krows, :]
            bias = jnp.where((delta >= -half - off) & (delta <= half - off), 0.0, -jnp.inf)
            zero = jnp.zeros_like(q)
            s_scr[j, 0, 0:qb, 0:span] = _dot_nt(jnp.where(head0, q, zero), kk) + bias
            s_scr[j, 1, 0:qb, 0:span] = _dot_nt(jnp.where(head0, zero, q), kk) + bias
            return c

        def softmax(j, c):
            rows, _, _ = place(si * seg + j)
            ms = []
            for h in range(2):
                s = s_scr[j, h, 0:qb, 0:span]
                m = jnp.max(s, axis=-1, keepdims=True)
                p_scr[j, h, 0:qb, 0:span] = jnp.exp(s - m).astype(BF16)
                ms.append(jnp.broadcast_to(m, (qb, LANES)))
            mt[rows, :] = jnp.where(head0, ms[0], ms[1])
            return c

        def values(j, c):
            rows, krows, _ = place(si * seg + j)
            vv = vs[krows, :]
            one = jnp.ones_like(vv)
            r0 = jnp.dot(p_scr[j, 0, 0:qb, 0:span], jnp.where(head0, vv, one), preferred_element_type=F32)
            r1 = jnp.dot(p_scr[j, 1, 0:qb, 0:span], jnp.where(head0, one, vv), preferred_element_type=F32)
            ot[rows, :] = jnp.where(head0, r0, r1)
            lt[rows, :] = pltpu.roll(jnp.where(head0, r1, r0), dh, 1)
            return c

        lax.fori_loop(0, seg, scores, 0, unroll=DA_UNROLL)
        lax.fori_loop(0, seg, softmax, 0, unroll=DA_UNROLL)
        lax.fori_loop(0, seg, values, 0, unroll=DA_UNROLL)
        return carry

    lax.fori_loop(0, n_blocks // seg, segment, 0)


def _attn_body(q_ref, k_ref, v_ref, cos_ref, sin_ref, o_ref, qs, ks, vs, ot, mt, lt, acc, mrun, lrun, s_scr, p_scr, wide):
    g = pl.program_id(2)
    seq = q_ref.shape[1]
    for gi, (window, dil) in enumerate(DA_PATTERNS):
        half = window // (2 * dil)
        assert half % 16 == 0
        length = seq // dil
        tile = math.gcd(length, 256)

        @pl.when(g == gi)
        def _(dil=dil, half=half, gi=gi, length=length, tile=tile):
            _attn_group(q_ref, k_ref, v_ref, cos_ref, sin_ref, qs, ks, vs, ot, mt, lt, s_scr, p_scr, wide, dil, half)
            for r in range(dil):
                for t0 in range(0, length, tile):
                    src = pl.ds(r * length + t0, tile)
                    dst = pl.ds(t0, tile) if dil == 1 else pl.ds(r + t0 * dil, tile, stride=dil)
                    o_new, m_new, l_new = ot[src, :], mt[src, :], lt[src, :]
                    if gi == 0:
                        acc[dst, :], mrun[dst, :], lrun[dst, :] = o_new, m_new, l_new
                    else:
                        m_old = mrun[dst, :]
                        m_all = jnp.maximum(m_old, m_new)
                        w_old, w_new = jnp.exp(m_old - m_all), jnp.exp(m_new - m_all)
                        acc[dst, :] = acc[dst, :] * w_old + o_new * w_new
                        lrun[dst, :] = lrun[dst, :] * w_old + l_new * w_new
                        mrun[dst, :] = m_all

    @pl.when(g == len(DA_PATTERNS) - 1)
    def _():
        tile = math.gcd(seq, 256)

        def finish(i, carry):
            r = pl.ds(pl.multiple_of(i * tile, tile), tile)
            o_ref[0, r, :] = (acc[r, :] / lrun[r, :]).astype(o_ref.dtype)
            return carry

        lax.fori_loop(0, seq // tile, finish, 0)


def _rope_tables(seq):
    dh = LANES // 2
    halfd = dh // 2
    inv = ROPE_THETA ** (-jnp.arange(halfd, dtype=F32) / halfd)
    cos_t, sin_t = [], []
    for _, dil in DA_PATTERNS:
        pos = jnp.arange(seq, dtype=F32).reshape(seq // dil, dil).T.reshape(seq)
        ang = pos[:, None] * inv[None, :]
        cos, sin = jnp.cos(ang), jnp.sin(ang)
        cos_t.append(jnp.tile(cos, (1, 4)))
        sin_t.append(jnp.concatenate([-sin, sin, -sin, sin], axis=1))
    return jnp.stack(cos_t), jnp.stack(sin_t)


def dilated_attention(proj):
    bsz, seq, d9 = proj.shape
    n_groups = len(DA_PATTERNS)
    d = d9 // (3 * n_groups)
    assert d // DA_HEADS == LANES // 2
    pairs = d // LANES
    cos, sin = _rope_tables(seq)
    qb_max = max(min(DA_QBLOCK, seq // dl) for _, dl in DA_PATTERNS)
    span_max = max(min(min(DA_QBLOCK, seq // dl) + 2 * (w // (2 * dl)), seq // dl) for w, dl in DA_PATTERNS)

    def col(part):
        return pl.BlockSpec((1, seq, LANES), lambda b, hp, g: (b, 0, (g * 3 + part) * pairs + hp))

    table = pl.BlockSpec((1, seq, LANES), lambda b, hp, g: (g, 0, 0))
    return pl.pallas_call(
        _attn_body,
        grid=(bsz, pairs, n_groups),
        in_specs=[col(0), col(1), col(2), table, table],
        out_specs=pl.BlockSpec((1, seq, LANES), lambda b, hp, g: (b, 0, hp)),
        out_shape=jax.ShapeDtypeStruct((bsz, seq, d), BF16),
        scratch_shapes=[pltpu.VMEM((seq, LANES), BF16)] * 3 + [pltpu.VMEM((seq, LANES), F32)] * 6
        + [pltpu.VMEM((DA_SEGMENT, 2, qb_max, span_max), F32),
           pltpu.VMEM((DA_SEGMENT, 2, qb_max, span_max), BF16), pltpu.VMEM((3, seq, LANES), F32)],
        compiler_params=_cparams("parallel", "parallel", "arbitrary"),
        name="dilated_attention",
    )(proj, proj, proj, cos, sin)


def _s5_operators(a_re, a_im, log_dt, b_re, b_im, c_re, c_im):
    lc, i_dim, p_dim = S5_CHUNK, S5_GROUP, S5_STATE
    n_groups = a_re.shape[1]
    hp = dict(precision=HIGHEST)
    a = lax.complex(a_re.astype(F32), a_im.astype(F32))
    lam = a * jnp.exp(log_dt.astype(F32))[..., None]
    a_bar = jnp.exp(lam)
    bmat = lax.complex(b_re.astype(F32), b_im.astype(F32))
    cmat = lax.complex(c_re.astype(F32), c_im.astype(F32))
    b_bar = ((a_bar - 1.0) / a)[..., None] * bmat[None]
    tau = jnp.arange(lc + 1, dtype=F32)
    apow = jnp.exp(lam[:, :, None, :] * tau[None, None, :, None])
    kern = jnp.real(jnp.einsum('gip,dgtp,dgpj->dgtij', cmat, apow[:, :, :lc], b_bar, **hp))
    s_idx = jnp.arange(lc)[:, None]
    t_idx = jnp.arange(lc)[None, :]
    lag = t_idx - s_idx
    m_f = jnp.where((lag >= 0)[None, :, :, None, None], kern[0][:, jnp.clip(lag, 0)], 0.0)
    m_b = jnp.where((lag <= 0)[None, :, :, None, None], kern[1][:, jnp.clip(-lag, 0)], 0.0)
    m_op = (m_f + m_b).transpose(0, 1, 4, 2, 3).reshape(n_groups, lc * i_dim, lc * i_dim)
    rev = jnp.arange(lc - 1, -1, -1)
    p_f = apow[0][:, rev][:, :, :, None] * b_bar[0][:, None]
    p_b = apow[1][:, :lc][:, :, :, None] * b_bar[1][:, None]
    p_op = jnp.stack([jnp.real(p_f), jnp.imag(p_f), jnp.real(p_b), jnp.imag(p_b)], axis=0)
    p_op = p_op.transpose(1, 2, 4, 0, 3).reshape(n_groups, lc * i_dim, 4, p_dim)
    q_f = cmat[:, None] * apow[0][:, 1:lc + 1][:, :, None, :]
    q_b = cmat[:, None] * apow[1][:, lc - jnp.arange(lc)][:, :, None, :]
    q_op = jnp.stack([jnp.real(q_f), -jnp.imag(q_f), jnp.real(q_b), -jnp.imag(q_b)], axis=0)
    q_op = q_op.transpose(1, 0, 4, 2, 3).reshape(n_groups, 4, p_dim, lc * i_dim)
    n_pairs = n_groups // 2
    eye = jnp.eye(2, dtype=F32)
    w = lc * i_dim
    m_pair = m_op.reshape(n_pairs, 2, w, w)
    p_pair = jnp.einsum('narqp,ab->narqbp', p_op.reshape(n_pairs, 2, w, 4, p_dim), eye)
    p_pair = p_pair.reshape(n_pairs, 2 * w, 8 * p_dim)
    q_pair = jnp.einsum('nbqpc,ab->nqbpac', q_op.reshape(n_pairs, 2, 4, p_dim, w), eye)
    q_pair = q_pair.reshape(n_pairs, 8 * p_dim, 2 * w)
    lam_chunk = (lam * lc).reshape(2, n_groups // 8, 8 * p_dim)
    tiles = n_groups // 8
    return (m_pair.astype(BF16).reshape(tiles, 4, 2, w, w), p_pair.astype(BF16).reshape(tiles, 4, 2 * w, 8 * p_dim),
            q_pair.astype(BF16).reshape(tiles, 4, 8 * p_dim, 2 * w), lam_chunk)


def _cmul(ar, ai, xr, xi):
    return ar * xr - ai * xi, ar * xi + ai * xr


def _s5_body(h_ref, perm_ref, m_ref, p_ref, q_ref, pw_ref, d_ref, o_ref, u_scr, uc_scr, v_scr, s_scr, yc_scr):
    seq = h_ref.shape[1]
    lc = S5_CHUNK
    n_chunks = seq // lc
    n_seg = S5_SEGMENTS
    ns = n_chunks // n_seg
    n_pairs = m_ref.shape[1]
    pw = p_ref.shape[2]
    sw = pw // 4
    qw = n_pairs * sw

    for s in range(n_seg):
        for t in range(lc):
            u_scr[t, pl.ds(s, ns, stride=n_seg), :] = h_ref[0, pl.ds(lc * s * ns + t, ns, stride=lc), :]
    u_all = jnp.concatenate([u_scr[t].astype(BF16) for t in range(lc)], axis=1)
    uc_scr[...] = jnp.dot(u_all, perm_ref[...], preferred_element_type=F32).astype(BF16)
    for p in range(n_pairs):
        vp = jnp.dot(uc_scr[:, p * pw:(p + 1) * pw], p_ref[0, p], preferred_element_type=F32)
        for c in range(4):
            v_scr[:, c * qw + p * sw:c * qw + (p + 1) * sw] = vp[:, c * sw:(c + 1) * sw]

    a_f = (pw_ref[0, 0, 1:2, :], pw_ref[0, 1, 1:2, :])
    a_b = (pw_ref[0, 2, 1:2, :], pw_ref[0, 3, 1:2, :])
    zero = jnp.zeros((n_seg, qw), F32)

    def scan(k, carry):
        f_re, f_im, b_re, b_im = carry
        rf = pl.ds(pl.multiple_of(k * n_seg, n_seg), n_seg)
        s_scr[rf, 0:qw] = f_re
        s_scr[rf, qw:2 * qw] = f_im
        n_re, n_im = _cmul(*a_f, f_re, f_im)
        rb = pl.ds(pl.multiple_of((ns - 1 - k) * n_seg, n_seg), n_seg)
        s_scr[rb, 2 * qw:3 * qw] = b_re
        s_scr[rb, 3 * qw:4 * qw] = b_im
        m_re, m_im = _cmul(*a_b, b_re, b_im)
        return (n_re + v_scr[rf, 0:qw], n_im + v_scr[rf, qw:2 * qw],
                m_re + v_scr[rb, 2 * qw:3 * qw], m_im + v_scr[rb, 3 * qw:4 * qw])

    f_re, f_im, b_re, b_im = lax.fori_loop(0, ns, scan, (zero, zero, zero, zero))

    a_seg_f = (pw_ref[0, 0, ns:ns + 1, :], pw_ref[0, 1, ns:ns + 1, :])
    a_seg_b = (pw_ref[0, 2, ns:ns + 1, :], pw_ref[0, 3, ns:ns + 1, :])
    row0 = jnp.zeros((1, qw), F32)
    cf = [(row0, row0)]
    for s in range(1, n_seg):
        xr, xi = _cmul(*a_seg_f, *cf[-1])
        cf.append((xr + f_re[s - 1:s, :], xi + f_im[s - 1:s, :]))
    cb = [(row0, row0)]
    for s in range(n_seg - 2, -1, -1):
        xr, xi = _cmul(*a_seg_b, *cb[0])
        cb.insert(0, (xr + b_re[s + 1:s + 2, :], xi + b_im[s + 1:s + 2, :]))
    cf_re, cf_im = (jnp.concatenate([c[j] for c in cf], axis=0) for j in range(2))
    cb_re, cb_im = (jnp.concatenate([c[j] for c in cb], axis=0) for j in range(2))

    def correct(k, carry):
        rf = pl.ds(pl.multiple_of(k * n_seg, n_seg), n_seg)
        xr, xi = _cmul(pw_ref[0, 0, pl.ds(k, 1), :], pw_ref[0, 1, pl.ds(k, 1), :], cf_re, cf_im)
        s_scr[rf, 0:qw] += xr
        s_scr[rf, qw:2 * qw] += xi
        kb = ns - 1 - k
        yr, yi = _cmul(pw_ref[0, 2, pl.ds(kb, 1), :], pw_ref[0, 3, pl.ds(kb, 1), :], cb_re, cb_im)
        s_scr[rf, 2 * qw:3 * qw] += yr
        s_scr[rf, 3 * qw:4 * qw] += yi
        return carry

    lax.fori_loop(0, ns, correct, 0)

    half = pw // 2
    for p in range(n_pairs):
        uc = uc_scr[:, p * pw:(p + 1) * pw]
        st = jnp.concatenate([s_scr[:, c * qw + p * sw:c * qw + (p + 1) * sw] for c in range(4)], axis=1)
        y = jnp.dot(st.astype(BF16), q_ref[0, p], preferred_element_type=F32)
        y = y + jnp.concatenate([jnp.dot(uc[:, :half], m_ref[0, p, 0], preferred_element_type=F32),
                                 jnp.dot(uc[:, half:], m_ref[0, p, 1], preferred_element_type=F32)], axis=1)
        yc_scr[:, p * pw:(p + 1) * pw] = y.astype(BF16)
    y_all = _dot_nt(yc_scr[...], perm_ref[...])
    for t in range(lc):
        u_scr[t] = y_all[:, t * LANES:(t + 1) * LANES]
    for s in range(n_seg):
        for t in range(lc):
            rows = pl.ds(lc * s * ns + t, ns, stride=lc)
            y = u_scr[t, pl.ds(s, ns, stride=n_seg), :] + d_ref[...] * h_ref[0, rows, :]
            o_ref[0, rows, :] = jax.nn.gelu(y)


def s5_mix(h3, ops, d_skip):
    m_op, p_op, q_op, lam_chunk = ops
    bsz, seq, d = h3.shape
    lc = S5_CHUNK
    tiles = d // LANES
    n_chunks = seq // lc
    width = lc * LANES
    assert seq % (lc * S5_SEGMENTS) == 0 and m_op.shape[0] == tiles
    ns = n_chunks // S5_SEGMENTS
    powers = jnp.exp(lam_chunk[:, :, None, :] * jnp.arange(ns + 1, dtype=F32)[None, None, :, None])
    pw_tab = jnp.stack([jnp.real(powers[0]), jnp.imag(powers[0]), jnp.real(powers[1]), jnp.imag(powers[1])], axis=1)
    src = jnp.arange(width)
    t_idx, g_idx, i_idx = src // LANES, (src % LANES) // S5_GROUP, src % S5_GROUP
    perm = (jnp.arange(width)[None, :] == (g_idx * (lc * S5_GROUP) + t_idx * S5_GROUP + i_idx)[:, None]).astype(BF16)
    return pl.pallas_call(
        _s5_body,
        grid=(tiles, bsz),
        in_specs=[
            pl.BlockSpec((1, seq, LANES), lambda l, b: (b, 0, l)),
            pl.BlockSpec((width, width), lambda l, b: (0, 0)),
            pl.BlockSpec((1,) + m_op.shape[1:], lambda l, b: (l, 0, 0, 0, 0)),
            pl.BlockSpec((1,) + p_op.shape[1:], lambda l, b: (l, 0, 0, 0)),
            pl.BlockSpec((1,) + q_op.shape[1:], lambda l, b: (l, 0, 0, 0)),
            pl.BlockSpec((1,) + pw_tab.shape[1:], lambda l, b: (l, 0, 0, 0)),
            pl.BlockSpec((1, LANES), lambda l, b: (0, l)),
        ],
        out_specs=pl.BlockSpec((1, seq, LANES), lambda l, b: (b, 0, l)),
        out_shape=jax.ShapeDtypeStruct((bsz, seq, d), F32),
        scratch_shapes=[pltpu.VMEM((lc, n_chunks, LANES), F32), pltpu.VMEM((n_chunks, width), BF16),
                        pltpu.VMEM((n_chunks, width), F32), pltpu.VMEM((n_chunks, width), F32),
                        pltpu.VMEM((n_chunks, width), BF16)],
        compiler_params=_cparams("parallel", "parallel"),
        name="s5_mix",
    )(h3, perm, m_op, p_op, q_op, pw_tab, d_skip.astype(F32)[None, :])


def _threshold_body(a_ref, thr_ref, cgt_ref, *, capacity):
    def bits():
        return lax.bitcast_convert_type(a_ref[...], jnp.int32)

    def step(i, thr):
        cand = thr | jnp.left_shift(jnp.int32(1), 30 - i)
        cnt = jnp.sum((bits() >= cand).astype(jnp.int32), axis=1, keepdims=True)
        return jnp.where(cnt >= capacity, cand, thr)

    thr = lax.fori_loop(0, 31, step, jnp.zeros((a_ref.shape[0], 1), jnp.int32))
    cgt = jnp.sum((bits() > thr).astype(jnp.int32), axis=1, keepdims=True)
    thr_ref[...] = jnp.broadcast_to(lax.bitcast_convert_type(thr, F32), thr_ref.shape)
    cgt_ref[...] = jnp.broadcast_to(cgt, cgt_ref.shape)


def expert_thresholds(aff_t, capacity):
    rows, n = aff_t.shape
    return pl.pallas_call(
        functools.partial(_threshold_body, capacity=capacity),
        grid=(1,),
        in_specs=[pl.BlockSpec((rows, n), lambda i: (0, 0))],
        out_specs=[pl.BlockSpec((rows, LANES), lambda i: (0, 0))] * 2,
        out_shape=[jax.ShapeDtypeStruct((rows, LANES), F32), jax.ShapeDtypeStruct((rows, LANES), jnp.int32)],
        compiler_params=_cparams("arbitrary"),
        name="expert_thresholds",
    )(aff_t)


def _tile_counts_body(a_ref, thr_ref, cgt_ref, ceq_ref):
    t = pl.program_id(0)
    a, thr = a_ref[...], thr_ref[...]
    cgt_ref[pl.ds(t, 1), :] = jnp.sum((a > thr).astype(jnp.int32), axis=0, keepdims=True)
    ceq_ref[pl.ds(t, 1), :] = jnp.sum((a == thr).astype(jnp.int32), axis=0, keepdims=True)


def tile_counts(aff, thr_l, tm):
    n = aff.shape[0]
    n_tiles = n // tm
    return pl.pallas_call(
        _tile_counts_body,
        grid=(n_tiles,),
        in_specs=[pl.BlockSpec((tm, LANES), lambda t: (t, 0)), pl.BlockSpec((1, LANES), lambda t: (0, 0))],
        out_specs=[pl.BlockSpec((n_tiles, LANES), lambda t: (0, 0))] * 2,
        out_shape=[jax.ShapeDtypeStruct((n_tiles, LANES), jnp.int32)] * 2,
        compiler_params=_cparams("arbitrary"),
        name="tile_counts",
    )(aff, thr_l)


def _slots_body(a_ref, thr_ref, need_ref, eqb_ref, selb_ref, slot_ref, slott_ref, *, n_exp):
    t = pl.program_id(0)
    tm = a_ref.shape[0]
    a, thr = a_ref[...], thr_ref[...]
    lane = lax.broadcasted_iota(jnp.int32, (1, LANES), 1)
    row = lax.broadcasted_iota(jnp.int32, (tm, tm), 0)
    col = lax.broadcasted_iota(jnp.int32, (tm, tm), 1)
    before = (row > col).astype(BF16)
    eq = a == thr
    eq_rank = jnp.dot(before, eq.astype(BF16), preferred_element_type=F32) + eqb_ref[pl.ds(t, 1), :].astype(F32)
    sel = ((a > thr) | (eq & (eq_rank < need_ref[...].astype(F32)))) & (lane < n_exp)
    pos = jnp.dot(before, sel.astype(BF16), preferred_element_type=F32) + selb_ref[pl.ds(t, 1), :].astype(F32)
    slot = jnp.where(sel, pos, -1.0)
    slot_ref[...] = slot.astype(jnp.int32)
    slott_ref[...] = slot.T[:EXPERT_ROWS, :].astype(jnp.int32)


def token_slots(aff, thr_l, need_l, eq_base, sel_base, n_exp, tm):
    n = aff.shape[0]
    n_tiles = n // tm
    full = lambda r: pl.BlockSpec((r, LANES), lambda t: (0, 0))
    return pl.pallas_call(
        functools.partial(_slots_body, n_exp=n_exp),
        grid=(n_tiles,),
        in_specs=[pl.BlockSpec((tm, LANES), lambda t: (t, 0)), full(1), full(1), full(n_tiles), full(n_tiles)],
        out_specs=[pl.BlockSpec((tm, LANES), lambda t: (t, 0)), pl.BlockSpec((EXPERT_ROWS, tm), lambda t: (0, t))],
        out_shape=[jax.ShapeDtypeStruct((n, LANES), jnp.int32), jax.ShapeDtypeStruct((EXPERT_ROWS, n), jnp.int32)],
        compiler_params=_cparams("parallel"),
        name="token_slots",
    )(aff, thr_l, need_l, eq_base, sel_base)


def _window_start(s):
    return pl.multiple_of((s // ROW_ALIGN) * ROW_ALIGN, ROW_ALIGN)


def _dispatch_body(base_ref, cnt_ref, slott_ref, h_ref, xe_ref, stage, extra, carry, sems, xsem, *, n_exp, capacity):
    t = pl.program_id(0)
    n_tiles = pl.num_programs(0)
    par = t % 2
    wc = stage.shape[2]
    sub = lax.broadcasted_iota(jnp.int32, (wc, 1), 0)

    @pl.when(t == 0)
    def _():
        carry[...] = jnp.zeros_like(carry)

    starts = [_window_start(base_ref[t * n_exp + e]) for e in range(n_exp)]
    onehot = jnp.concatenate([(slott_ref[e:e + 1, :] - starts[e] == sub) for e in range(n_exp)], axis=0)
    rows = jnp.dot(onehot.astype(BF16), h_ref[...], preferred_element_type=F32)

    def window_copy(e, k):
        return pltpu.make_async_copy(stage.at[par, e], xe_ref.at[e, pl.ds(starts[e] + k * wc, wc)], sems.at[e])

    @pl.when(t > 0)
    def _():
        for e in range(n_exp):
            pltpu.make_async_copy(stage.at[1 - par, e], xe_ref.at[e, pl.ds(0, wc)], sems.at[e]).wait()

    for e in range(n_exp):
        s = base_ref[t * n_exp + e]
        end16 = _window_start(s + cnt_ref[t * n_exp + e])
        n_win = (end16 - starts[e]) // wc + 1
        stage[par, e] = rows[e * wc:(e + 1) * wc].astype(BF16)
        stage[par, e, 0:ROW_ALIGN, :] += carry[e]
        window_copy(e, 0).start()

        def more(k, c, e=e):
            oh = slott_ref[e:e + 1, :] - (starts[e] + k * wc) == sub
            extra[...] = jnp.dot(oh.astype(BF16), h_ref[...], preferred_element_type=F32).astype(BF16)
            cp = pltpu.make_async_copy(extra, xe_ref.at[e, pl.ds(starts[e] + k * wc, wc)], xsem)
            cp.start()
            cp.wait()
            return c

        lax.fori_loop(1, n_win, more, 0)
        off = pl.multiple_of(end16 - starts[e] - (n_win - 1) * wc, ROW_ALIGN)

        @pl.when(n_win == 1)
        def _(e=e, off=off):
            carry[e] = stage[par, e, pl.ds(off, ROW_ALIGN), :]

        @pl.when(n_win > 1)
        def _(e=e, off=off):
            carry[e] = extra[pl.ds(off, ROW_ALIGN), :]

    @pl.when(t == n_tiles - 1)
    def _():
        for e in range(n_exp):
            window_copy(e, 0).wait()
        extra[...] = jnp.zeros_like(extra)
        for e in range(n_exp):
            cp = pltpu.make_async_copy(extra, xe_ref.at[e, pl.ds(capacity, wc)], xsem)
            cp.start()
            cp.wait()


def dispatch_rows(h, slot_t, base, cnt, n_exp, capacity, tm):
    n, d = h.shape
    n_tiles = n // tm
    wc = DISPATCH_WINDOW
    grid_spec = pltpu.PrefetchScalarGridSpec(
        num_scalar_prefetch=2,
        grid=(n_tiles,),
        in_specs=[pl.BlockSpec((EXPERT_ROWS, tm), lambda t, b, c: (0, t)),
                  pl.BlockSpec((tm, d), lambda t, b, c: (t, 0))],
        out_specs=pl.BlockSpec(memory_space=pl.ANY),
        scratch_shapes=[pltpu.VMEM((2, n_exp, wc, d), BF16), pltpu.VMEM((wc, d), BF16),
                        pltpu.VMEM((n_exp, ROW_ALIGN, d), BF16),
                        pltpu.SemaphoreType.DMA((n_exp,)), pltpu.SemaphoreType.DMA(())],
    )
    return pl.pallas_call(
        functools.partial(_dispatch_body, n_exp=n_exp, capacity=capacity),
        grid_spec=grid_spec,
        out_shape=jax.ShapeDtypeStruct((n_exp, capacity + wc, d), BF16),
        compiler_params=_cparams("arbitrary"),
        name="dispatch_rows",
    )(base, cnt, slot_t, h)


def _expert_body(x_ref, wg_ref, wu_ref, wd_ref, o_ref, *, tf):
    x = x_ref[0]
    acc = None
    for f0 in range(0, wg_ref.shape[2], tf):
        a = jnp.dot(x, wg_ref[0, :, f0:f0 + tf], preferred_element_type=F32)
        u = jnp.dot(x, wu_ref[0, :, f0:f0 + tf], preferred_element_type=F32)
        hid = (a * jax.nn.sigmoid(a) * u).astype(BF16)
        part = jnp.dot(hid, wd_ref[0, f0:f0 + tf, :], preferred_element_type=F32)
        acc = part if acc is None else acc + part
    o_ref[0] = acc.astype(o_ref.dtype)


def expert_ffn(xe, w_gate, w_up, w_down, capacity, tf=1024):
    e, _, d = xe.shape
    f = w_gate.shape[2]
    tm = math.gcd(capacity, ROW_TILE)
    tf = math.gcd(f, tf)
    return pl.pallas_call(
        functools.partial(_expert_body, tf=tf),
        grid=(e, capacity // tm),
        in_specs=[
            pl.BlockSpec((1, tm, d), lambda ei, i: (ei, i, 0)),
            pl.BlockSpec((1, d, f), lambda ei, i: (ei, 0, 0)),
            pl.BlockSpec((1, d, f), lambda ei, i: (ei, 0, 0)),
            pl.BlockSpec((1, f, d), lambda ei, i: (ei, 0, 0)),
        ],
        out_specs=pl.BlockSpec((1, tm, d), lambda ei, i: (ei, i, 0)),
        out_shape=jax.ShapeDtypeStruct((e, capacity, d), BF16),
        compiler_params=_cparams("parallel", "parallel"),
        name="expert_ffn",
    )(xe, w_gate, w_up, w_down)


def _combine_body(base_ref, cnt_ref, slot_ref, aff_ref, x_ref, gate_ref, fg_ref, ye_ref, o_ref,
                  win, extra, acc, sems, xsem, *, n_exp, capacity, final_norm):
    t = pl.program_id(0)
    n_tiles = pl.num_programs(0)
    par = t % 2
    wc = win.shape[3]
    lanes = lax.broadcasted_iota(jnp.int32, (1, wc), 1)

    def start_of(tt, e, k):
        lo = _window_start(base_ref[tt * n_exp + e]) + k * wc
        return lo, pl.multiple_of(jnp.minimum(lo, capacity - wc), ROW_ALIGN)

    def fetch(tt, slot_par, e, k):
        _, st = start_of(tt, e, k)
        return pltpu.make_async_copy(ye_ref.at[e, pl.ds(st, wc)], win.at[k, slot_par, e], sems.at[k, slot_par, e])

    @pl.when(t == 0)
    def _():
        for e in range(n_exp):
            fetch(0, 0, e, 0).start()
            fetch(0, 0, e, 1).start()

    @pl.when(t + 1 < n_tiles)
    def _():
        for e in range(n_exp):
            fetch(t + 1, 1 - par, e, 0).start()
            fetch(t + 1, 1 - par, e, 1).start()

    def spread(e, lo, st):
        col = slot_ref[:, e:e + 1]
        return jnp.where((col - st == lanes) & (col >= lo), aff_ref[:, e:e + 1], 0.0).astype(BF16)

    src_e = lax.broadcasted_iota(jnp.int32, (LANES, n_exp * wc), 0)
    dst_e = lax.broadcasted_iota(jnp.int32, (LANES, n_exp * wc), 1) // wc
    expand = (src_e == dst_e).astype(BF16)
    s1 = slot_ref[...] + 1
    parts = jnp.concatenate([(s1 // 64).astype(BF16), (s1 % 64).astype(BF16)], axis=1)
    slot_rep = jnp.dot(parts, jnp.concatenate([expand * 64, expand], axis=0), preferred_element_type=F32) - 1.0
    aff_rep = jnp.dot(aff_ref[...].astype(BF16), expand, preferred_element_type=F32)
    lo_vec = jnp.concatenate([jnp.full((1, wc), start_of(t, e, 0)[0], jnp.int32) for e in range(n_exp)], axis=1)
    st_vec = jnp.concatenate([jnp.full((1, wc), start_of(t, e, 0)[1], jnp.int32) for e in range(n_exp)], axis=1)
    lane_in_win = lax.broadcasted_iota(jnp.int32, (1, n_exp * wc), 1) % wc
    hit = (slot_rep == (st_vec + lane_in_win).astype(F32)) & (slot_rep >= lo_vec.astype(F32))
    onehot = jnp.where(hit, aff_rep, 0.0).astype(BF16)
    for e in range(n_exp):
        fetch(t, par, e, 0).wait()
    acc[...] = jnp.dot(onehot, win[0, par].reshape(n_exp * wc, win.shape[4]), preferred_element_type=F32)
    for e in range(n_exp):
        s = base_ref[t * n_exp + e]
        n_win = (s - _window_start(s) + cnt_ref[t * n_exp + e] + wc - 1) // wc
        fetch(t, par, e, 1).wait()

        @pl.when(n_win > 1)
        def _(e=e):
            lo_1, st_1 = start_of(t, e, 1)
            acc[...] += jnp.dot(spread(e, lo_1, st_1), win[1, par, e], preferred_element_type=F32)

        def more(k, c, e=e):
            lo_k, st_k = start_of(t, e, k)
            cp = pltpu.make_async_copy(ye_ref.at[e, pl.ds(st_k, wc)], extra, xsem)
            cp.start()
            cp.wait()
            acc[...] += jnp.dot(spread(e, lo_k, st_k), extra[...], preferred_element_type=F32)
            return c

        lax.fori_loop(2, n_win, more, 0)

    x = x_ref[...] + gate_ref[0] * acc[...]
    if final_norm:
        x = x * lax.rsqrt(jnp.mean(x * x, axis=-1, keepdims=True) + EPS) * fg_ref[...]
    o_ref[...] = x


def combine_rows(slot, aff, x2, gate, final_g, ye, base, cnt, seq, n_exp, capacity, tm, final_norm):
    n, d = x2.shape
    n_tiles = n // tm
    wc = DISPATCH_WINDOW
    assert wc <= capacity <= 64 * 256 and capacity % ROW_ALIGN == 0 and seq % tm == 0
    grid_spec = pltpu.PrefetchScalarGridSpec(
        num_scalar_prefetch=2,
        grid=(n_tiles,),
        in_specs=[pl.BlockSpec((tm, LANES), lambda t, b, c: (t, 0)),
                  pl.BlockSpec((tm, LANES), lambda t, b, c: (t, 0)),
                  pl.BlockSpec((tm, d), lambda t, b, c: (t, 0)),
                  pl.BlockSpec((1, 1, d), lambda t, b, c: ((t * tm) // seq, 0, 0)),
                  pl.BlockSpec((1, d), lambda t, b, c: (0, 0)),
                  pl.BlockSpec(memory_space=pl.ANY)],
        out_specs=pl.BlockSpec((tm, d), lambda t, b, c: (t, 0)),
        scratch_shapes=[pltpu.VMEM((2, 2, n_exp, wc, d), BF16), pltpu.VMEM((wc, d), BF16), pltpu.VMEM((tm, d), F32),
                        pltpu.SemaphoreType.DMA((2, 2, n_exp)), pltpu.SemaphoreType.DMA(())],
    )
    return pl.pallas_call(
        functools.partial(_combine_body, n_exp=n_exp, capacity=capacity, final_norm=final_norm),
        grid_spec=grid_spec,
        out_shape=jax.ShapeDtypeStruct((n, d), F32),
        compiler_params=_cparams("arbitrary"),
        name="combine_rows",
    )(base, cnt, slot, aff, x2, gate, final_g, ye)


def expert_choice_moe(x2, seq, norm_g, sc, sh, gate2, final_g, w_router, w_gate, w_up, w_down, final_norm):
    n, d = x2.shape
    n_exp = w_router.shape[1]
    capacity = CAPACITY_FACTOR * n // n_exp
    tm = _row_tile(seq, ROUTE_TILE)
    h, aff, aff_t = norm_router(x2, seq, norm_g, sc, sh, w_router)
    thr, cgt = expert_thresholds(aff_t, capacity)
    pad = LANES - thr.shape[0]
    thr_l = jnp.pad(thr[:, 0], (0, pad))[None, :]
    need_l = jnp.pad(capacity - cgt[:, 0], (0, pad))[None, :]
    t_gt, t_eq = tile_counts(aff, thr_l, tm)
    eq_base = jnp.cumsum(t_eq, axis=0) - t_eq
    t_sel = t_gt + jnp.clip(need_l - eq_base, 0, t_eq)
    sel_base = jnp.cumsum(t_sel, axis=0) - t_sel
    slot, slot_t = token_slots(aff, thr_l, need_l, eq_base, sel_base, n_exp, tm)
    base = sel_base[:, :n_exp].reshape(-1)
    cnt = t_sel[:, :n_exp].reshape(-1)
    xe = dispatch_rows(h, slot_t, base, cnt, n_exp, capacity, tm)
    ye = expert_ffn(xe, w_gate, w_up, w_down, capacity)
    return combine_rows(slot, aff, x2, gate2, final_g, ye, base, cnt, seq, n_exp, capacity, tm, final_norm)


def _trunk(x, c, params):
    (norm_mix_g, norm_ffn_g, ada_w, ada_b, hg_w_in, hg_w_out, hg_norm_g, lb_table,
     da_w_qkv, da_w_out, s5_ops, s5_d, s5_w_glu, moe_w_router, moe_w_gate, moe_w_up, moe_w_down, final_g) = params
    bsz, seq, d = x.shape
    depth = norm_mix_g.shape[0]
    x2 = x.reshape(bsz * seq, d)
    cond = jax.nn.silu(c)
    for layer in range(depth):
        kind, slot = layer % N_MIXERS, layer // N_MIXERS
        mod = (jnp.dot(cond, ada_w[layer], precision=HIGHEST) + ada_b[layer])[:, None, :]
        sh1, sc1, g1, sh2, sc2, g2 = jnp.split(mod, 6, axis=-1)
        gmix = norm_mix_g[layer][None, :]
        if kind == 0:
            proj = nm_matmul(x2, seq, gmix, sc1, sh1, hg_w_in[slot])
            lb = lb_table[layer].reshape(HG_HEADS, 1, -1)
            m = hgrn_recurrence(proj.reshape(bsz, seq, -1), lb, jnp.log1p(-lb), 1.0 - lb,
                                hg_norm_g[slot][None, :].astype(F32))
            x2 = proj_residual(m.reshape(bsz * seq, d), hg_w_out[slot], x2, g1, seq)
        elif kind == 1:
            proj = nm_matmul(x2, seq, gmix, sc1, sh1, da_w_qkv[slot])
            m = dilated_attention(proj.reshape(bsz, seq, -1))
            x2 = proj_residual(m.reshape(bsz * seq, d), da_w_out[slot], x2, g1, seq)
        else:
            h = norm_mod(x2, seq, gmix, sc1, sh1)
            z = s5_mix(h.reshape(bsz, seq, d), s5_ops[slot], s5_d[slot])
            x2 = proj_residual(z.reshape(bsz * seq, d), s5_w_glu[slot], x2, g1, seq, glu=True)
        x2 = expert_choice_moe(x2, seq, norm_ffn_g[layer][None, :], sc2, sh2, g2, final_g[None, :].astype(F32),
                               moe_w_router[layer], moe_w_gate[layer], moe_w_up[layer], moe_w_down[layer],
                               final_norm=(layer == depth - 1))
    return x2.reshape(bsz, seq, d)


def kernel(x_prompt, x_sample, c_prompt, c_sample, norm_mix_g, norm_ffn_g, ada_w, ada_b, hg_w_in, hg_w_out, hg_norm_g, hg_lb_logits, da_w_qkv, da_w_out, s5_a_re, s5_a_im, s5_log_dt, s5_b_re, s5_b_im, s5_c_re, s5_c_im, s5_d, s5_w_glu, moe_w_router, moe_w_gate, moe_w_up, moe_w_down, final_g):
    lb_table = jnp.cumsum(jax.nn.softmax(hg_lb_logits.astype(F32), axis=0), axis=0)
    lb_table = lb_table - lb_table[0:1]
    s5_ops = [_s5_operators(s5_a_re[s], s5_a_im[s], s5_log_dt[s], s5_b_re[s], s5_b_im[s],
                            s5_c_re[s], s5_c_im[s]) for s in range(s5_a_re.shape[0])]
    bf = lambda w: w.astype(BF16)
    params = (norm_mix_g.astype(F32), norm_ffn_g.astype(F32), ada_w, ada_b, bf(hg_w_in), bf(hg_w_out),
              hg_norm_g, lb_table, bf(da_w_qkv), bf(da_w_out), s5_ops, s5_d, bf(s5_w_glu), moe_w_router,
              bf(moe_w_gate), bf(moe_w_up), bf(moe_w_down), final_g)
    return (_trunk(x_prompt, c_prompt, params), _trunk(x_sample, c_sample, params))
```

```python
import functools
import math

import jax
import jax.numpy as jnp
from jax import lax
from jax.experimental import pallas as pl
from jax.experimental.pallas import tpu as pltpu

F32 = jnp.float32
BF16 = jnp.bfloat16
HIGHEST = lax.Precision.HIGHEST

EPS = 1e-6
N_MIXERS = 3
HG_HEADS = 8
HG_CHUNK = 64
HG_SUPER = 4
HG_FINISH = 8
DA_PATTERNS = ((128, 1), (512, 4), (2048, 16))
DA_HEADS = 16
DA_QBLOCK = 128
DA_SEGMENT = 8
DA_UNROLL = 8
ROPE_THETA = 10000.0
S5_GROUP = 16
S5_STATE = 64
S5_CHUNK = 16
S5_SEGMENTS = 8
S5_ROWS = 256
CAPACITY_FACTOR = 2
EXPERT_ROWS = 16
ROUTE_TILE = 512
DISPATCH_WINDOW = 128
ROW_ALIGN = 16

LANES = 128
VMEM_LIMIT = 56 * 1024 * 1024
ROW_TILE = 1024


def _cparams(*sem):
    return pltpu.CompilerParams(dimension_semantics=sem, vmem_limit_bytes=VMEM_LIMIT)


def _row_tile(t, cap=ROW_TILE):
    return math.gcd(t, cap)


def _norm_mod(x, g, sc, sh):
    ms = jnp.mean(x * x, axis=-1, keepdims=True)
    return (x * lax.rsqrt(ms + EPS) * g) * (1.0 + sc) + sh


def _split2(x):
    hi = x.astype(BF16)
    return hi, (x - hi.astype(F32)).astype(BF16)


def _nm_matmul_body(x_ref, g_ref, sc_ref, sh_ref, w_ref, o_ref, h_scr):
    @pl.when(pl.program_id(1) == 0)
    def _():
        h_scr[...] = _norm_mod(x_ref[...], g_ref[...], sc_ref[0], sh_ref[0]).astype(BF16)

    o_ref[...] = jnp.dot(h_scr[...], w_ref[...], preferred_element_type=F32).astype(o_ref.dtype)


def nm_matmul(x2, seq, g, sc, sh, w, tn=1024):
    n, d = x2.shape
    f = w.shape[1]
    tm = _row_tile(seq)
    tn = math.gcd(f, tn)
    return pl.pallas_call(
        _nm_matmul_body,
        grid=(n // tm, f // tn),
        in_specs=[
            pl.BlockSpec((tm, d), lambda i, j: (i, 0)),
            pl.BlockSpec((1, d), lambda i, j: (0, 0)),
            pl.BlockSpec((1, 1, d), lambda i, j: ((i * tm) // seq, 0, 0)),
            pl.BlockSpec((1, 1, d), lambda i, j: ((i * tm) // seq, 0, 0)),
            pl.BlockSpec((d, tn), lambda i, j: (0, j)),
        ],
        out_specs=pl.BlockSpec((tm, tn), lambda i, j: (i, j)),
        out_shape=jax.ShapeDtypeStruct((n, f), BF16),
        scratch_shapes=[pltpu.VMEM((tm, d), BF16)],
        compiler_params=_cparams("parallel", "arbitrary"),
        name="nm_matmul",
    )(x2, g, sc, sh, w)


def _norm_mod_body(x_ref, g_ref, sc_ref, sh_ref, o_ref):
    o_ref[...] = _norm_mod(x_ref[...], g_ref[...], sc_ref[0], sh_ref[0])


def norm_mod(x2, seq, g, sc, sh):
    n, d = x2.shape
    tm = _row_tile(seq)
    return pl.pallas_call(
        _norm_mod_body,
        grid=(n // tm,),
        in_specs=[
            pl.BlockSpec((tm, d), lambda i: (i, 0)),
            pl.BlockSpec((1, d), lambda i: (0, 0)),
            pl.BlockSpec((1, 1, d), lambda i: ((i * tm) // seq, 0, 0)),
            pl.BlockSpec((1, 1, d), lambda i: ((i * tm) // seq, 0, 0)),
        ],
        out_specs=pl.BlockSpec((tm, d), lambda i: (i, 0)),
        out_shape=jax.ShapeDtypeStruct((n, d), F32),
        compiler_params=_cparams("parallel"),
        name="norm_mod",
    )(x2, g, sc, sh)


def _norm_router_body(x_ref, g_ref, sc_ref, sh_ref, wh_ref, wl_ref, h_ref, aff_ref, afft_ref, *, n_exp):
    h = _norm_mod(x_ref[...], g_ref[...], sc_ref[0], sh_ref[0])
    h_hi, h_lo = _split2(h)
    h_ref[...] = h_hi
    logits = (jnp.dot(h_hi, wh_ref[...], preferred_element_type=F32)
              + jnp.dot(h_lo, wh_ref[...], preferred_element_type=F32)
              + jnp.dot(h_hi, wl_ref[...], preferred_element_type=F32))
    lane = lax.broadcasted_iota(jnp.int32, (1, LANES), 1)
    logits = jnp.where(lane < n_exp, logits, -jnp.inf)
    ex = jnp.exp(logits - jnp.max(logits, axis=-1, keepdims=True))
    aff = ex / jnp.sum(ex, axis=-1, keepdims=True)
    aff_ref[...] = aff
    afft_ref[...] = aff.T[:EXPERT_ROWS, :]


def norm_router(x2, seq, g, sc, sh, w_router):
    n, d = x2.shape
    n_exp = w_router.shape[1]
    assert n_exp <= EXPERT_ROWS
    wr = jnp.zeros((d, LANES), F32).at[:, :n_exp].set(w_router.astype(F32))
    w_hi = wr.astype(BF16)
    w_lo = (wr - w_hi.astype(F32)).astype(BF16)
    tm = _row_tile(seq, ROUTE_TILE)
    return pl.pallas_call(
        functools.partial(_norm_router_body, n_exp=n_exp),
        grid=(n // tm,),
        in_specs=[
            pl.BlockSpec((tm, d), lambda i: (i, 0)),
            pl.BlockSpec((1, d), lambda i: (0, 0)),
            pl.BlockSpec((1, 1, d), lambda i: ((i * tm) // seq, 0, 0)),
            pl.BlockSpec((1, 1, d), lambda i: ((i * tm) // seq, 0, 0)),
            pl.BlockSpec((d, LANES), lambda i: (0, 0)),
            pl.BlockSpec((d, LANES), lambda i: (0, 0)),
        ],
        out_specs=[pl.BlockSpec((tm, d), lambda i: (i, 0)),
                   pl.BlockSpec((tm, LANES), lambda i: (i, 0)),
                   pl.BlockSpec((EXPERT_ROWS, tm), lambda i: (0, i))],
        out_shape=[jax.ShapeDtypeStruct((n, d), BF16), jax.ShapeDtypeStruct((n, LANES), F32),
                   jax.ShapeDtypeStruct((EXPERT_ROWS, n), F32)],
        compiler_params=_cparams("parallel"),
        name="norm_router",
    )(x2, g, sc, sh, w_hi, w_lo)


def _proj_res_body(m_ref, w_ref, x_ref, gate_ref, o_ref):
    y = jnp.dot(m_ref[...].astype(BF16), w_ref[...], preferred_element_type=F32)
    o_ref[...] = x_ref[...] + gate_ref[0] * y


def _glu_res_body(m_ref, w_ref, x_ref, gate_ref, o_ref):
    d = x_ref.shape[-1]
    y = jnp.dot(m_ref[...].astype(BF16), w_ref[...], preferred_element_type=F32)
    o_ref[...] = x_ref[...] + gate_ref[0] * (y[:, :d] * jax.nn.sigmoid(y[:, d:]))


def proj_residual(m2, w, x2, gate, seq, glu=False):
    n, d = x2.shape
    f = w.shape[1]
    tm = _row_tile(seq, 512)
    return pl.pallas_call(
        _glu_res_body if glu else _proj_res_body,
        grid=(n // tm,),
        in_specs=[
            pl.BlockSpec((tm, d), lambda i: (i, 0)),
            pl.BlockSpec((d, f), lambda i: (0, 0)),
            pl.BlockSpec((tm, d), lambda i: (i, 0)),
            pl.BlockSpec((1, 1, d), lambda i: ((i * tm) // seq, 0, 0)),
        ],
        out_specs=pl.BlockSpec((tm, d), lambda i: (i, 0)),
        out_shape=jax.ShapeDtypeStruct((n, d), F32),
        compiler_params=_cparams("parallel"),
        name="glu_residual" if glu else "proj_residual",
    )(m2, w, x2, gate)


def _dot_nt(a, b):
    return lax.dot_general(a, b, (((1,), (1,)), ((), ())), preferred_element_type=F32)


def _hgrn_gates(z, lb, l1, om):
    e = jnp.exp(-jnp.abs(z))
    r = 1.0 / (1.0 + e)
    pos = z >= 0.0
    k = om * (jnp.where(pos, e, 1.0) * r)
    f = lb + om * (jnp.where(pos, 1.0, e) * r)
    log_sig = jnp.minimum(z, 0.0) + jnp.log(r)
    return jnp.maximum(jnp.log(f), l1 + log_sig), k


def _bcast_rows(x, rows, c, n_sub):
    return jnp.concatenate(
        [jnp.broadcast_to(x[rows[j]:rows[j] + 1, :], (c, x.shape[1])) for j in range(n_sub)], axis=0)


def _hgrn_intra(q, v, z, lb, l1, om, cum_mat, causal, ref_row, last_row, c, n_sub):
    logf, k = _hgrn_gates(z, lb, l1, om)
    l_hi, l_lo = _split2(logf)
    b = (jnp.dot(cum_mat, l_hi, preferred_element_type=F32)
         + jnp.dot(cum_mat, l_lo, preferred_element_type=F32))
    b_ref = _bcast_rows(b, [j * c + ref_row for j in range(n_sub)], c, n_sub)
    b_last = _bcast_rows(b, [j * c + last_row for j in range(n_sub)], c, n_sub)
    up, down = jnp.exp(b - b_ref), jnp.exp(b_ref - b)
    qd = (q * up).astype(BF16)
    kd = (k * down).astype(BF16)
    s = jnp.where(causal, _dot_nt(qd, kd), 0.0).astype(BF16)
    o = jnp.dot(s, v.astype(BF16), preferred_element_type=F32)
    qe = (q * (up * jnp.exp(b_ref))).astype(BF16)
    kl = (k * (down * jnp.exp(b_last - b_ref))).astype(BF16)
    kvs, decs = [], []
    for j in range(n_sub):
        r = slice(j * c, (j + 1) * c)
        kvs.append(jnp.dot(v[r].T.astype(BF16), kl[r], preferred_element_type=F32))
        decs.append(jnp.exp(b[j * c + last_row:j * c + last_row + 1, :]))
    return o, qe, kvs, decs


def _hgrn_body(q_ref, v_ref, zf_ref, zb_ref, g_ref, lb_ref, l1_ref, om_ref, ng_ref, o_ref,
               oi_scr, qe_scr, kv_scr, dec_scr, st_scr):
    seq = q_ref.shape[1]
    c = min(HG_CHUNK, seq)
    n_chunks = seq // c
    n_sub = math.gcd(n_chunks, HG_SUPER)
    sc = n_sub * c
    lb, l1, om = lb_ref[0], l1_ref[0], om_ref[0]
    row = lax.broadcasted_iota(jnp.int32, (sc, sc), 0)
    col = lax.broadcasted_iota(jnp.int32, (sc, sc), 1)
    same = (row // c) == (col // c)
    lower, upper = same & (row >= col), same & (row <= col)
    tril, triu = lower.astype(BF16), upper.astype(BF16)
    mid = c // 2

    def intra(i, carry):
        r = pl.ds(pl.multiple_of(i * sc, sc), sc)
        q, v = q_ref[0, r, :].astype(F32), v_ref[0, r, :].astype(F32)
        of, qf, kvf, decf = _hgrn_intra(q, v, zf_ref[0, r, :].astype(F32), lb, l1, om, tril, lower, mid - 1, c - 1, c, n_sub)
        ob, qb, kvb, decb = _hgrn_intra(q, v, zb_ref[0, r, :].astype(F32), lb, l1, om, triu, upper, c - mid, 0, c, n_sub)
        oi_scr[r, :] = of + ob
        qe_scr[r, 0:LANES] = qf
        qe_scr[r, LANES:2 * LANES] = qb
        for j in range(n_sub):
            n = i * n_sub + j
            kv_scr[0, n], kv_scr[1, n] = kvf[j], kvb[j]
            dec_scr[0, n], dec_scr[1, n] = jnp.broadcast_to(decf[j], (8, LANES)), jnp.broadcast_to(decb[j], (8, LANES))
        return carry

    lax.fori_loop(0, n_chunks // n_sub, intra, 0, unroll=2)

    def carry_state(n, carry):
        sf, sb = carry
        nb = n_chunks - 1 - n
        st_scr[n, :, 0:LANES] = sf.astype(BF16)
        st_scr[nb, :, LANES:2 * LANES] = sb.astype(BF16)
        sf = sf * dec_scr[0, n, 0:1, :] + kv_scr[0, n]
        sb = sb * dec_scr[1, nb, 0:1, :] + kv_scr[1, nb]
        return sf, sb

    zero = jnp.zeros((LANES, LANES), F32)
    lax.fori_loop(0, n_chunks, carry_state, (zero, zero))

    n_fin = math.gcd(n_chunks, HG_FINISH)
    fc = n_fin * c

    def finish(i, carry):
        parts = []
        for j in range(n_fin):
            n = i * n_fin + j
            parts.append(_dot_nt(qe_scr[pl.ds(pl.multiple_of(n * c, c), c), :], st_scr[n]))
        r = pl.ds(pl.multiple_of(i * fc, fc), fc)
        o = oi_scr[r, :] + jnp.concatenate(parts, axis=0)
        o = o * lax.rsqrt(jnp.mean(o * o, axis=-1, keepdims=True) + EPS) * ng_ref[...]
        o_ref[0, r, :] = (o * jax.nn.sigmoid(g_ref[0, r, :].astype(F32))).astype(o_ref.dtype)
        return carry

    lax.fori_loop(0, n_chunks // n_fin, finish, 0)


def hgrn_recurrence(proj, lb, l1, om, norm_g):
    bsz, seq, d5 = proj.shape
    d = d5 // 5
    h = HG_HEADS
    dk = d // h
    assert dk == LANES
    n_chunks = seq // min(HG_CHUNK, seq)

    def col(section):
        return pl.BlockSpec((1, seq, dk), lambda b, hh: (b, 0, section * h + hh))

    par = pl.BlockSpec((1, 1, dk), lambda b, hh: (hh, 0, 0))
    return pl.pallas_call(
        _hgrn_body,
        grid=(bsz, h),
        in_specs=[col(0), col(1), col(2), col(3), col(4), par, par, par,
                  pl.BlockSpec((1, dk), lambda b, hh: (0, 0))],
        out_specs=pl.BlockSpec((1, seq, dk), lambda b, hh: (b, 0, hh)),
        out_shape=jax.ShapeDtypeStruct((bsz, seq, d), BF16),
        scratch_shapes=[pltpu.VMEM((seq, dk), F32), pltpu.VMEM((seq, 2 * dk), BF16),
                        pltpu.VMEM((2, n_chunks, dk, dk), F32), pltpu.VMEM((2, n_chunks, 8, dk), F32),
                        pltpu.VMEM((n_chunks, dk, 2 * dk), BF16)],
        compiler_params=_cparams("parallel", "parallel"),
        name="hgrn_recurrence",
    )(proj, proj, proj, proj, proj, lb, l1, om, norm_g)


def _rope_pair(x, cos, sin_signed, swap):
    partner = jnp.dot(x.astype(BF16), swap, preferred_element_type=F32)
    return x * cos + partner * sin_signed


def _attn_group(q_ref, k_ref, v_ref, cos_ref, sin_ref, qs, ks, vs, ot, mt, lt, s_scr, p_scr, wide, dil, half):
    seq = q_ref.shape[1]
    length = seq // dil
    dh = LANES // 2
    lane = lax.broadcasted_iota(jnp.int32, (1, LANES), 1)
    src_lane = lax.broadcasted_iota(jnp.int32, (LANES, LANES), 0)
    dst_lane = lax.broadcasted_iota(jnp.int32, (LANES, LANES), 1)
    quarter = dh // 2
    swap = (src_lane == jnp.where((dst_lane % dh) < quarter, dst_lane + quarter, dst_lane - quarter)).astype(BF16)
    head0 = lane < dh
    tile = math.gcd(length, 256)
    if dil > 1:
        for j, ref in enumerate((q_ref, k_ref, v_ref)):
            for t0 in range(0, seq, 512):
                rows = pl.ds(t0, min(512, seq))
                wide[j, rows, :] = ref[0, rows, :].astype(F32)
    for r in range(dil):
        for t0 in range(0, length, tile):
            dst = pl.ds(r * length + t0, tile)
            if dil == 1:
                q, k, v = (ref[0, dst, :].astype(F32) for ref in (q_ref, k_ref, v_ref))
            else:
                src = pl.ds(r + t0 * dil, tile, stride=dil)
                q, k, v = wide[0, src, :], wide[1, src, :], wide[2, src, :]
            cos, sin = cos_ref[0, dst, :], sin_ref[0, dst, :]
            qs[dst, :] = (_rope_pair(q, cos, sin, swap) * (dh ** -0.5)).astype(BF16)
            ks[dst, :] = _rope_pair(k, cos, sin, swap).astype(BF16)
            vs[dst, :] = v.astype(BF16)

    qb = min(DA_QBLOCK, length)
    span = min(qb + 2 * half, length)
    n_blocks = seq // qb
    seg = math.gcd(n_blocks, DA_SEGMENT)
    delta = lax.broadcasted_iota(jnp.int32, (qb, span), 1) - lax.broadcasted_iota(jnp.int32, (qb, span), 0)

    def place(bi):
        row0 = pl.multiple_of(bi * qb, qb)
        r = row0 // length
        m0 = row0 - r * length
        k0 = jnp.clip(m0 - half, 0, length - span)
        return pl.ds(row0, qb), pl.ds(pl.multiple_of(r * length + k0, 16), span), k0 - m0

    def segment(si, carry):
        def scores(j, c):
            rows, krows, off = place(si * seg + j)
            q, kk = qs[rows, :], ks[krows, :]
            bias = jnp.where((delta >= -half - off) & (delta <= half - off), 0.0, -jnp.inf)
            zero = jnp.zeros_like(q)
            s_scr[j, 0, 0:qb, 0:span] = _dot_nt(jnp.where(head0, q, zero), kk) + bias
            s_scr[j, 1, 0:qb, 0:span] = _dot_nt(jnp.where(head0, zero, q), kk) + bias
            return c

        def softmax(j, c):
            rows, _, _ = place(si * seg + j)
            ms = []
            for h in range(2):
                s = s_scr[j, h, 0:qb, 0:span]
                m = jnp.max(s, axis=-1, keepdims=True)
                p_scr[j, h, 0:qb, 0:span] = jnp.exp(s - m).astype(BF16)
                ms.append(jnp.broadcast_to(m, (qb, LANES)))
            mt[rows, :] = jnp.where(head0, ms[0], ms[1])
            return c

        def values(j, c):
            rows, krows, _ = place(si * seg + j)
            vv = vs[krows, :]
            one = jnp.ones_like(vv)
            r0 = jnp.dot(p_scr[j, 0, 0:qb, 0:span], jnp.where(head0, vv, one), preferred_element_type=F32)
            r1 = jnp.dot(p_scr[j, 1, 0:qb, 0:span], jnp.where(head0, one, vv), preferred_element_type=F32)
            ot[rows, :] = jnp.where(head0, r0, r1)
            lt[rows, :] = pltpu.roll(jnp.where(head0, r1, r0), dh, 1)
            return c

        lax.fori_loop(0, seg, scores, 0, unroll=DA_UNROLL)
        lax.fori_loop(0, seg, softmax, 0, unroll=DA_UNROLL)
        lax.fori_loop(0, seg, values, 0, unroll=DA_UNROLL)
        return carry

    lax.fori_loop(0, n_blocks // seg, segment, 0)


def _attn_body(q_ref, k_ref, v_ref, cos_ref, sin_ref, o_ref, qs, ks, vs, ot, mt, lt, acc, mrun, lrun, s_scr, p_scr, wide):
    g = pl.program_id(2)
    seq = q_ref.shape[1]
    sets = ((acc, mrun, lrun), (ot, mt, lt))
    for gi, (window, dil) in enumerate(DA_PATTERNS):
        half = window // (2 * dil)
        assert half % 16 == 0
        length = seq // dil
        tile = math.gcd(length, 256)
        fresh, old = sets[gi % 2], sets[(gi + 1) % 2]
        prev_dil = DA_PATTERNS[gi - 1][1] if gi else 1
        assert dil % prev_dil == 0

        @pl.when(g == gi)
        def _(dil=dil, half=half, gi=gi, length=length, tile=tile, fresh=fresh, old=old, prev_dil=prev_dil):
            _attn_group(q_ref, k_ref, v_ref, cos_ref, sin_ref, qs, ks, vs, *fresh, s_scr, p_scr, wide, dil, half)
            if gi == 0:
                return
            step = dil // prev_dil
            prev_length = seq // prev_dil
            for r in range(dil):
                for t0 in range(0, length, tile):
                    dst = pl.ds(r * length + t0, tile)
                    first = (r % prev_dil) * prev_length + r // prev_dil + t0 * step
                    src = pl.ds(first, tile, stride=step) if step > 1 else pl.ds(first, tile)
                    m_old, m_new = old[1][src, :], fresh[1][dst, :]
                    m_all = jnp.maximum(m_old, m_new)
                    w_old, w_new = jnp.exp(m_old - m_all), jnp.exp(m_new - m_all)
                    fresh[0][dst, :] = old[0][src, :] * w_old + fresh[0][dst, :] * w_new
                    fresh[2][dst, :] = old[2][src, :] * w_old + fresh[2][dst, :] * w_new
                    fresh[1][dst, :] = m_all

    last = len(DA_PATTERNS) - 1

    @pl.when(g == last)
    def _():
        run = sets[last % 2]
        dil = DA_PATTERNS[last][1]
        length = seq // dil
        tile = math.gcd(length, 256)
        for r in range(dil):
            for t0 in range(0, length, tile):
                src = pl.ds(r * length + t0, tile)
                dst = pl.ds(r + t0 * dil, tile, stride=dil) if dil > 1 else src
                o_ref[0, dst, :] = run[0][src, :] / run[2][src, :]


def _rope_tables(seq):
    dh = LANES // 2
    halfd = dh // 2
    inv = ROPE_THETA ** (-jnp.arange(halfd, dtype=F32) / halfd)
    cos_t, sin_t = [], []
    for _, dil in DA_PATTERNS:
        pos = jnp.arange(seq, dtype=F32).reshape(seq // dil, dil).T.reshape(seq)
        ang = pos[:, None] * inv[None, :]
        cos, sin = jnp.cos(ang), jnp.sin(ang)
        cos_t.append(jnp.tile(cos, (1, 4)))
        sin_t.append(jnp.concatenate([-sin, sin, -sin, sin], axis=1))
    return jnp.stack(cos_t), jnp.stack(sin_t)


def dilated_attention(proj):
    bsz, seq, d9 = proj.shape
    n_groups = len(DA_PATTERNS)
    d = d9 // (3 * n_groups)
    assert d // DA_HEADS == LANES // 2
    pairs = d // LANES
    cos, sin = _rope_tables(seq)
    qb_max = max(min(DA_QBLOCK, seq // dl) for _, dl in DA_PATTERNS)
    span_max = max(min(min(DA_QBLOCK, seq // dl) + 2 * (w // (2 * dl)), seq // dl) for w, dl in DA_PATTERNS)

    def col(part):
        return pl.BlockSpec((1, seq, LANES), lambda b, hp, g: (b, 0, (g * 3 + part) * pairs + hp))

    table = pl.BlockSpec((1, seq, LANES), lambda b, hp, g: (g, 0, 0))
    return pl.pallas_call(
        _attn_body,
        grid=(bsz, pairs, n_groups),
        in_specs=[col(0), col(1), col(2), table, table],
        out_specs=pl.BlockSpec((1, seq, LANES), lambda b, hp, g: (b, 0, hp)),
        out_shape=jax.ShapeDtypeStruct((bsz, seq, d), F32),
        scratch_shapes=[pltpu.VMEM((seq, LANES), BF16)] * 3 + [pltpu.VMEM((seq, LANES), F32)] * 6
        + [pltpu.VMEM((DA_SEGMENT, 2, qb_max, span_max), F32),
           pltpu.VMEM((DA_SEGMENT, 2, qb_max, span_max), BF16), pltpu.VMEM((3, seq, LANES), F32)],
        compiler_params=_cparams("parallel", "parallel", "arbitrary"),
        name="dilated_attention",
    )(proj, proj, proj, cos, sin)


def _s5_operators(a_re, a_im, log_dt, b_re, b_im, c_re, c_im):
    lc, i_dim, p_dim = S5_CHUNK, S5_GROUP, S5_STATE
    n_groups = a_re.shape[1]
    hp = dict(precision=HIGHEST)
    a = lax.complex(a_re.astype(F32), a_im.astype(F32))
    lam = a * jnp.exp(log_dt.astype(F32))[..., None]
    a_bar = jnp.exp(lam)
    bmat = lax.complex(b_re.astype(F32), b_im.astype(F32))
    cmat = lax.complex(c_re.astype(F32), c_im.astype(F32))
    b_bar = ((a_bar - 1.0) / a)[..., None] * bmat[None]
    tau = jnp.arange(lc + 1, dtype=F32)
    apow = jnp.exp(lam[:, :, None, :] * tau[None, None, :, None])
    kern = jnp.real(jnp.einsum('gip,dgtp,dgpj->dgtij', cmat, apow[:, :, :lc], b_bar, **hp))
    s_idx = jnp.arange(lc)[:, None]
    t_idx = jnp.arange(lc)[None, :]
    lag = t_idx - s_idx
    m_f = jnp.where((lag >= 0)[None, :, :, None, None], kern[0][:, jnp.clip(lag, 0)], 0.0)
    m_b = jnp.where((lag <= 0)[None, :, :, None, None], kern[1][:, jnp.clip(-lag, 0)], 0.0)
    m_op = (m_f + m_b).transpose(0, 1, 4, 2, 3).reshape(n_groups, lc * i_dim, lc * i_dim)
    rev = jnp.arange(lc - 1, -1, -1)
    p_f = apow[0][:, rev][:, :, :, None] * b_bar[0][:, None]
    p_b = apow[1][:, :lc][:, :, :, None] * b_bar[1][:, None]
    p_op = jnp.stack([jnp.real(p_f), jnp.imag(p_f), jnp.real(p_b), jnp.imag(p_b)], axis=0)
    p_op = p_op.transpose(1, 2, 4, 0, 3).reshape(n_groups, lc * i_dim, 4, p_dim)
    q_f = cmat[:, None] * apow[0][:, 1:lc + 1][:, :, None, :]
    q_b = cmat[:, None] * apow[1][:, lc - jnp.arange(lc)][:, :, None, :]
    q_op = jnp.stack([jnp.real(q_f), -jnp.imag(q_f), jnp.real(q_b), -jnp.imag(q_b)], axis=0)
    q_op = q_op.transpose(1, 0, 4, 2, 3).reshape(n_groups, 4, p_dim, lc * i_dim)
    n_pairs = n_groups // 2
    eye = jnp.eye(2, dtype=F32)
    w = lc * i_dim
    m_pair = m_op.reshape(n_pairs, 2, w, w)
    p_pair = jnp.einsum('narqp,ab->narqbp', p_op.reshape(n_pairs, 2, w, 4, p_dim), eye)
    p_pair = p_pair.reshape(n_pairs, 2 * w, 8 * p_dim)
    q_pair = jnp.einsum('nbqpc,ab->nqbpac', q_op.reshape(n_pairs, 2, 4, p_dim, w), eye)
    q_pair = q_pair.reshape(n_pairs, 8 * p_dim, 2 * w)
    lam_chunk = (lam * lc).reshape(2, n_groups // 8, 8 * p_dim)
    tiles = n_groups // 8
    return (m_pair.astype(BF16).reshape(tiles, 4, 2, w, w), p_pair.astype(BF16).reshape(tiles, 4, 2 * w, 8 * p_dim),
            q_pair.astype(BF16).reshape(tiles, 4, 8 * p_dim, 2 * w), lam_chunk)


def _cmul(ar, ai, xr, xi):
    return ar * xr - ai * xi, ar * xi + ai * xr


def _s5_body(h_ref, perm_ref, m_ref, p_ref, q_ref, pw_ref, d_ref, o_ref, u_scr, uc_scr, v_scr, s_scr, yc_scr):
    n_seq, seq = h_ref.shape[0], h_ref.shape[1]
    lc = S5_CHUNK
    n_chunks = seq // lc
    ns = n_chunks // S5_SEGMENTS
    n_seg = n_seq * S5_SEGMENTS
    n_pairs = m_ref.shape[1]
    pw = p_ref.shape[2]
    sw = pw // 4
    qw = n_pairs * sw

    for s in range(n_seg):
        b, sg = divmod(s, S5_SEGMENTS)
        for t in range(lc):
            u_scr[t, pl.ds(s, ns, stride=n_seg), :] = h_ref[b, pl.ds(lc * sg * ns + t, ns, stride=lc), :]
    u_all = jnp.concatenate([u_scr[t].astype(BF16) for t in range(lc)], axis=1)
    uc_scr[...] = jnp.dot(u_all, perm_ref[...], preferred_element_type=F32).astype(BF16)
    for p in range(n_pairs):
        vp = jnp.dot(uc_scr[:, p * pw:(p + 1) * pw], p_ref[0, p], preferred_element_type=F32)
        for c in range(4):
            v_scr[:, c * qw + p * sw:c * qw + (p + 1) * sw] = vp[:, c * sw:(c + 1) * sw]

    a_f = (pw_ref[0, 0, 1:2, :], pw_ref[0, 1, 1:2, :])
    a_b = (pw_ref[0, 2, 1:2, :], pw_ref[0, 3, 1:2, :])
    zero = jnp.zeros((n_seg, qw), F32)

    def scan(k, carry):
        f_re, f_im, b_re, b_im = carry
        rf = pl.ds(pl.multiple_of(k * n_seg, n_seg), n_seg)
        s_scr[rf, 0:qw] = f_re
        s_scr[rf, qw:2 * qw] = f_im
        n_re, n_im = _cmul(*a_f, f_re, f_im)
        rb = pl.ds(pl.multiple_of((ns - 1 - k) * n_seg, n_seg), n_seg)
        s_scr[rb, 2 * qw:3 * qw] = b_re
        s_scr[rb, 3 * qw:4 * qw] = b_im
        m_re, m_im = _cmul(*a_b, b_re, b_im)
        return (n_re + v_scr[rf, 0:qw], n_im + v_scr[rf, qw:2 * qw],
                m_re + v_scr[rb, 2 * qw:3 * qw], m_im + v_scr[rb, 3 * qw:4 * qw])

    f_re, f_im, b_re, b_im = lax.fori_loop(0, ns, scan, (zero, zero, zero, zero))

    a_seg_f = (pw_ref[0, 0, ns:ns + 1, :], pw_ref[0, 1, ns:ns + 1, :])
    a_seg_b = (pw_ref[0, 2, ns:ns + 1, :], pw_ref[0, 3, ns:ns + 1, :])
    row0 = jnp.zeros((1, qw), F32)
    cf = []
    for s in range(n_seg):
        if s % S5_SEGMENTS == 0:
            cf.append((row0, row0))
        else:
            xr, xi = _cmul(*a_seg_f, *cf[-1])
            cf.append((xr + f_re[s - 1:s, :], xi + f_im[s - 1:s, :]))
    cb = []
    for s in range(n_seg - 1, -1, -1):
        if s % S5_SEGMENTS == S5_SEGMENTS - 1:
            cb.insert(0, (row0, row0))
        else:
            xr, xi = _cmul(*a_seg_b, *cb[0])
            cb.insert(0, (xr + b_re[s + 1:s + 2, :], xi + b_im[s + 1:s + 2, :]))
    cf_re, cf_im = (jnp.concatenate([c[j] for c in cf], axis=0) for j in range(2))
    cb_re, cb_im = (jnp.concatenate([c[j] for c in cb], axis=0) for j in range(2))

    def correct(k, carry):
        rf = pl.ds(pl.multiple_of(k * n_seg, n_seg), n_seg)
        xr, xi = _cmul(pw_ref[0, 0, pl.ds(k, 1), :], pw_ref[0, 1, pl.ds(k, 1), :], cf_re, cf_im)
        s_scr[rf, 0:qw] += xr
        s_scr[rf, qw:2 * qw] += xi
        kb = ns - 1 - k
        yr, yi = _cmul(pw_ref[0, 2, pl.ds(kb, 1), :], pw_ref[0, 3, pl.ds(kb, 1), :], cb_re, cb_im)
        s_scr[rf, 2 * qw:3 * qw] += yr
        s_scr[rf, 3 * qw:4 * qw] += yi
        return carry

    lax.fori_loop(0, ns, correct, 0)

    half = pw // 2
    for p in range(n_pairs):
        uc = uc_scr[:, p * pw:(p + 1) * pw]
        st = jnp.concatenate([s_scr[:, c * qw + p * sw:c * qw + (p + 1) * sw] for c in range(4)], axis=1)
        y = jnp.dot(st.astype(BF16), q_ref[0, p], preferred_element_type=F32)
        y = y + jnp.concatenate([jnp.dot(uc[:, :half], m_ref[0, p, 0], preferred_element_type=F32),
                                 jnp.dot(uc[:, half:], m_ref[0, p, 1], preferred_element_type=F32)], axis=1)
        yc_scr[:, p * pw:(p + 1) * pw] = y.astype(BF16)
    y_all = _dot_nt(yc_scr[...], perm_ref[...])
    for t in range(lc):
        u_scr[t] = y_all[:, t * LANES:(t + 1) * LANES]
    for s in range(n_seg):
        b, sg = divmod(s, S5_SEGMENTS)
        for t in range(lc):
            rows = pl.ds(lc * sg * ns + t, ns, stride=lc)
            y = u_scr[t, pl.ds(s, ns, stride=n_seg), :] + d_ref[...] * h_ref[b, rows, :]
            o_ref[b, rows, :] = jax.nn.gelu(y)


def s5_mix(h3, ops, d_skip):
    m_op, p_op, q_op, lam_chunk = ops
    bsz, seq, d = h3.shape
    lc = S5_CHUNK
    tiles = d // LANES
    n_chunks = seq // lc
    width = lc * LANES
    assert seq % (lc * S5_SEGMENTS) == 0 and m_op.shape[0] == tiles
    ns = n_chunks // S5_SEGMENTS
    nb = math.gcd(bsz, max(1, S5_ROWS // n_chunks))
    rows = nb * n_chunks
    powers = jnp.exp(lam_chunk[:, :, None, :] * jnp.arange(ns + 1, dtype=F32)[None, None, :, None])
    pw_tab = jnp.stack([jnp.real(powers[0]), jnp.imag(powers[0]), jnp.real(powers[1]), jnp.imag(powers[1])], axis=1)
    src = jnp.arange(width)
    t_idx, g_idx, i_idx = src // LANES, (src % LANES) // S5_GROUP, src % S5_GROUP
    perm = (jnp.arange(width)[None, :] == (g_idx * (lc * S5_GROUP) + t_idx * S5_GROUP + i_idx)[:, None]).astype(BF16)
    return pl.pallas_call(
        _s5_body,
        grid=(tiles, bsz // nb),
        in_specs=[
            pl.BlockSpec((nb, seq, LANES), lambda l, b: (b, 0, l)),
            pl.BlockSpec((width, width), lambda l, b: (0, 0)),
            pl.BlockSpec((1,) + m_op.shape[1:], lambda l, b: (l, 0, 0, 0, 0)),
            pl.BlockSpec((1,) + p_op.shape[1:], lambda l, b: (l, 0, 0, 0)),
            pl.BlockSpec((1,) + q_op.shape[1:], lambda l, b: (l, 0, 0, 0)),
            pl.BlockSpec((1,) + pw_tab.shape[1:], lambda l, b: (l, 0, 0, 0)),
            pl.BlockSpec((1, LANES), lambda l, b: (0, l)),
        ],
        out_specs=pl.BlockSpec((nb, seq, LANES), lambda l, b: (b, 0, l)),
        out_shape=jax.ShapeDtypeStruct((bsz, seq, d), F32),
        scratch_shapes=[pltpu.VMEM((lc, rows, LANES), F32), pltpu.VMEM((rows, width), BF16),
                        pltpu.VMEM((rows, width), F32), pltpu.VMEM((rows, width), F32),
                        pltpu.VMEM((rows, width), BF16)],
        compiler_params=_cparams("parallel", "parallel"),
        name="s5_mix",
    )(h3, perm, m_op, p_op, q_op, pw_tab, d_skip.astype(F32)[None, :])


def _threshold_body(a_ref, thr_ref, cgt_ref, *, capacity):
    def bits():
        return lax.bitcast_convert_type(a_ref[...], jnp.int32)

    def step(i, thr):
        cand = thr | jnp.left_shift(jnp.int32(1), 30 - i)
        cnt = jnp.sum((bits() >= cand).astype(jnp.int32), axis=1, keepdims=True)
        return jnp.where(cnt >= capacity, cand, thr)

    thr = lax.fori_loop(0, 31, step, jnp.zeros((a_ref.shape[0], 1), jnp.int32))
    cgt = jnp.sum((bits() > thr).astype(jnp.int32), axis=1, keepdims=True)
    thr_ref[...] = jnp.broadcast_to(lax.bitcast_convert_type(thr, F32), thr_ref.shape)
    cgt_ref[...] = jnp.broadcast_to(cgt, cgt_ref.shape)


def expert_thresholds(aff_t, capacity):
    rows, n = aff_t.shape
    return pl.pallas_call(
        functools.partial(_threshold_body, capacity=capacity),
        grid=(1,),
        in_specs=[pl.BlockSpec((rows, n), lambda i: (0, 0))],
        out_specs=[pl.BlockSpec((rows, LANES), lambda i: (0, 0))] * 2,
        out_shape=[jax.ShapeDtypeStruct((rows, LANES), F32), jax.ShapeDtypeStruct((rows, LANES), jnp.int32)],
        compiler_params=_cparams("arbitrary"),
        name="expert_thresholds",
    )(aff_t)


def _tile_counts_body(a_ref, thr_ref, cgt_ref, ceq_ref):
    t = pl.program_id(0)
    a, thr = a_ref[...], thr_ref[...]
    cgt_ref[pl.ds(t, 1), :] = jnp.sum((a > thr).astype(jnp.int32), axis=0, keepdims=True)
    ceq_ref[pl.ds(t, 1), :] = jnp.sum((a == thr).astype(jnp.int32), axis=0, keepdims=True)


def tile_counts(aff, thr_l, tm):
    n = aff.shape[0]
    n_tiles = n // tm
    return pl.pallas_call(
        _tile_counts_body,
        grid=(n_tiles,),
        in_specs=[pl.BlockSpec((tm, LANES), lambda t: (t, 0)), pl.BlockSpec((1, LANES), lambda t: (0, 0))],
        out_specs=[pl.BlockSpec((n_tiles, LANES), lambda t: (0, 0))] * 2,
        out_shape=[jax.ShapeDtypeStruct((n_tiles, LANES), jnp.int32)] * 2,
        compiler_params=_cparams("arbitrary"),
        name="tile_counts",
    )(aff, thr_l)


def _slots_body(a_ref, thr_ref, need_ref, eqb_ref, selb_ref, slot_ref, slott_ref, *, n_exp):
    t = pl.program_id(0)
    tm = a_ref.shape[0]
    a, thr = a_ref[...], thr_ref[...]
    lane = lax.broadcasted_iota(jnp.int32, (1, LANES), 1)
    row = lax.broadcasted_iota(jnp.int32, (tm, tm), 0)
    col = lax.broadcasted_iota(jnp.int32, (tm, tm), 1)
    before = (row > col).astype(BF16)
    eq = a == thr
    eq_rank = jnp.dot(before, eq.astype(BF16), preferred_element_type=F32) + eqb_ref[pl.ds(t, 1), :].astype(F32)
    sel = ((a > thr) | (eq & (eq_rank < need_ref[...].astype(F32)))) & (lane < n_exp)
    pos = jnp.dot(before, sel.astype(BF16), preferred_element_type=F32) + selb_ref[pl.ds(t, 1), :].astype(F32)
    slot = jnp.where(sel, pos, -1.0)
    slot_ref[...] = slot.astype(jnp.int32)
    slott_ref[...] = slot.T[:EXPERT_ROWS, :].astype(jnp.int32)


def token_slots(aff, thr_l, need_l, eq_base, sel_base, n_exp, tm):
    n = aff.shape[0]
    n_tiles = n // tm
    full = lambda r: pl.BlockSpec((r, LANES), lambda t: (0, 0))
    return pl.pallas_call(
        functools.partial(_slots_body, n_exp=n_exp),
        grid=(n_tiles,),
        in_specs=[pl.BlockSpec((tm, LANES), lambda t: (t, 0)), full(1), full(1), full(n_tiles), full(n_tiles)],
        out_specs=[pl.BlockSpec((tm, LANES), lambda t: (t, 0)), pl.BlockSpec((EXPERT_ROWS, tm), lambda t: (0, t))],
        out_shape=[jax.ShapeDtypeStruct((n, LANES), jnp.int32), jax.ShapeDtypeStruct((EXPERT_ROWS, n), jnp.int32)],
        compiler_params=_cparams("parallel"),
        name="token_slots",
    )(aff, thr_l, need_l, eq_base, sel_base)


def _window_start(s):
    return pl.multiple_of((s // ROW_ALIGN) * ROW_ALIGN, ROW_ALIGN)


def _dispatch_body(base_ref, cnt_ref, slott_ref, h_ref, xe_ref, stage, extra, carry, sems, xsem, *, n_exp, capacity):
    t = pl.program_id(0)
    n_tiles = pl.num_programs(0)
    par = t % 2
    wc = stage.shape[2]
    sub = lax.broadcasted_iota(jnp.int32, (wc, 1), 0)

    @pl.when(t == 0)
    def _():
        carry[...] = jnp.zeros_like(carry)

    starts = [_window_start(base_ref[t * n_exp + e]) for e in range(n_exp)]
    onehot = jnp.concatenate([(slott_ref[e:e + 1, :] - starts[e] == sub) for e in range(n_exp)], axis=0)
    rows = jnp.dot(onehot.astype(BF16), h_ref[...], preferred_element_type=F32)

    def window_copy(e, k):
        return pltpu.make_async_copy(stage.at[par, e], xe_ref.at[e, pl.ds(starts[e] + k * wc, wc)], sems.at[e])

    @pl.when(t > 0)
    def _():
        for e in range(n_exp):
            pltpu.make_async_copy(stage.at[1 - par, e], xe_ref.at[e, pl.ds(0, wc)], sems.at[e]).wait()

    for e in range(n_exp):
        s = base_ref[t * n_exp + e]
        end16 = _window_start(s + cnt_ref[t * n_exp + e])
        n_win = (end16 - starts[e]) // wc + 1
        stage[par, e] = rows[e * wc:(e + 1) * wc].astype(BF16)
        stage[par, e, 0:ROW_ALIGN, :] += carry[e]
        window_copy(e, 0).start()

        def more(k, c, e=e):
            oh = slott_ref[e:e + 1, :] - (starts[e] + k * wc) == sub
            extra[...] = jnp.dot(oh.astype(BF16), h_ref[...], preferred_element_type=F32).astype(BF16)
            cp = pltpu.make_async_copy(extra, xe_ref.at[e, pl.ds(starts[e] + k * wc, wc)], xsem)
            cp.start()
            cp.wait()
            return c

        lax.fori_loop(1, n_win, more, 0)
        off = pl.multiple_of(end16 - starts[e] - (n_win - 1) * wc, ROW_ALIGN)

        @pl.when(n_win == 1)
        def _(e=e, off=off):
            carry[e] = stage[par, e, pl.ds(off, ROW_ALIGN), :]

        @pl.when(n_win > 1)
        def _(e=e, off=off):
            carry[e] = extra[pl.ds(off, ROW_ALIGN), :]

    @pl.when(t == n_tiles - 1)
    def _():
        for e in range(n_exp):
            window_copy(e, 0).wait()
        extra[...] = jnp.zeros_like(extra)
        for e in range(n_exp):
            cp = pltpu.make_async_copy(extra, xe_ref.at[e, pl.ds(capacity, wc)], xsem)
            cp.start()
            cp.wait()


def dispatch_rows(h, slot_t, base, cnt, n_exp, capacity, tm):
    n, d = h.shape
    n_tiles = n // tm
    wc = DISPATCH_WINDOW
    grid_spec = pltpu.PrefetchScalarGridSpec(
        num_scalar_prefetch=2,
        grid=(n_tiles,),
        in_specs=[pl.BlockSpec((EXPERT_ROWS, tm), lambda t, b, c: (0, t)),
                  pl.BlockSpec((tm, d), lambda t, b, c: (t, 0))],
        out_specs=pl.BlockSpec(memory_space=pl.ANY),
        scratch_shapes=[pltpu.VMEM((2, n_exp, wc, d), BF16), pltpu.VMEM((wc, d), BF16),
                        pltpu.VMEM((n_exp, ROW_ALIGN, d), BF16),
                        pltpu.SemaphoreType.DMA((n_exp,)), pltpu.SemaphoreType.DMA(())],
    )
    return pl.pallas_call(
        functools.partial(_dispatch_body, n_exp=n_exp, capacity=capacity),
        grid_spec=grid_spec,
        out_shape=jax.ShapeDtypeStruct((n_exp, capacity + wc, d), BF16),
        compiler_params=_cparams("arbitrary"),
        name="dispatch_rows",
    )(base, cnt, slot_t, h)


def _expert_body(x_ref, wg_ref, wu_ref, wd_ref, o_ref, *, tf):
    x = x_ref[0]
    acc = None
    for f0 in range(0, wg_ref.shape[2], tf):
        a = jnp.dot(x, wg_ref[0, :, f0:f0 + tf], preferred_element_type=F32)
        u = jnp.dot(x, wu_ref[0, :, f0:f0 + tf], preferred_element_type=F32)
        hid = (a * jax.nn.sigmoid(a) * u).astype(BF16)
        part = jnp.dot(hid, wd_ref[0, f0:f0 + tf, :], preferred_element_type=F32)
        acc = part if acc is None else acc + part
    o_ref[0] = acc.astype(o_ref.dtype)


def expert_ffn(xe, w_gate, w_up, w_down, capacity, tf=1024):
    e, _, d = xe.shape
    f = w_gate.shape[2]
    tm = math.gcd(capacity, ROW_TILE)
    tf = math.gcd(f, tf)
    return pl.pallas_call(
        functools.partial(_expert_body, tf=tf),
        grid=(e, capacity // tm),
        in_specs=[
            pl.BlockSpec((1, tm, d), lambda ei, i: (ei, i, 0)),
            pl.BlockSpec((1, d, f), lambda ei, i: (ei, 0, 0)),
            pl.BlockSpec((1, d, f), lambda ei, i: (ei, 0, 0)),
            pl.BlockSpec((1, f, d), lambda ei, i: (ei, 0, 0)),
        ],
        out_specs=pl.BlockSpec((1, tm, d), lambda ei, i: (ei, i, 0)),
        out_shape=jax.ShapeDtypeStruct((e, capacity, d), BF16),
        compiler_params=_cparams("parallel", "parallel"),
        name="expert_ffn",
    )(xe, w_gate, w_up, w_down)


def _combine_body(base_ref, cnt_ref, slot_ref, aff_ref, x_ref, gate_ref, fg_ref, ye_ref, o_ref,
                  win, extra, acc, sems, xsem, *, n_exp, capacity, final_norm):
    t = pl.program_id(0)
    n_tiles = pl.num_programs(0)
    par = t % 2
    wc = win.shape[3]
    lanes = lax.broadcasted_iota(jnp.int32, (1, wc), 1)

    def start_of(tt, e, k):
        lo = _window_start(base_ref[tt * n_exp + e]) + k * wc
        return lo, pl.multiple_of(jnp.minimum(lo, capacity - wc), ROW_ALIGN)

    def fetch(tt, slot_par, e, k):
        _, st = start_of(tt, e, k)
        return pltpu.make_async_copy(ye_ref.at[e, pl.ds(st, wc)], win.at[k, slot_par, e], sems.at[k, slot_par, e])

    @pl.when(t == 0)
    def _():
        for e in range(n_exp):
            fetch(0, 0, e, 0).start()
            fetch(0, 0, e, 1).start()

    @pl.when(t + 1 < n_tiles)
    def _():
        for e in range(n_exp):
            fetch(t + 1, 1 - par, e, 0).start()
            fetch(t + 1, 1 - par, e, 1).start()

    def spread(e, lo, st):
        col = slot_ref[:, e:e + 1]
        return jnp.where((col - st == lanes) & (col >= lo), aff_ref[:, e:e + 1], 0.0).astype(BF16)

    src_e = lax.broadcasted_iota(jnp.int32, (LANES, n_exp * wc), 0)
    dst_e = lax.broadcasted_iota(jnp.int32, (LANES, n_exp * wc), 1) // wc
    expand = (src_e == dst_e).astype(BF16)
    s1 = slot_ref[...] + 1
    parts = jnp.concatenate([(s1 // 64).astype(BF16), (s1 % 64).astype(BF16)], axis=1)
    slot_rep = jnp.dot(parts, jnp.concatenate([expand * 64, expand], axis=0), preferred_element_type=F32) - 1.0
    aff_rep = jnp.dot(aff_ref[...].astype(BF16), expand, preferred_element_type=F32)
    lo_vec = jnp.concatenate([jnp.full((1, wc), start_of(t, e, 0)[0], jnp.int32) for e in range(n_exp)], axis=1)
    st_vec = jnp.concatenate([jnp.full((1, wc), start_of(t, e, 0)[1], jnp.int32) for e in range(n_exp)], axis=1)
    lane_in_win = lax.broadcasted_iota(jnp.int32, (1, n_exp * wc), 1) % wc
    hit = (slot_rep == (st_vec + lane_in_win).astype(F32)) & (slot_rep >= lo_vec.astype(F32))
    onehot = jnp.where(hit, aff_rep, 0.0).astype(BF16)
    for e in range(n_exp):
        fetch(t, par, e, 0).wait()
    acc[...] = jnp.dot(onehot, win[0, par].reshape(n_exp * wc, win.shape[4]), preferred_element_type=F32)
    for e in range(n_exp):
        s = base_ref[t * n_exp + e]
        n_win = (s - _window_start(s) + cnt_ref[t * n_exp + e] + wc - 1) // wc
        fetch(t, par, e, 1).wait()

        @pl.when(n_win > 1)
        def _(e=e):
            lo_1, st_1 = start_of(t, e, 1)
            acc[...] += jnp.dot(spread(e, lo_1, st_1), win[1, par, e], preferred_element_type=F32)

        def more(k, c, e=e):
            lo_k, st_k = start_of(t, e, k)
            cp = pltpu.make_async_copy(ye_ref.at[e, pl.ds(st_k, wc)], extra, xsem)
            cp.start()
            cp.wait()
            acc[...] += jnp.dot(spread(e, lo_k, st_k), extra[...], preferred_element_type=F32)
            return c

        lax.fori_loop(2, n_win, more, 0)

    x = x_ref[...] + gate_ref[0] * acc[...]
    if final_norm:
        x = x * lax.rsqrt(jnp.mean(x * x, axis=-1, keepdims=True) + EPS) * fg_ref[...]
    o_ref[...] = x


def combine_rows(slot, aff, x2, gate, final_g, ye, base, cnt, seq, n_exp, capacity, tm, final_norm):
    n, d = x2.shape
    n_tiles = n // tm
    wc = DISPATCH_WINDOW
    assert wc <= capacity <= 64 * 256 and capacity % ROW_ALIGN == 0 and seq % tm == 0
    grid_spec = pltpu.PrefetchScalarGridSpec(
        num_scalar_prefetch=2,
        grid=(n_tiles,),
        in_specs=[pl.BlockSpec((tm, LANES), lambda t, b, c: (t, 0)),
                  pl.BlockSpec((tm, LANES), lambda t, b, c: (t, 0)),
                  pl.BlockSpec((tm, d), lambda t, b, c: (t, 0)),
                  pl.BlockSpec((1, 1, d), lambda t, b, c: ((t * tm) // seq, 0, 0)),
                  pl.BlockSpec((1, d), lambda t, b, c: (0, 0)),
                  pl.BlockSpec(memory_space=pl.ANY)],
        out_specs=pl.BlockSpec((tm, d), lambda t, b, c: (t, 0)),
        scratch_shapes=[pltpu.VMEM((2, 2, n_exp, wc, d), BF16), pltpu.VMEM((wc, d), BF16), pltpu.VMEM((tm, d), F32),
                        pltpu.SemaphoreType.DMA((2, 2, n_exp)), pltpu.SemaphoreType.DMA(())],
    )
    return pl.pallas_call(
        functools.partial(_combine_body, n_exp=n_exp, capacity=capacity, final_norm=final_norm),
        grid_spec=grid_spec,
        out_shape=jax.ShapeDtypeStruct((n, d), F32),
        compiler_params=_cparams("arbitrary"),
        name="combine_rows",
    )(base, cnt, slot, aff, x2, gate, final_g, ye)


def expert_choice_moe(x2, seq, norm_g, sc, sh, gate2, final_g, w_router, w_gate, w_up, w_down, final_norm):
    n, d = x2.shape
    n_exp = w_router.shape[1]
    capacity = CAPACITY_FACTOR * n // n_exp
    tm = _row_tile(seq, ROUTE_TILE)
    h, aff, aff_t = norm_router(x2, seq, norm_g, sc, sh, w_router)
    thr, cgt = expert_thresholds(aff_t, capacity)
    pad = LANES - thr.shape[0]
    thr_l = jnp.pad(thr[:, 0], (0, pad))[None, :]
    need_l = jnp.pad(capacity - cgt[:, 0], (0, pad))[None, :]
    t_gt, t_eq = tile_counts(aff, thr_l, tm)
    eq_base = jnp.cumsum(t_eq, axis=0) - t_eq
    t_sel = t_gt + jnp.clip(need_l - eq_base, 0, t_eq)
    sel_base = jnp.cumsum(t_sel, axis=0) - t_sel
    slot, slot_t = token_slots(aff, thr_l, need_l, eq_base, sel_base, n_exp, tm)
    base = sel_base[:, :n_exp].reshape(-1)
    cnt = t_sel[:, :n_exp].reshape(-1)
    xe = dispatch_rows(h, slot_t, base, cnt, n_exp, capacity, tm)
    ye = expert_ffn(xe, w_gate, w_up, w_down, capacity)
    return combine_rows(slot, aff, x2, gate2, final_g, ye, base, cnt, seq, n_exp, capacity, tm, final_norm)


def _trunk(x, c, params):
    (norm_mix_g, norm_ffn_g, ada_w, ada_b, hg_w_in, hg_w_out, hg_norm_g, lb_table,
     da_w_qkv, da_w_out, s5_ops, s5_d, s5_w_glu, moe_w_router, moe_w_gate, moe_w_up, moe_w_down, final_g) = params
    bsz, seq, d = x.shape
    depth = norm_mix_g.shape[0]
    x2 = x.reshape(bsz * seq, d)
    cond = jax.nn.silu(c)
    for layer in range(depth):
        kind, slot = layer % N_MIXERS, layer // N_MIXERS
        mod = (jnp.dot(cond, ada_w[layer], precision=HIGHEST) + ada_b[layer])[:, None, :]
        sh1, sc1, g1, sh2, sc2, g2 = jnp.split(mod, 6, axis=-1)
        gmix = norm_mix_g[layer][None, :]
        if kind == 0:
            proj = nm_matmul(x2, seq, gmix, sc1, sh1, hg_w_in[slot])
            lb = lb_table[layer].reshape(HG_HEADS, 1, -1)
            m = hgrn_recurrence(proj.reshape(bsz, seq, -1), lb, jnp.log1p(-lb), 1.0 - lb,
                                hg_norm_g[slot][None, :].astype(F32))
            x2 = proj_residual(m.reshape(bsz * seq, d), hg_w_out[slot], x2, g1, seq)
        elif kind == 1:
            proj = nm_matmul(x2, seq, gmix, sc1, sh1, da_w_qkv[slot])
            m = dilated_attention(proj.reshape(bsz, seq, -1))
            x2 = proj_residual(m.reshape(bsz * seq, d), da_w_out[slot], x2, g1, seq)
        else:
            h = norm_mod(x2, seq, gmix, sc1, sh1)
            z = s5_mix(h.reshape(bsz, seq, d), s5_ops[slot], s5_d[slot])
            x2 = proj_residual(z.reshape(bsz * seq, d), s5_w_glu[slot], x2, g1, seq, glu=True)
        x2 = expert_choice_moe(x2, seq, norm_ffn_g[layer][None, :], sc2, sh2, g2, final_g[None, :].astype(F32),
                               moe_w_router[layer], moe_w_gate[layer], moe_w_up[layer], moe_w_down[layer],
                               final_norm=(layer == depth - 1))
    return x2.reshape(bsz, seq, d)


def kernel(x_prompt, x_sample, c_prompt, c_sample, norm_mix_g, norm_ffn_g, ada_w, ada_b, hg_w_in, hg_w_out, hg_norm_g, hg_lb_logits, da_w_qkv, da_w_out, s5_a_re, s5_a_im, s5_log_dt, s5_b_re, s5_b_im, s5_c_re, s5_c_im, s5_d, s5_w_glu, moe_w_router, moe_w_gate, moe_w_up, moe_w_down, final_g):
    lb_table = jnp.cumsum(jax.nn.softmax(hg_lb_logits.astype(F32), axis=0), axis=0)
    lb_table = lb_table - lb_table[0:1]
    s5_ops = [_s5_operators(s5_a_re[s], s5_a_im[s], s5_log_dt[s], s5_b_re[s], s5_b_im[s],
                            s5_c_re[s], s5_c_im[s]) for s in range(s5_a_re.shape[0])]
    bf = lambda w: w.astype(BF16)
    params = (norm_mix_g.astype(F32), norm_ffn_g.astype(F32), ada_w, ada_b, bf(hg_w_in), bf(hg_w_out),
              hg_norm_g, lb_table, bf(da_w_qkv), bf(da_w_out), s5_ops, s5_d, bf(s5_w_glu), moe_w_router,
              bf(moe_w_gate), bf(moe_w_up), bf(moe_w_down), final_g)
    return (_trunk(x_prompt, c_prompt, params), _trunk(x_sample, c_sample, params))
```

```python
import functools
import math

import jax
import jax.numpy as jnp
from jax import lax
from jax.experimental import pallas as pl
from jax.experimental.pallas import tpu as pltpu

F32 = jnp.float32
BF16 = jnp.bfloat16
HIGHEST = lax.Precision.HIGHEST

EPS = 1e-6
N_MIXERS = 3
HG_HEADS = 8
HG_CHUNK = 64
HG_SUPER = 4
HG_FINISH = 8
DA_PATTERNS = ((128, 1), (512, 4), (2048, 16))
DA_HEADS = 16
DA_QBLOCK = 128
DA_SEGMENT = 8
DA_UNROLL = 8
ROPE_THETA = 10000.0
S5_GROUP = 16
S5_STATE = 64
S5_CHUNK = 16
S5_SEGMENTS = 8
S5_ROWS = 256
CAPACITY_FACTOR = 2
EXPERT_ROWS = 16
ROUTE_TILE = 512
DISPATCH_WINDOW = 128
ROW_ALIGN = 16

LANES = 128
VMEM_LIMIT = 56 * 1024 * 1024
ROW_TILE = 1024


def _cparams(*sem):
    return pltpu.CompilerParams(dimension_semantics=sem, vmem_limit_bytes=VMEM_LIMIT)


def _row_tile(t, cap=ROW_TILE):
    return math.gcd(t, cap)


def _norm_mod(x, g, sc, sh):
    ms = jnp.mean(x * x, axis=-1, keepdims=True)
    return (x * lax.rsqrt(ms + EPS) * g) * (1.0 + sc) + sh


def _split2(x):
    hi = x.astype(BF16)
    return hi, (x - hi.astype(F32)).astype(BF16)


def _nm_matmul_body(x_ref, g_ref, sc_ref, sh_ref, w_ref, o_ref, h_scr):
    @pl.when(pl.program_id(1) == 0)
    def _():
        h_scr[...] = _norm_mod(x_ref[...], g_ref[...], sc_ref[0], sh_ref[0]).astype(BF16)

    o_ref[...] = jnp.dot(h_scr[...], w_ref[...], preferred_element_type=F32).astype(o_ref.dtype)


def nm_matmul(x2, seq, g, sc, sh, w, tn=1024):
    n, d = x2.shape
    f = w.shape[1]
    tm = _row_tile(seq)
    tn = math.gcd(f, tn)
    return pl.pallas_call(
        _nm_matmul_body,
        grid=(n // tm, f // tn),
        in_specs=[
            pl.BlockSpec((tm, d), lambda i, j: (i, 0)),
            pl.BlockSpec((1, d), lambda i, j: (0, 0)),
            pl.BlockSpec((1, 1, d), lambda i, j: ((i * tm) // seq, 0, 0)),
            pl.BlockSpec((1, 1, d), lambda i, j: ((i * tm) // seq, 0, 0)),
            pl.BlockSpec((d, tn), lambda i, j: (0, j)),
        ],
        out_specs=pl.BlockSpec((tm, tn), lambda i, j: (i, j)),
        out_shape=jax.ShapeDtypeStruct((n, f), BF16),
        scratch_shapes=[pltpu.VMEM((tm, d), BF16)],
        compiler_params=_cparams("parallel", "arbitrary"),
        name="nm_matmul",
    )(x2, g, sc, sh, w)


def _norm_mod_body(x_ref, g_ref, sc_ref, sh_ref, o_ref):
    o_ref[...] = _norm_mod(x_ref[...], g_ref[...], sc_ref[0], sh_ref[0])


def norm_mod(x2, seq, g, sc, sh):
    n, d = x2.shape
    tm = _row_tile(seq)
    return pl.pallas_call(
        _norm_mod_body,
        grid=(n // tm,),
        in_specs=[
            pl.BlockSpec((tm, d), lambda i: (i, 0)),
            pl.BlockSpec((1, d), lambda i: (0, 0)),
            pl.BlockSpec((1, 1, d), lambda i: ((i * tm) // seq, 0, 0)),
            pl.BlockSpec((1, 1, d), lambda i: ((i * tm) // seq, 0, 0)),
        ],
        out_specs=pl.BlockSpec((tm, d), lambda i: (i, 0)),
        out_shape=jax.ShapeDtypeStruct((n, d), F32),
        compiler_params=_cparams("parallel"),
        name="norm_mod",
    )(x2, g, sc, sh)


def _norm_router_body(x_ref, g_ref, sc_ref, sh_ref, wh_ref, wl_ref, h_ref, aff_ref, afft_ref, *, n_exp):
    h = _norm_mod(x_ref[...], g_ref[...], sc_ref[0], sh_ref[0])
    h_hi, h_lo = _split2(h)
    h_ref[...] = h_hi
    logits = (jnp.dot(h_hi, wh_ref[...], preferred_element_type=F32)
              + jnp.dot(h_lo, wh_ref[...], preferred_element_type=F32)
              + jnp.dot(h_hi, wl_ref[...], preferred_element_type=F32))
    lane = lax.broadcasted_iota(jnp.int32, (1, LANES), 1)
    logits = jnp.where(lane < n_exp, logits, -jnp.inf)
    ex = jnp.exp(logits - jnp.max(logits, axis=-1, keepdims=True))
    aff = ex / jnp.sum(ex, axis=-1, keepdims=True)
    aff_ref[...] = aff
    afft_ref[...] = aff.T[:EXPERT_ROWS, :]


def norm_router(x2, seq, g, sc, sh, w_router):
    n, d = x2.shape
    n_exp = w_router.shape[1]
    assert n_exp <= EXPERT_ROWS
    wr = jnp.zeros((d, LANES), F32).at[:, :n_exp].set(w_router.astype(F32))
    w_hi = wr.astype(BF16)
    w_lo = (wr - w_hi.astype(F32)).astype(BF16)
    tm = _row_tile(seq, ROUTE_TILE)
    return pl.pallas_call(
        functools.partial(_norm_router_body, n_exp=n_exp),
        grid=(n // tm,),
        in_specs=[
            pl.BlockSpec((tm, d), lambda i: (i, 0)),
            pl.BlockSpec((1, d), lambda i: (0, 0)),
            pl.BlockSpec((1, 1, d), lambda i: ((i * tm) // seq, 0, 0)),
            pl.BlockSpec((1, 1, d), lambda i: ((i * tm) // seq, 0, 0)),
            pl.BlockSpec((d, LANES), lambda i: (0, 0)),
            pl.BlockSpec((d, LANES), lambda i: (0, 0)),
        ],
        out_specs=[pl.BlockSpec((tm, d), lambda i: (i, 0)),
                   pl.BlockSpec((tm, LANES), lambda i: (i, 0)),
                   pl.BlockSpec((EXPERT_ROWS, tm), lambda i: (0, i))],
        out_shape=[jax.ShapeDtypeStruct((n, d), BF16), jax.ShapeDtypeStruct((n, LANES), F32),
                   jax.ShapeDtypeStruct((EXPERT_ROWS, n), F32)],
        compiler_params=_cparams("parallel"),
        name="norm_router",
    )(x2, g, sc, sh, w_hi, w_lo)


def _proj_res_body(m_ref, w_ref, x_ref, gate_ref, o_ref):
    y = jnp.dot(m_ref[...].astype(BF16), w_ref[...], preferred_element_type=F32)
    o_ref[...] = x_ref[...] + gate_ref[0] * y


def _glu_res_body(m_ref, w_ref, x_ref, gate_ref, o_ref):
    d = x_ref.shape[-1]
    y = jnp.dot(m_ref[...].astype(BF16), w_ref[...], preferred_element_type=F32)
    o_ref[...] = x_ref[...] + gate_ref[0] * (y[:, :d] * jax.nn.sigmoid(y[:, d:]))


def proj_residual(m2, w, x2, gate, seq, glu=False):
    n, d = x2.shape
    f = w.shape[1]
    tm = _row_tile(seq, 512)
    return pl.pallas_call(
        _glu_res_body if glu else _proj_res_body,
        grid=(n // tm,),
        in_specs=[
            pl.BlockSpec((tm, d), lambda i: (i, 0)),
            pl.BlockSpec((d, f), lambda i: (0, 0)),
            pl.BlockSpec((tm, d), lambda i: (i, 0)),
            pl.BlockSpec((1, 1, d), lambda i: ((i * tm) // seq, 0, 0)),
        ],
        out_specs=pl.BlockSpec((tm, d), lambda i: (i, 0)),
        out_shape=jax.ShapeDtypeStruct((n, d), F32),
        compiler_params=_cparams("parallel"),
        name="glu_residual" if glu else "proj_residual",
    )(m2, w, x2, gate)


def _dot_nt(a, b):
    return lax.dot_general(a, b, (((1,), (1,)), ((), ())), preferred_element_type=F32)


def _hgrn_gates(z, lb, l1, om):
    e = jnp.exp(-jnp.abs(z))
    r = 1.0 / (1.0 + e)
    pos = z >= 0.0
    k = om * (jnp.where(pos, e, 1.0) * r)
    f = lb + om * (jnp.where(pos, 1.0, e) * r)
    log_sig = jnp.minimum(z, 0.0) + jnp.log(r)
    return jnp.maximum(jnp.log(f), l1 + log_sig), k


def _bcast_rows(x, rows, c, n_sub):
    return jnp.concatenate(
        [jnp.broadcast_to(x[rows[j]:rows[j] + 1, :], (c, x.shape[1])) for j in range(n_sub)], axis=0)


def _hgrn_intra(q, v, z, lb, l1, om, cum_mat, causal, ref_row, last_row, c, n_sub):
    logf, k = _hgrn_gates(z, lb, l1, om)
    l_hi, l_lo = _split2(logf)
    b = (jnp.dot(cum_mat, l_hi, preferred_element_type=F32)
         + jnp.dot(cum_mat, l_lo, preferred_element_type=F32))
    b_ref = _bcast_rows(b, [j * c + ref_row for j in range(n_sub)], c, n_sub)
    b_last = _bcast_rows(b, [j * c + last_row for j in range(n_sub)], c, n_sub)
    up, down = jnp.exp(b - b_ref), jnp.exp(b_ref - b)
    qd = (q * up).astype(BF16)
    kd = (k * down).astype(BF16)
    s = jnp.where(causal, _dot_nt(qd, kd), 0.0).astype(BF16)
    o = jnp.dot(s, v.astype(BF16), preferred_element_type=F32)
    qe = (q * (up * jnp.exp(b_ref))).astype(BF16)
    kl = (k * (down * jnp.exp(b_last - b_ref))).astype(BF16)
    kvs, decs = [], []
    for j in range(n_sub):
        r = slice(j * c, (j + 1) * c)
        kvs.append(jnp.dot(v[r].T.astype(BF16), kl[r], preferred_element_type=F32))
        decs.append(jnp.exp(b[j * c + last_row:j * c + last_row + 1, :]))
    return o, qe, kvs, decs


def _hgrn_body(q_ref, v_ref, zf_ref, zb_ref, g_ref, lb_ref, l1_ref, om_ref, ng_ref, o_ref,
               oi_scr, qe_scr, kv_scr, dec_scr, st_scr):
    seq = q_ref.shape[1]
    c = min(HG_CHUNK, seq)
    n_chunks = seq // c
    n_sub = math.gcd(n_chunks, HG_SUPER)
    sc = n_sub * c
    lb, l1, om = lb_ref[0], l1_ref[0], om_ref[0]
    row = lax.broadcasted_iota(jnp.int32, (sc, sc), 0)
    col = lax.broadcasted_iota(jnp.int32, (sc, sc), 1)
    same = (row // c) == (col // c)
    lower, upper = same & (row >= col), same & (row <= col)
    tril, triu = lower.astype(BF16), upper.astype(BF16)
    mid = c // 2

    def intra(i, carry):
        r = pl.ds(pl.multiple_of(i * sc, sc), sc)
        q, v = q_ref[0, r, :].astype(F32), v_ref[0, r, :].astype(F32)
        of, qf, kvf, decf = _hgrn_intra(q, v, zf_ref[0, r, :].astype(F32), lb, l1, om, tril, lower, mid - 1, c - 1, c, n_sub)
        ob, qb, kvb, decb = _hgrn_intra(q, v, zb_ref[0, r, :].astype(F32), lb, l1, om, triu, upper, c - mid, 0, c, n_sub)
        oi_scr[r, :] = of + ob
        qe_scr[r, 0:LANES] = qf
        qe_scr[r, LANES:2 * LANES] = qb
        for j in range(n_sub):
            n = i * n_sub + j
            kv_scr[0, n], kv_scr[1, n] = kvf[j], kvb[j]
            dec_scr[0, n], dec_scr[1, n] = jnp.broadcast_to(decf[j], (8, LANES)), jnp.broadcast_to(decb[j], (8, LANES))
        return carry

    lax.fori_loop(0, n_chunks // n_sub, intra, 0, unroll=2)

    def carry_state(n, carry):
        sf, sb = carry
        nb = n_chunks - 1 - n
        st_scr[n, :, 0:LANES] = sf.astype(BF16)
        st_scr[nb, :, LANES:2 * LANES] = sb.astype(BF16)
        sf = sf * dec_scr[0, n, 0:1, :] + kv_scr[0, n]
        sb = sb * dec_scr[1, nb, 0:1, :] + kv_scr[1, nb]
        return sf, sb

    zero = jnp.zeros((LANES, LANES), F32)
    lax.fori_loop(0, n_chunks, carry_state, (zero, zero))

    n_fin = math.gcd(n_chunks, HG_FINISH)
    fc = n_fin * c

    def finish(i, carry):
        parts = []
        for j in range(n_fin):
            n = i * n_fin + j
            parts.append(_dot_nt(qe_scr[pl.ds(pl.multiple_of(n * c, c), c), :], st_scr[n]))
        r = pl.ds(pl.multiple_of(i * fc, fc), fc)
        o = oi_scr[r, :] + jnp.concatenate(parts, axis=0)
        o = o * lax.rsqrt(jnp.mean(o * o, axis=-1, keepdims=True) + EPS) * ng_ref[...]
        o_ref[0, r, :] = (o * jax.nn.sigmoid(g_ref[0, r, :].astype(F32))).astype(o_ref.dtype)
        return carry

    lax.fori_loop(0, n_chunks // n_fin, finish, 0)


def hgrn_recurrence(proj, lb, l1, om, norm_g):
    bsz, seq, d5 = proj.shape
    d = d5 // 5
    h = HG_HEADS
    dk = d // h
    assert dk == LANES
    n_chunks = seq // min(HG_CHUNK, seq)

    def col(section):
        return pl.BlockSpec((1, seq, dk), lambda b, hh: (b, 0, section * h + hh))

    par = pl.BlockSpec((1, 1, dk), lambda b, hh: (hh, 0, 0))
    return pl.pallas_call(
        _hgrn_body,
        grid=(bsz, h),
        in_specs=[col(0), col(1), col(2), col(3), col(4), par, par, par,
                  pl.BlockSpec((1, dk), lambda b, hh: (0, 0))],
        out_specs=pl.BlockSpec((1, seq, dk), lambda b, hh: (b, 0, hh)),
        out_shape=jax.ShapeDtypeStruct((bsz, seq, d), BF16),
        scratch_shapes=[pltpu.VMEM((seq, dk), F32), pltpu.VMEM((seq, 2 * dk), BF16),
                        pltpu.VMEM((2, n_chunks, dk, dk), F32), pltpu.VMEM((2, n_chunks, 8, dk), F32),
                        pltpu.VMEM((n_chunks, dk, 2 * dk), BF16)],
        compiler_params=_cparams("parallel", "parallel"),
        name="hgrn_recurrence",
    )(proj, proj, proj, proj, proj, lb, l1, om, norm_g)


def _rope_pair(x, cos, sin_signed, swap):
    partner = jnp.dot(x.astype(BF16), swap, preferred_element_type=F32)
    return x * cos + partner * sin_signed


def _attn_group(q_ref, k_ref, v_ref, cos_ref, sin_ref, qs, ks, vs, ot, mt, lt, s_scr, p_scr, wide, dil, half):
    seq = q_ref.shape[1]
    length = seq // dil
    dh = LANES // 2
    lane = lax.broadcasted_iota(jnp.int32, (1, LANES), 1)
    src_lane = lax.broadcasted_iota(jnp.int32, (LANES, LANES), 0)
    dst_lane = lax.broadcasted_iota(jnp.int32, (LANES, LANES), 1)
    quarter = dh // 2
    swap = (src_lane == jnp.where((dst_lane % dh) < quarter, dst_lane + quarter, dst_lane - quarter)).astype(BF16)
    head0 = lane < dh
    tile = math.gcd(length, 256)
    if dil > 1:
        for j, ref in enumerate((q_ref, k_ref, v_ref)):
            for t0 in range(0, seq, 512):
                rows = pl.ds(t0, min(512, seq))
                wide[j, rows, :] = ref[0, rows, :].astype(F32)
    for r in range(dil):
        for t0 in range(0, length, tile):
            dst = pl.ds(r * length + t0, tile)
            if dil == 1:
                q, k, v = (ref[0, dst, :].astype(F32) for ref in (q_ref, k_ref, v_ref))
            else:
                src = pl.ds(r + t0 * dil, tile, stride=dil)
                q, k, v = wide[0, src, :], wide[1, src, :], wide[2, src, :]
            cos, sin = cos_ref[0, dst, :], sin_ref[0, dst, :]
            qs[dst, :] = (_rope_pair(q, cos, sin, swap) * (dh ** -0.5)).astype(BF16)
            ks[dst, :] = _rope_pair(k, cos, sin, swap).astype(BF16)
            vs[dst, :] = v.astype(BF16)

    qb = min(DA_QBLOCK, length)
    span = min(qb + 2 * half, length)
    n_blocks = seq // qb
    seg = math.gcd(n_blocks, DA_SEGMENT)
    delta = lax.broadcasted_iota(jnp.int32, (qb, span), 1) - lax.broadcasted_iota(jnp.int32, (qb, span), 0)

    def place(bi):
        row0 = pl.multiple_of(bi * qb, qb)
        r = row0 // length
        m0 = row0 - r * length
        k0 = jnp.clip(m0 - half, 0, length - span)
        return pl.ds(row0, qb), pl.ds(pl.multiple_of(r * length + k0, 16), span), k0 - m0

    def segment(si, carry):
        def scores(j, c):
            rows, krows, off = place(si * seg + j)
            q, kk = qs[rows, :], ks[krows, :]
            bias = jnp.where((delta >= -half - off) & (delta <= half - off), 0.0, -jnp.inf)
            zero = jnp.zeros_like(q)
            s_scr[j, 0, 0:qb, 0:span] = _dot_nt(jnp.where(head0, q, zero), kk) + bias
            s_scr[j, 1, 0:qb, 0:span] = _dot_nt(jnp.where(head0, zero, q), kk) + bias
            return c

        def softmax(j, c):
            rows, _, _ = place(si * seg + j)
            ms = []
            for h in range(2):
                s = s_scr[j, h, 0:qb, 0:span]
                m = jnp.max(s, axis=-1, keepdims=True)
                p_scr[j, h, 0:qb, 0:span] = jnp.exp(s - m).astype(BF16)
                ms.append(jnp.broadcast_to(m, (qb, LANES)))
            mt[rows, :] = jnp.where(head0, ms[0], ms[1])
            return c

        def values(j, c):
            rows, krows, _ = place(si * seg + j)
            vv = vs[krows, :]
            one = jnp.ones_like(vv)
            r0 = jnp.dot(p_scr[j, 0, 0:qb, 0:span], jnp.where(head0, vv, one), preferred_element_type=F32)
            r1 = jnp.dot(p_scr[j, 1, 0:qb, 0:span], jnp.where(head0, one, vv), preferred_element_type=F32)
            ot[rows, :] = jnp.where(head0, r0, r1)
            lt[rows, :] = pltpu.roll(jnp.where(head0, r1, r0), dh, 1)
            return c

        lax.fori_loop(0, seg, scores, 0, unroll=DA_UNROLL)
        lax.fori_loop(0, seg, softmax, 0, unroll=DA_UNROLL)
        lax.fori_loop(0, seg, values, 0, unroll=DA_UNROLL)
        return carry

    lax.fori_loop(0, n_blocks // seg, segment, 0)


def _attn_body(q_ref, k_ref, v_ref, cos_ref, sin_ref, o_ref, qs, ks, vs, ot, mt, lt, acc, mrun, lrun, s_scr, p_scr, wide):
    g = pl.program_id(2)
    seq = q_ref.shape[1]
    sets = ((acc, mrun, lrun), (ot, mt, lt))
    for gi, (window, dil) in enumerate(DA_PATTERNS):
        half = window // (2 * dil)
        assert half % 16 == 0
        length = seq // dil
        tile = math.gcd(length, 256)
        fresh, old = sets[gi % 2], sets[(gi + 1) % 2]
        prev_dil = DA_PATTERNS[gi - 1][1] if gi else 1
        assert dil % prev_dil == 0

        @pl.when(g == gi)
        def _(dil=dil, half=half, gi=gi, length=length, tile=tile, fresh=fresh, old=old, prev_dil=prev_dil):
            _attn_group(q_ref, k_ref, v_ref, cos_ref, sin_ref, qs, ks, vs, *fresh, s_scr, p_scr, wide, dil, half)
            if gi == 0:
                return
            step = dil // prev_dil
            prev_length = seq // prev_dil
            for r in range(dil):
                for t0 in range(0, length, tile):
                    dst = pl.ds(r * length + t0, tile)
                    first = (r % prev_dil) * prev_length + r // prev_dil + t0 * step
                    src = pl.ds(first, tile, stride=step) if step > 1 else pl.ds(first, tile)
                    m_old, m_new = old[1][src, :], fresh[1][dst, :]
                    m_all = jnp.maximum(m_old, m_new)
                    w_old, w_new = jnp.exp(m_old - m_all), jnp.exp(m_new - m_all)
                    fresh[0][dst, :] = old[0][src, :] * w_old + fresh[0][dst, :] * w_new
                    fresh[2][dst, :] = old[2][src, :] * w_old + fresh[2][dst, :] * w_new
                    fresh[1][dst, :] = m_all

    last = len(DA_PATTERNS) - 1

    @pl.when(g == last)
    def _():
        run = sets[last % 2]
        dil = DA_PATTERNS[last][1]
        length = seq // dil
        tile = math.gcd(length, 256)
        for r in range(dil):
            for t0 in range(0, length, tile):
                src = pl.ds(r * length + t0, tile)
                dst = pl.ds(r + t0 * dil, tile, stride=dil) if dil > 1 else src
                o_ref[0, dst, :] = run[0][src, :] / run[2][src, :]


def _rope_tables(seq):
    dh = LANES // 2
    halfd = dh // 2
    inv = ROPE_THETA ** (-jnp.arange(halfd, dtype=F32) / halfd)
    cos_t, sin_t = [], []
    for _, dil in DA_PATTERNS:
        pos = jnp.arange(seq, dtype=F32).reshape(seq // dil, dil).T.reshape(seq)
        ang = pos[:, None] * inv[None, :]
        cos, sin = jnp.cos(ang), jnp.sin(ang)
        cos_t.append(jnp.tile(cos, (1, 4)))
        sin_t.append(jnp.concatenate([-sin, sin, -sin, sin], axis=1))
    return jnp.stack(cos_t), jnp.stack(sin_t)


def dilated_attention(proj):
    bsz, seq, d9 = proj.shape
    n_groups = len(DA_PATTERNS)
    d = d9 // (3 * n_groups)
    assert d // DA_HEADS == LANES // 2
    pairs = d // LANES
    cos, sin = _rope_tables(seq)
    qb_max = max(min(DA_QBLOCK, seq // dl) for _, dl in DA_PATTERNS)
    span_max = max(min(min(DA_QBLOCK, seq // dl) + 2 * (w // (2 * dl)), seq // dl) for w, dl in DA_PATTERNS)

    def col(part):
        return pl.BlockSpec((1, seq, LANES), lambda b, hp, g: (b, 0, (g * 3 + part) * pairs + hp))

    table = pl.BlockSpec((1, seq, LANES), lambda b, hp, g: (g, 0, 0))
    return pl.pallas_call(
        _attn_body,
        grid=(bsz, pairs, n_groups),
        in_specs=[col(0), col(1), col(2), table, table],
        out_specs=pl.BlockSpec((1, seq, LANES), lambda b, hp, g: (b, 0, hp)),
        out_shape=jax.ShapeDtypeStruct((bsz, seq, d), F32),
        scratch_shapes=[pltpu.VMEM((seq, LANES), BF16)] * 3 + [pltpu.VMEM((seq, LANES), F32)] * 6
        + [pltpu.VMEM((DA_SEGMENT, 2, qb_max, span_max), F32),
           pltpu.VMEM((DA_SEGMENT, 2, qb_max, span_max), BF16), pltpu.VMEM((3, seq, LANES), F32)],
        compiler_params=_cparams("parallel", "parallel", "arbitrary"),
        name="dilated_attention",
    )(proj, proj, proj, cos, sin)


def _s5_operators(a_re, a_im, log_dt, b_re, b_im, c_re, c_im):
    lc, i_dim, p_dim = S5_CHUNK, S5_GROUP, S5_STATE
    n_groups = a_re.shape[1]
    hp = dict(precision=HIGHEST)
    a = lax.complex(a_re.astype(F32), a_im.astype(F32))
    lam = a * jnp.exp(log_dt.astype(F32))[..., None]
    a_bar = jnp.exp(lam)
    bmat = lax.complex(b_re.astype(F32), b_im.astype(F32))
    cmat = lax.complex(c_re.astype(F32), c_im.astype(F32))
    b_bar = ((a_bar - 1.0) / a)[..., None] * bmat[None]
    tau = jnp.arange(lc + 1, dtype=F32)
    apow = jnp.exp(lam[:, :, None, :] * tau[None, None, :, None])
    kern = jnp.real(jnp.einsum('gip,dgtp,dgpj->dgtij', cmat, apow[:, :, :lc], b_bar, **hp))
    s_idx = jnp.arange(lc)[:, None]
    t_idx = jnp.arange(lc)[None, :]
    lag = t_idx - s_idx
    m_f = jnp.where((lag >= 0)[None, :, :, None, None], kern[0][:, jnp.clip(lag, 0)], 0.0)
    m_b = jnp.where((lag <= 0)[None, :, :, None, None], kern[1][:, jnp.clip(-lag, 0)], 0.0)
    m_op = (m_f + m_b).transpose(0, 1, 4, 2, 3).reshape(n_groups, lc * i_dim, lc * i_dim)
    rev = jnp.arange(lc - 1, -1, -1)
    p_f = apow[0][:, rev][:, :, :, None] * b_bar[0][:, None]
    p_b = apow[1][:, :lc][:, :, :, None] * b_bar[1][:, None]
    p_op = jnp.stack([jnp.real(p_f), jnp.imag(p_f), jnp.real(p_b), jnp.imag(p_b)], axis=0)
    p_op = p_op.transpose(1, 2, 4, 0, 3).reshape(n_groups, lc * i_dim, 4, p_dim)
    q_f = cmat[:, None] * apow[0][:, 1:lc + 1][:, :, None, :]
    q_b = cmat[:, None] * apow[1][:, lc - jnp.arange(lc)][:, :, None, :]
    q_op = jnp.stack([jnp.real(q_f), -jnp.imag(q_f), jnp.real(q_b), -jnp.imag(q_b)], axis=0)
    q_op = q_op.transpose(1, 0, 4, 2, 3).reshape(n_groups, 4, p_dim, lc * i_dim)
    n_pairs = n_groups // 2
    eye = jnp.eye(2, dtype=F32)
    w = lc * i_dim
    m_pair = m_op.reshape(n_pairs, 2, w, w)
    p_pair = jnp.einsum('narqp,ab->narqbp', p_op.reshape(n_pairs, 2, w, 4, p_dim), eye)
    p_pair = p_pair.reshape(n_pairs, 2 * w, 8 * p_dim)
    q_pair = jnp.einsum('nbqpc,ab->nqbpac', q_op.reshape(n_pairs, 2, 4, p_dim, w), eye)
    q_pair = q_pair.reshape(n_pairs, 8 * p_dim, 2 * w)
    lam_chunk = (lam * lc).reshape(2, n_groups // 8, 8 * p_dim)
    tiles = n_groups // 8
    return (m_pair.astype(BF16).reshape(tiles, 4, 2, w, w), p_pair.astype(BF16).reshape(tiles, 4, 2 * w, 8 * p_dim),
            q_pair.astype(BF16).reshape(tiles, 4, 8 * p_dim, 2 * w), lam_chunk)


def _cmul(ar, ai, xr, xi):
    return ar * xr - ai * xi, ar * xi + ai * xr


def _s5_body(h_ref, perm_ref, m_ref, p_ref, q_ref, pw_ref, d_ref, o_ref, u_scr, uc_scr, v_scr, s_scr, yc_scr):
    n_seq, seq = h_ref.shape[0], h_ref.shape[1]
    lc = S5_CHUNK
    n_chunks = seq // lc
    ns = n_chunks // S5_SEGMENTS
    n_seg = n_seq * S5_SEGMENTS
    n_pairs = m_ref.shape[1]
    pw = p_ref.shape[2]
    sw = pw // 4
    qw = n_pairs * sw

    for s in range(n_seg):
        b, sg = divmod(s, S5_SEGMENTS)
        for t in range(lc):
            u_scr[t, pl.ds(s, ns, stride=n_seg), :] = h_ref[b, pl.ds(lc * sg * ns + t, ns, stride=lc), :]
    u_all = jnp.concatenate([u_scr[t].astype(BF16) for t in range(lc)], axis=1)
    uc_scr[...] = jnp.dot(u_all, perm_ref[...], preferred_element_type=F32).astype(BF16)
    for p in range(n_pairs):
        vp = jnp.dot(uc_scr[:, p * pw:(p + 1) * pw], p_ref[0, p], preferred_element_type=F32)
        for c in range(4):
            v_scr[:, c * qw + p * sw:c * qw + (p + 1) * sw] = vp[:, c * sw:(c + 1) * sw]

    a_f = (pw_ref[0, 0, 1:2, :], pw_ref[0, 1, 1:2, :])
    a_b = (pw_ref[0, 2, 1:2, :], pw_ref[0, 3, 1:2, :])
    zero = jnp.zeros((n_seg, qw), F32)

    def scan(k, carry):
        f_re, f_im, b_re, b_im = carry
        rf = pl.ds(pl.multiple_of(k * n_seg, n_seg), n_seg)
        s_scr[rf, 0:qw] = f_re
        s_scr[rf, qw:2 * qw] = f_im
        n_re, n_im = _cmul(*a_f, f_re, f_im)
        rb = pl.ds(pl.multiple_of((ns - 1 - k) * n_seg, n_seg), n_seg)
        s_scr[rb, 2 * qw:3 * qw] = b_re
        s_scr[rb, 3 * qw:4 * qw] = b_im
        m_re, m_im = _cmul(*a_b, b_re, b_im)
        return (n_re + v_scr[rf, 0:qw], n_im + v_scr[rf, qw:2 * qw],
                m_re + v_scr[rb, 2 * qw:3 * qw], m_im + v_scr[rb, 3 * qw:4 * qw])

    f_re, f_im, b_re, b_im = lax.fori_loop(0, ns, scan, (zero, zero, zero, zero))

    a_seg_f = (pw_ref[0, 0, ns:ns + 1, :], pw_ref[0, 1, ns:ns + 1, :])
    a_seg_b = (pw_ref[0, 2, ns:ns + 1, :], pw_ref[0, 3, ns:ns + 1, :])
    row0 = jnp.zeros((1, qw), F32)
    cf = []
    for s in range(n_seg):
        if s % S5_SEGMENTS == 0:
            cf.append((row0, row0))
        else:
            xr, xi = _cmul(*a_seg_f, *cf[-1])
            cf.append((xr + f_re[s - 1:s, :], xi + f_im[s - 1:s, :]))
    cb = []
    for s in range(n_seg - 1, -1, -1):
        if s % S5_SEGMENTS == S5_SEGMENTS - 1:
            cb.insert(0, (row0, row0))
        else:
            xr, xi = _cmul(*a_seg_b, *cb[0])
            cb.insert(0, (xr + b_re[s + 1:s + 2, :], xi + b_im[s + 1:s + 2, :]))
    cf_re, cf_im = (jnp.concatenate([c[j] for c in cf], axis=0) for j in range(2))
    cb_re, cb_im = (jnp.concatenate([c[j] for c in cb], axis=0) for j in range(2))

    def correct(k, carry):
        rf = pl.ds(pl.multiple_of(k * n_seg, n_seg), n_seg)
        xr, xi = _cmul(pw_ref[0, 0, pl.ds(k, 1), :], pw_ref[0, 1, pl.ds(k, 1), :], cf_re, cf_im)
        s_scr[rf, 0:qw] += xr
        s_scr[rf, qw:2 * qw] += xi
        kb = ns - 1 - k
        yr, yi = _cmul(pw_ref[0, 2, pl.ds(kb, 1), :], pw_ref[0, 3, pl.ds(kb, 1), :], cb_re, cb_im)
        s_scr[rf, 2 * qw:3 * qw] += yr
        s_scr[rf, 3 * qw:4 * qw] += yi
        return carry

    lax.fori_loop(0, ns, correct, 0)

    half = pw // 2
    for p in range(n_pairs):
        uc = uc_scr[:, p * pw:(p + 1) * pw]
        st = jnp.concatenate([s_scr[:, c * qw + p * sw:c * qw + (p + 1) * sw] for c in range(4)], axis=1)
        y = jnp.dot(st.astype(BF16), q_ref[0, p], preferred_element_type=F32)
        y = y + jnp.concatenate([jnp.dot(uc[:, :half], m_ref[0, p, 0], preferred_element_type=F32),
                                 jnp.dot(uc[:, half:], m_ref[0, p, 1], preferred_element_type=F32)], axis=1)
        yc_scr[:, p * pw:(p + 1) * pw] = y.astype(BF16)
    y_all = _dot_nt(yc_scr[...], perm_ref[...])
    for t in range(lc):
        u_scr[t] = y_all[:, t * LANES:(t + 1) * LANES]
    for s in range(n_seg):
        b, sg = divmod(s, S5_SEGMENTS)
        for t in range(lc):
            rows = pl.ds(lc * sg * ns + t, ns, stride=lc)
            y = u_scr[t, pl.ds(s, ns, stride=n_seg), :] + d_ref[...] * h_ref[b, rows, :]
            o_ref[b, rows, :] = jax.nn.gelu(y)


def s5_mix(h3, ops, d_skip):
    m_op, p_op, q_op, lam_chunk = ops
    bsz, seq, d = h3.shape
    lc = S5_CHUNK
    tiles = d // LANES
    n_chunks = seq // lc
    width = lc * LANES
    assert seq % (lc * S5_SEGMENTS) == 0 and m_op.shape[0] == tiles
    ns = n_chunks // S5_SEGMENTS
    nb = math.gcd(bsz, max(1, S5_ROWS // n_chunks))
    rows = nb * n_chunks
    powers = jnp.exp(lam_chunk[:, :, None, :] * jnp.arange(ns + 1, dtype=F32)[None, None, :, None])
    pw_tab = jnp.stack([jnp.real(powers[0]), jnp.imag(powers[0]), jnp.real(powers[1]), jnp.imag(powers[1])], axis=1)
    src = jnp.arange(width)
    t_idx, g_idx, i_idx = src // LANES, (src % LANES) // S5_GROUP, src % S5_GROUP
    perm = (jnp.arange(width)[None, :] == (g_idx * (lc * S5_GROUP) + t_idx * S5_GROUP + i_idx)[:, None]).astype(BF16)
    return pl.pallas_call(
        _s5_body,
        grid=(tiles, bsz // nb),
        in_specs=[
            pl.BlockSpec((nb, seq, LANES), lambda l, b: (b, 0, l)),
            pl.BlockSpec((width, width), lambda l, b: (0, 0)),
            pl.BlockSpec((1,) + m_op.shape[1:], lambda l, b: (l, 0, 0, 0, 0)),
            pl.BlockSpec((1,) + p_op.shape[1:], lambda l, b: (l, 0, 0, 0)),
            pl.BlockSpec((1,) + q_op.shape[1:], lambda l, b: (l, 0, 0, 0)),
            pl.BlockSpec((1,) + pw_tab.shape[1:], lambda l, b: (l, 0, 0, 0)),
            pl.BlockSpec((1, LANES), lambda l, b: (0, l)),
        ],
        out_specs=pl.BlockSpec((nb, seq, LANES), lambda l, b: (b, 0, l)),
        out_shape=jax.ShapeDtypeStruct((bsz, seq, d), F32),
        scratch_shapes=[pltpu.VMEM((lc, rows, LANES), F32), pltpu.VMEM((rows, width), BF16),
                        pltpu.VMEM((rows, width), F32), pltpu.VMEM((rows, width), F32),
                        pltpu.VMEM((rows, width), BF16)],
        compiler_params=_cparams("parallel", "parallel"),
        name="s5_mix",
    )(h3, perm, m_op, p_op, q_op, pw_tab, d_skip.astype(F32)[None, :])


def _threshold_body(a_ref, thr_ref, cgt_ref, *, capacity):
    def bits():
        return lax.bitcast_convert_type(a_ref[...], jnp.int32)

    def step(i, thr):
        cand = thr | jnp.left_shift(jnp.int32(1), 30 - i)
        cnt = jnp.sum((bits() >= cand).astype(jnp.int32), axis=1, keepdims=True)
        return jnp.where(cnt >= capacity, cand, thr)

    thr = lax.fori_loop(0, 31, step, jnp.zeros((a_ref.shape[0], 1), jnp.int32))
    cgt = jnp.sum((bits() > thr).astype(jnp.int32), axis=1, keepdims=True)
    thr_ref[...] = jnp.broadcast_to(lax.bitcast_convert_type(thr, F32), thr_ref.shape)
    cgt_ref[...] = jnp.broadcast_to(cgt, cgt_ref.shape)


def expert_thresholds(aff_t, capacity):
    rows, n = aff_t.shape
    return pl.pallas_call(
        functools.partial(_threshold_body, capacity=capacity),
        grid=(1,),
        in_specs=[pl.BlockSpec((rows, n), lambda i: (0, 0))],
        out_specs=[pl.BlockSpec((rows, LANES), lambda i: (0, 0))] * 2,
        out_shape=[jax.ShapeDtypeStruct((rows, LANES), F32), jax.ShapeDtypeStruct((rows, LANES), jnp.int32)],
        compiler_params=_cparams("arbitrary"),
        name="expert_thresholds",
    )(aff_t)


def _tile_counts_body(a_ref, thr_ref, cgt_ref, ceq_ref):
    t = pl.program_id(0)
    a, thr = a_ref[...], thr_ref[...]
    cgt_ref[pl.ds(t, 1), :] = jnp.sum((a > thr).astype(jnp.int32), axis=0, keepdims=True)
    ceq_ref[pl.ds(t, 1), :] = jnp.sum((a == thr).astype(jnp.int32), axis=0, keepdims=True)


def tile_counts(aff, thr_l, tm):
    n = aff.shape[0]
    n_tiles = n // tm
    return pl.pallas_call(
        _tile_counts_body,
        grid=(n_tiles,),
        in_specs=[pl.BlockSpec((tm, LANES), lambda t: (t, 0)), pl.BlockSpec((1, LANES), lambda t: (0, 0))],
        out_specs=[pl.BlockSpec((n_tiles, LANES), lambda t: (0, 0))] * 2,
        out_shape=[jax.ShapeDtypeStruct((n_tiles, LANES), jnp.int32)] * 2,
        compiler_params=_cparams("arbitrary"),
        name="tile_counts",
    )(aff, thr_l)


def _slots_body(a_ref, thr_ref, need_ref, eqb_ref, selb_ref, slot_ref, slott_ref, *, n_exp):
    t = pl.program_id(0)
    tm = a_ref.shape[0]
    a, thr = a_ref[...], thr_ref[...]
    lane = lax.broadcasted_iota(jnp.int32, (1, LANES), 1)
    row = lax.broadcasted_iota(jnp.int32, (tm, tm), 0)
    col = lax.broadcasted_iota(jnp.int32, (tm, tm), 1)
    before = (row > col).astype(BF16)
    eq = a == thr
    eq_rank = jnp.dot(before, eq.astype(BF16), preferred_element_type=F32) + eqb_ref[pl.ds(t, 1), :].astype(F32)
    sel = ((a > thr) | (eq & (eq_rank < need_ref[...].astype(F32)))) & (lane < n_exp)
    pos = jnp.dot(before, sel.astype(BF16), preferred_element_type=F32) + selb_ref[pl.ds(t, 1), :].astype(F32)
    slot = jnp.where(sel, pos, -1.0)
    slot_ref[...] = slot.astype(jnp.int32)
    slott_ref[...] = slot.T[:EXPERT_ROWS, :].astype(jnp.int32)


def token_slots(aff, thr_l, need_l, eq_base, sel_base, n_exp, tm):
    n = aff.shape[0]
    n_tiles = n // tm
    full = lambda r: pl.BlockSpec((r, LANES), lambda t: (0, 0))
    return pl.pallas_call(
        functools.partial(_slots_body, n_exp=n_exp),
        grid=(n_tiles,),
        in_specs=[pl.BlockSpec((tm, LANES), lambda t: (t, 0)), full(1), full(1), full(n_tiles), full(n_tiles)],
        out_specs=[pl.BlockSpec((tm, LANES), lambda t: (t, 0)), pl.BlockSpec((EXPERT_ROWS, tm), lambda t: (0, t))],
        out_shape=[jax.ShapeDtypeStruct((n, LANES), jnp.int32), jax.ShapeDtypeStruct((EXPERT_ROWS, n), jnp.int32)],
        compiler_params=_cparams("parallel"),
        name="token_slots",
    )(aff, thr_l, need_l, eq_base, sel_base)


def _window_start(s):
    return pl.multiple_of((s // ROW_ALIGN) * ROW_ALIGN, ROW_ALIGN)


def _dispatch_body(base_ref, cnt_ref, slott_ref, h_ref, xe_ref, stage, extra, carry, sems, xsem, *, n_exp, capacity):
    t = pl.program_id(0)
    n_tiles = pl.num_programs(0)
    par = t % 2
    wc = stage.shape[3]
    sub = lax.broadcasted_iota(jnp.int32, (wc, 1), 0)

    @pl.when(t == 0)
    def _():
        carry[...] = jnp.zeros_like(carry)

    def n_windows(tt, e):
        s = base_ref[tt * n_exp + e]
        return (_window_start(s + cnt_ref[tt * n_exp + e]) - _window_start(s)) // wc + 1

    starts = [_window_start(base_ref[t * n_exp + e]) for e in range(n_exp)]
    onehot = jnp.concatenate([(slott_ref[e:e + 1, :] - starts[e] == sub) for e in range(n_exp)], axis=0)
    rows = jnp.dot(onehot.astype(BF16), h_ref[...], preferred_element_type=F32)

    def window_copy(e, k):
        return pltpu.make_async_copy(stage.at[k, par, e], xe_ref.at[e, pl.ds(starts[e] + k * wc, wc)], sems.at[k, e])

    def wait_tile(tt, slot_par):
        for e in range(n_exp):
            pltpu.make_async_copy(stage.at[0, slot_par, e], xe_ref.at[e, pl.ds(0, wc)], sems.at[0, e]).wait()

            @pl.when(n_windows(tt, e) > 1)
            def _(e=e):
                pltpu.make_async_copy(stage.at[1, slot_par, e], xe_ref.at[e, pl.ds(0, wc)], sems.at[1, e]).wait()

    @pl.when(t > 0)
    def _():
        wait_tile(t - 1, 1 - par)

    def spill(e, k):
        oh = slott_ref[e:e + 1, :] - (starts[e] + k * wc) == sub
        return jnp.dot(oh.astype(BF16), h_ref[...], preferred_element_type=F32).astype(BF16)

    for e in range(n_exp):
        s = base_ref[t * n_exp + e]
        end16 = _window_start(s + cnt_ref[t * n_exp + e])
        n_win = n_windows(t, e)
        stage[0, par, e] = rows[e * wc:(e + 1) * wc].astype(BF16)
        stage[0, par, e, 0:ROW_ALIGN, :] += carry[e]
        window_copy(e, 0).start()

        @pl.when(n_win > 1)
        def _(e=e):
            stage[1, par, e] = spill(e, 1)
            window_copy(e, 1).start()

        def more(k, c, e=e):
            extra[...] = spill(e, k)
            cp = pltpu.make_async_copy(extra, xe_ref.at[e, pl.ds(starts[e] + k * wc, wc)], xsem)
            cp.start()
            cp.wait()
            return c

        lax.fori_loop(2, n_win, more, 0)
        off = pl.multiple_of(end16 - starts[e] - (n_win - 1) * wc, ROW_ALIGN)

        @pl.when(n_win == 1)
        def _(e=e, off=off):
            carry[e] = stage[0, par, e, pl.ds(off, ROW_ALIGN), :]

        @pl.when(n_win == 2)
        def _(e=e, off=off):
            carry[e] = stage[1, par, e, pl.ds(off, ROW_ALIGN), :]

        @pl.when(n_win > 2)
        def _(e=e, off=off):
            carry[e] = extra[pl.ds(off, ROW_ALIGN), :]

    @pl.when(t == n_tiles - 1)
    def _():
        wait_tile(t, par)
        extra[...] = jnp.zeros_like(extra)
        for e in range(n_exp):
            cp = pltpu.make_async_copy(extra, xe_ref.at[e, pl.ds(capacity, wc)], xsem)
            cp.start()
            cp.wait()


def dispatch_rows(h, slot_t, base, cnt, n_exp, capacity, tm):
    n, d = h.shape
    n_tiles = n // tm
    wc = DISPATCH_WINDOW
    grid_spec = pltpu.PrefetchScalarGridSpec(
        num_scalar_prefetch=2,
        grid=(n_tiles,),
        in_specs=[pl.BlockSpec((EXPERT_ROWS, tm), lambda t, b, c: (0, t)),
                  pl.BlockSpec((tm, d), lambda t, b, c: (t, 0))],
        out_specs=pl.BlockSpec(memory_space=pl.ANY),
        scratch_shapes=[pltpu.VMEM((2, 2, n_exp, wc, d), BF16), pltpu.VMEM((wc, d), BF16),
                        pltpu.VMEM((n_exp, ROW_ALIGN, d), BF16),
                        pltpu.SemaphoreType.DMA((2, n_exp)), pltpu.SemaphoreType.DMA(())],
    )
    return pl.pallas_call(
        functools.partial(_dispatch_body, n_exp=n_exp, capacity=capacity),
        grid_spec=grid_spec,
        out_shape=jax.ShapeDtypeStruct((n_exp, capacity + wc, d), BF16),
        compiler_params=_cparams("arbitrary"),
        name="dispatch_rows",
    )(base, cnt, slot_t, h)


def _expert_body(x_ref, wg_ref, wu_ref, wd_ref, o_ref, *, tf):
    x = x_ref[0]
    acc = None
    for f0 in range(0, wg_ref.shape[2], tf):
        a = jnp.dot(x, wg_ref[0, :, f0:f0 + tf], preferred_element_type=F32)
        u = jnp.dot(x, wu_ref[0, :, f0:f0 + tf], preferred_element_type=F32)
        hid = (a * jax.nn.sigmoid(a) * u).astype(BF16)
        part = jnp.dot(hid, wd_ref[0, f0:f0 + tf, :], preferred_element_type=F32)
        acc = part if acc is None else acc + part
    o_ref[0] = acc.astype(o_ref.dtype)


def expert_ffn(xe, w_gate, w_up, w_down, capacity, tf=1024):
    e, _, d = xe.shape
    f = w_gate.shape[2]
    tm = math.gcd(capacity, ROW_TILE)
    tf = math.gcd(f, tf)
    return pl.pallas_call(
        functools.partial(_expert_body, tf=tf),
        grid=(e, capacity // tm),
        in_specs=[
            pl.BlockSpec((1, tm, d), lambda ei, i: (ei, i, 0)),
            pl.BlockSpec((1, d, f), lambda ei, i: (ei, 0, 0)),
            pl.BlockSpec((1, d, f), lambda ei, i: (ei, 0, 0)),
            pl.BlockSpec((1, f, d), lambda ei, i: (ei, 0, 0)),
        ],
        out_specs=pl.BlockSpec((1, tm, d), lambda ei, i: (ei, i, 0)),
        out_shape=jax.ShapeDtypeStruct((e, capacity, d), BF16),
        compiler_params=_cparams("parallel", "parallel"),
        name="expert_ffn",
    )(xe, w_gate, w_up, w_down)


def _combine_body(base_ref, cnt_ref, slot_ref, aff_ref, x_ref, gate_ref, fg_ref, ye_ref, o_ref,
                  win, extra, acc, sems, xsem, *, n_exp, capacity, final_norm):
    t = pl.program_id(0)
    n_tiles = pl.num_programs(0)
    par = t % 2
    wc = win.shape[3]
    lanes = lax.broadcasted_iota(jnp.int32, (1, wc), 1)

    def start_of(tt, e, k):
        lo = _window_start(base_ref[tt * n_exp + e]) + k * wc
        return lo, pl.multiple_of(jnp.minimum(lo, capacity - wc), ROW_ALIGN)

    def fetch(tt, slot_par, e, k):
        _, st = start_of(tt, e, k)
        return pltpu.make_async_copy(ye_ref.at[e, pl.ds(st, wc)], win.at[k, slot_par, e], sems.at[k, slot_par, e])

    @pl.when(t == 0)
    def _():
        for e in range(n_exp):
            fetch(0, 0, e, 0).start()
            fetch(0, 0, e, 1).start()

    @pl.when(t + 1 < n_tiles)
    def _():
        for e in range(n_exp):
            fetch(t + 1, 1 - par, e, 0).start()
            fetch(t + 1, 1 - par, e, 1).start()

    def spread(e, lo, st):
        col = slot_ref[:, e:e + 1]
        return jnp.where((col - st == lanes) & (col >= lo), aff_ref[:, e:e + 1], 0.0).astype(BF16)

    src_e = lax.broadcasted_iota(jnp.int32, (LANES, n_exp * wc), 0)
    dst_e = lax.broadcasted_iota(jnp.int32, (LANES, n_exp * wc), 1) // wc
    expand = (src_e == dst_e).astype(BF16)
    s1 = slot_ref[...] + 1
    parts = jnp.concatenate([(s1 // 64).astype(BF16), (s1 % 64).astype(BF16)], axis=1)
    slot_rep = jnp.dot(parts, jnp.concatenate([expand * 64, expand], axis=0), preferred_element_type=F32) - 1.0
    aff_rep = jnp.dot(aff_ref[...].astype(BF16), expand, preferred_element_type=F32)
    lo_vec = jnp.concatenate([jnp.full((1, wc), start_of(t, e, 0)[0], jnp.int32) for e in range(n_exp)], axis=1)
    st_vec = jnp.concatenate([jnp.full((1, wc), start_of(t, e, 0)[1], jnp.int32) for e in range(n_exp)], axis=1)
    lane_in_win = lax.broadcasted_iota(jnp.int32, (1, n_exp * wc), 1) % wc
    hit = (slot_rep == (st_vec + lane_in_win).astype(F32)) & (slot_rep >= lo_vec.astype(F32))
    onehot = jnp.where(hit, aff_rep, 0.0).astype(BF16)
    for e in range(n_exp):
        fetch(t, par, e, 0).wait()
    acc[...] = jnp.dot(onehot, win[0, par].reshape(n_exp * wc, win.shape[4]), preferred_element_type=F32)
    for e in range(n_exp):
        s = base_ref[t * n_exp + e]
        n_win = (s - _window_start(s) + cnt_ref[t * n_exp + e] + wc - 1) // wc
        fetch(t, par, e, 1).wait()

        @pl.when(n_win > 1)
        def _(e=e):
            lo_1, st_1 = start_of(t, e, 1)
            acc[...] += jnp.dot(spread(e, lo_1, st_1), win[1, par, e], preferred_element_type=F32)

        def more(k, c, e=e):
            lo_k, st_k = start_of(t, e, k)
            cp = pltpu.make_async_copy(ye_ref.at[e, pl.ds(st_k, wc)], extra, xsem)
            cp.start()
            cp.wait()
            acc[...] += jnp.dot(spread(e, lo_k, st_k), extra[...], preferred_element_type=F32)
            return c

        lax.fori_loop(2, n_win, more, 0)

    x = x_ref[...] + gate_ref[0] * acc[...]
    if final_norm:
        x = x * lax.rsqrt(jnp.mean(x * x, axis=-1, keepdims=True) + EPS) * fg_ref[...]
    o_ref[...] = x


def combine_rows(slot, aff, x2, gate, final_g, ye, base, cnt, seq, n_exp, capacity, tm, final_norm):
    n, d = x2.shape
    n_tiles = n // tm
    wc = DISPATCH_WINDOW
    assert wc <= capacity <= 64 * 256 and capacity % ROW_ALIGN == 0 and seq % tm == 0
    grid_spec = pltpu.PrefetchScalarGridSpec(
        num_scalar_prefetch=2,
        grid=(n_tiles,),
        in_specs=[pl.BlockSpec((tm, LANES), lambda t, b, c: (t, 0)),
                  pl.BlockSpec((tm, LANES), lambda t, b, c: (t, 0)),
                  pl.BlockSpec((tm, d), lambda t, b, c: (t, 0)),
                  pl.BlockSpec((1, 1, d), lambda t, b, c: ((t * tm) // seq, 0, 0)),
                  pl.BlockSpec((1, d), lambda t, b, c: (0, 0)),
                  pl.BlockSpec(memory_space=pl.ANY)],
        out_specs=pl.BlockSpec((tm, d), lambda t, b, c: (t, 0)),
        scratch_shapes=[pltpu.VMEM((2, 2, n_exp, wc, d), BF16), pltpu.VMEM((wc, d), BF16), pltpu.VMEM((tm, d), F32),
                        pltpu.SemaphoreType.DMA((2, 2, n_exp)), pltpu.SemaphoreType.DMA(())],
    )
    return pl.pallas_call(
        functools.partial(_combine_body, n_exp=n_exp, capacity=capacity, final_norm=final_norm),
        grid_spec=grid_spec,
        out_shape=jax.ShapeDtypeStruct((n, d), F32),
        compiler_params=_cparams("arbitrary"),
        name="combine_rows",
    )(base, cnt, slot, aff, x2, gate, final_g, ye)


def expert_choice_moe(x2, seq, norm_g, sc, sh, gate2, final_g, w_router, w_gate, w_up, w_down, final_norm):
    n, d = x2.shape
    n_exp = w_router.shape[1]
    capacity = CAPACITY_FACTOR * n // n_exp
    tm = _row_tile(seq, ROUTE_TILE)
    h, aff, aff_t = norm_router(x2, seq, norm_g, sc, sh, w_router)
    thr, cgt = expert_thresholds(aff_t, capacity)
    pad = LANES - thr.shape[0]
    thr_l = jnp.pad(thr[:, 0], (0, pad))[None, :]
    need_l = jnp.pad(capacity - cgt[:, 0], (0, pad))[None, :]
    t_gt, t_eq = tile_counts(aff, thr_l, tm)
    eq_base = jnp.cumsum(t_eq, axis=0) - t_eq
    t_sel = t_gt + jnp.clip(need_l - eq_base, 0, t_eq)
    sel_base = jnp.cumsum(t_sel, axis=0) - t_sel
    slot, slot_t = token_slots(aff, thr_l, need_l, eq_base, sel_base, n_exp, tm)
    base = sel_base[:, :n_exp].reshape(-1)
    cnt = t_sel[:, :n_exp].reshape(-1)
    xe = dispatch_rows(h, slot_t, base, cnt, n_exp, capacity, tm)
    ye = expert_ffn(xe, w_gate, w_up, w_down, capacity)
    return combine_rows(slot, aff, x2, gate2, final_g, ye, base, cnt, seq, n_exp, capacity, tm, final_norm)


def _trunk(x, c, params):
    (norm_mix_g, norm_ffn_g, ada_w, ada_b, hg_w_in, hg_w_out, hg_norm_g, lb_table,
     da_w_qkv, da_w_out, s5_ops, s5_d, s5_w_glu, moe_w_router, moe_w_gate, moe_w_up, moe_w_down, final_g) = params
    bsz, seq, d = x.shape
    depth = norm_mix_g.shape[0]
    x2 = x.reshape(bsz * seq, d)
    cond = jax.nn.silu(c)
    for layer in range(depth):
        kind, slot = layer % N_MIXERS, layer // N_MIXERS
        mod = (jnp.dot(cond, ada_w[layer], precision=HIGHEST) + ada_b[layer])[:, None, :]
        sh1, sc1, g1, sh2, sc2, g2 = jnp.split(mod, 6, axis=-1)
        gmix = norm_mix_g[layer][None, :]
        if kind == 0:
            proj = nm_matmul(x2, seq, gmix, sc1, sh1, hg_w_in[slot])
            lb = lb_table[layer].reshape(HG_HEADS, 1, -1)
            m = hgrn_recurrence(proj.reshape(bsz, seq, -1), lb, jnp.log1p(-lb), 1.0 - lb,
                                hg_norm_g[slot][None, :].astype(F32))
            x2 = proj_residual(m.reshape(bsz * seq, d), hg_w_out[slot], x2, g1, seq)
        elif kind == 1:
            proj = nm_matmul(x2, seq, gmix, sc1, sh1, da_w_qkv[slot])
            m = dilated_attention(proj.reshape(bsz, seq, -1))
            x2 = proj_residual(m.reshape(bsz * seq, d), da_w_out[slot], x2, g1, seq)
        else:
            h = norm_mod(x2, seq, gmix, sc1, sh1)
            z = s5_mix(h.reshape(bsz, seq, d), s5_ops[slot], s5_d[slot])
            x2 = proj_residual(z.reshape(bsz * seq, d), s5_w_glu[slot], x2, g1, seq, glu=True)
        x2 = expert_choice_moe(x2, seq, norm_ffn_g[layer][None, :], sc2, sh2, g2, final_g[None, :].astype(F32),
                               moe_w_router[layer], moe_w_gate[layer], moe_w_up[layer], moe_w_down[layer],
                               final_norm=(layer == depth - 1))
    return x2.reshape(bsz, seq, d)


def kernel(x_prompt, x_sample, c_prompt, c_sample, norm_mix_g, norm_ffn_g, ada_w, ada_b, hg_w_in, hg_w_out, hg_norm_g, hg_lb_logits, da_w_qkv, da_w_out, s5_a_re, s5_a_im, s5_log_dt, s5_b_re, s5_b_im, s5_c_re, s5_c_im, s5_d, s5_w_glu, moe_w_router, moe_w_gate, moe_w_up, moe_w_down, final_g):
    lb_table = jnp.cumsum(jax.nn.softmax(hg_lb_logits.astype(F32), axis=0), axis=0)
    lb_table = lb_table - lb_table[0:1]
    s5_ops = [_s5_operators(s5_a_re[s], s5_a_im[s], s5_log_dt[s], s5_b_re[s], s5_b_im[s],
                            s5_c_re[s], s5_c_im[s]) for s in range(s5_a_re.shape[0])]
    bf = lambda w: w.astype(BF16)
    params = (norm_mix_g.astype(F32), norm_ffn_g.astype(F32), ada_w, ada_b, bf(hg_w_in), bf(hg_w_out),
              hg_norm_g, lb_table, bf(da_w_qkv), bf(da_w_out), s5_ops, s5_d, bf(s5_w_glu), moe_w_router,
              bf(moe_w_gate), bf(moe_w_up), bf(moe_w_down), final_g)
    return (_trunk(x_prompt, c_prompt, params), _trunk(x_sample, c_sample, params))
```
